```python
import math
import jax, jax.numpy as jnp
from jax import lax
import numpy as np

D_MODEL = 1024
BATCH = 32
SEQ = 256
DEPTH = 4
DEC_BATCH = 2
DEC_SEQ = 1024
PAST_LEN = 256

GRID_W = 64
N_MIXERS = 2
N_DIFF = (DEPTH + 1) // 2
N_MLA = DEPTH // 2
N_MOD = 9
D_FF = 2816
DIFF_HEADS = 8
DIFF_HD = 64
DIFF_VD = 2 * DIFF_HD
MLA_HEADS = 16
MLA_NOPE = 64
MLA_ROPE = 32
MLA_QK = MLA_NOPE + MLA_ROPE
MLA_VD = 64
Q_RANK = 768
KV_RANK = 256
ROPE_THETA = 10000.0
EPS = 1e-6
Q_BLOCK = 128
DIFF_SCALE = DIFF_HD ** -0.5
MLA_SCALE = MLA_QK ** -0.5

kernel_name = "diffmla_macaron_adaln_prefix_step"


def rms_norm(x, g):
    x32 = x.astype(jnp.float32)
    y = x32 * lax.rsqrt(jnp.mean(x32 * x32, axis=-1, keepdims=True) + EPS)
    return y.astype(x.dtype) * g


def grid_positions(n_tokens):
    rows = n_tokens // GRID_W
    row = jnp.repeat(jnp.arange(rows, dtype=jnp.int32), GRID_W)
    col = jnp.tile(jnp.arange(GRID_W, dtype=jnp.int32), rows)
    return row, col


def _rope_half(x, pos):
    n = x.shape[-1]
    freqs = ROPE_THETA ** (-jnp.arange(0, n, 2, dtype=jnp.float32) / n)
    ang = pos.astype(jnp.float32)[:, None] * freqs[None, :]
    shape = (1, pos.shape[0]) + (1,) * (x.ndim - 3) + (n // 2,)
    cos = jnp.cos(ang).reshape(shape).astype(x.dtype)
    sin = jnp.sin(ang).reshape(shape).astype(x.dtype)
    x1, x2 = x[..., : n // 2], x[..., n // 2:]
    return jnp.concatenate([x1 * cos - x2 * sin, x2 * cos + x1 * sin], axis=-1)


def rope_2d(x, row, col):
    r = x.shape[-1] // 2
    return jnp.concatenate([_rope_half(x[..., :r], row), _rope_half(x[..., r:], col)], axis=-1)


def rope_tail(x, row, col):
    return jnp.concatenate([x[..., :MLA_NOPE], rope_2d(x[..., MLA_NOPE:], row, col)], axis=-1)


def sweep_query_blocks(fn, *qs):
    B, L = qs[0].shape[:2]
    nb = L // Q_BLOCK
    blocks = tuple(jnp.moveaxis(q.reshape((B, nb, Q_BLOCK) + q.shape[2:]), 1, 0) for q in qs)
    out = lax.map(lambda qb: fn(*qb), blocks)
    return jnp.moveaxis(out, 0, 1).reshape((B, L) + out.shape[3:])


def softmax_f32(s, scale):
    return jax.nn.softmax(s.astype(jnp.float32) * scale, axis=-1)


def diff_attention(q, k, v, lam, scale):
    k1, k2 = k[..., 0, :], k[..., 1, :]

    def block(qb):
        a1 = softmax_f32(jnp.einsum('bqhd,bkhd->bhqk', qb[..., 0, :], k1), scale)
        a2 = softmax_f32(jnp.einsum('bqhd,bkhd->bhqk', qb[..., 1, :], k2), scale)
        a = a1 - lam * a2
        return jnp.einsum('bhqk,bkhv->bqhv', a.astype(v.dtype), v)

    return sweep_query_blocks(block, q)


def mha(q, k, v, scale):
    def block(qb):
        a = softmax_f32(jnp.einsum('bqhd,bkhd->bhqk', qb, k), scale)
        return jnp.einsum('bhqk,bkhv->bqhv', a.astype(v.dtype), v)

    return sweep_query_blocks(block, q)


def diff_qkv(h, w_qkv, q_g, k_g):
    B, L, _ = h.shape
    q, k, v = jnp.split(h @ w_qkv, 3, axis=-1)
    q = rms_norm(q.reshape(B, L, DIFF_HEADS, 2, DIFF_HD), q_g)
    k = rms_norm(k.reshape(B, L, DIFF_HEADS, 2, DIFF_HD), k_g)
    v = v.reshape(B, L, DIFF_HEADS, DIFF_VD)
    return q, k, v


def lambda_value(lam_p, layer_idx):
    lam_init = 0.8 - 0.6 * math.exp(-0.3 * layer_idx)
    lp = lam_p.astype(jnp.float32)
    lam = jnp.exp(jnp.sum(lp[0] * lp[1])) - jnp.exp(jnp.sum(lp[2] * lp[3])) + lam_init
    return lam, lam_init


def diff_out(o, lam_init, subln_g, w_o):
    B, L = o.shape[:2]
    o = rms_norm(o, subln_g) * (1.0 - lam_init)
    return o.reshape(B, L, DIFF_HEADS * DIFF_VD) @ w_o


def mla_down(h, w_down, qa_g, kva_g):
    cq, ckv, kpe = jnp.split(h @ w_down, [Q_RANK, Q_RANK + KV_RANK], axis=-1)
    return rms_norm(cq, qa_g), rms_norm(ckv, kva_g), kpe


def mla_queries(cq, w_q_up, q_g):
    B, L = cq.shape[:2]
    return rms_norm((cq @ w_q_up).reshape(B, L, MLA_HEADS, MLA_QK), q_g)


def mla_keys_values(ckv, kpe, w_kv_up, k_g):
    B, L = ckv.shape[:2]
    kv = (ckv @ w_kv_up).reshape(B, L, MLA_HEADS, MLA_NOPE + MLA_VD)
    k_nope, v = kv[..., :MLA_NOPE], kv[..., MLA_NOPE:]
    k_pe = jnp.broadcast_to(kpe[:, :, None, :], (B, L, MLA_HEADS, MLA_ROPE))
    k = rms_norm(jnp.concatenate([k_nope, k_pe], axis=-1), k_g)
    return k, v


def mla_out(o, w_o):
    B, L = o.shape[:2]
    return o.reshape(B, L, MLA_HEADS * MLA_VD) @ w_o


def modulation(cond, w_mod, b_mod):
    m = jax.nn.silu(cond) @ w_mod + b_mod
    return m.reshape(cond.shape[0], 1, N_MOD, D_MODEL)


def modulate(x, g, shift, scale):
    return rms_norm(x, g) * (1.0 + scale) + shift


def swiglu(h, w_in, w_out):
    g, u = jnp.split(h @ w_in, 2, axis=-1)
    return (jax.nn.silu(g) * u) @ w_out


def macaron_layer(x, mod, norm_g, w_in, w_out, mixer):
    m = [mod[:, :, i] for i in range(N_MOD)]
    x = x + m[2] * (0.5 * swiglu(modulate(x, norm_g[0], m[0], m[1]), w_in[0], w_out[0]))
    out, aux = mixer(modulate(x, norm_g[1], m[3], m[4]))
    x = x + m[5] * out
    x = x + m[8] * (0.5 * swiglu(modulate(x, norm_g[2], m[6], m[7]), w_in[1], w_out[1]))
    return x, aux


def setup_inputs(seed: int = 0) -> dict:
    key = jax.random.key(seed)
    ks = jax.random.split(key, 27)
    f32 = jnp.float32
    D = D_MODEL

    def nrm(k, shape, s):
        return jax.random.normal(k, shape, f32) * s

    def gain(k, shape):
        return 1.0 + 0.02 * jax.random.normal(k, shape, f32)

    return {
        "x_prompt": nrm(ks[0], (BATCH, SEQ, D), 1.0),
        "x_sample": nrm(ks[1], (DEC_BATCH, DEC_SEQ, D), 1.0),
        "c": nrm(ks[2], (DEC_BATCH, D), 1.0),
        "cache_diff_k": nrm(ks[3], (DEC_BATCH, N_DIFF, PAST_LEN, DIFF_HEADS, 2, DIFF_HD), 1.0),
        "cache_diff_v": nrm(ks[4], (DEC_BATCH, N_DIFF, PAST_LEN, DIFF_HEADS, DIFF_VD), 1.0),
        "cache_mla_ckv": nrm(ks[5], (DEC_BATCH, N_MLA, PAST_LEN, KV_RANK), 1.0),
        "cache_mla_kpe": nrm(ks[6], (DEC_BATCH, N_MLA, PAST_LEN, MLA_ROPE), 1.0),
        "c_ctx": nrm(ks[7], (D,), 1.0),
        "w_mod": nrm(ks[8], (DEPTH, D, N_MOD * D), 0.5 * D ** -0.5),
        "b_mod": nrm(ks[9], (DEPTH, N_MOD * D), 0.02),
        "norm_g": gain(ks[10], (DEPTH, 3, D)),
        "ffn_w_in": nrm(ks[11], (DEPTH, 2, D, 2 * D_FF), D ** -0.5),
        "ffn_w_out": nrm(ks[12], (DEPTH, 2, D_FF, D), D_FF ** -0.5),
        "diff_w_qkv": nrm(ks[13], (N_DIFF, D, 3 * DIFF_HEADS * DIFF_VD), D ** -0.5),
        "diff_q_norm": gain(ks[14], (N_DIFF, DIFF_HD)),
        "diff_k_norm": gain(ks[15], (N_DIFF, DIFF_HD)),
        "diff_lambda": nrm(ks[16], (N_DIFF, 4, DIFF_HD), 0.1),
        "diff_subln": gain(ks[17], (N_DIFF, DIFF_VD)),
        "diff_w_o": nrm(ks[18], (N_DIFF, DIFF_HEADS * DIFF_VD, D), (DIFF_HEADS * DIFF_VD) ** -0.5),
        "mla_w_down": nrm(ks[19], (N_MLA, D, Q_RANK + KV_RANK + MLA_ROPE), D ** -0.5),
        "mla_q_a_norm": gain(ks[20], (N_MLA, Q_RANK)),
        "mla_kv_a_norm": gain(ks[21], (N_MLA, KV_RANK)),
        "mla_w_q_up": nrm(ks[22], (N_MLA, Q_RANK, MLA_HEADS * MLA_QK), Q_RANK ** -0.5),
        "mla_w_kv_up": nrm(ks[23], (N_MLA, KV_RANK, MLA_HEADS * (MLA_NOPE + MLA_VD)), KV_RANK ** -0.5),
        "mla_q_norm": gain(ks[24], (N_MLA, MLA_QK)),
        "mla_k_norm": gain(ks[25], (N_MLA, MLA_QK)),
        "mla_w_o": nrm(ks[26], (N_MLA, MLA_HEADS * MLA_VD, D), (MLA_HEADS * MLA_VD) ** -0.5),
    }


def reference(x_prompt, x_sample, c, cache_diff_k, cache_diff_v, cache_mla_ckv, cache_mla_kpe, c_ctx,
              w_mod, b_mod, norm_g, ffn_w_in, ffn_w_out,
              diff_w_qkv, diff_q_norm, diff_k_norm, diff_lambda, diff_subln, diff_w_o,
              mla_w_down, mla_q_a_norm, mla_kv_a_norm, mla_w_q_up, mla_w_kv_up, mla_q_norm, mla_k_norm, mla_w_o):
    row, col = grid_positions(x_sample.shape[1])
    xp, xs = x_prompt, x_sample
    new_diff_k, new_diff_v, new_mla_ckv, new_mla_kpe = [], [], [], []
    for l in range(DEPTH):
        j = l // N_MIXERS
        mod_ctx = modulation(c_ctx[None, :], w_mod[l], b_mod[l])
        mod_lat = modulation(c, w_mod[l], b_mod[l])
        if l % N_MIXERS == 0:
            w_qkv, qg, kg = diff_w_qkv[j], diff_q_norm[j], diff_k_norm[j]
            sub_g, w_o = diff_subln[j], diff_w_o[j]
            lam, lam_init = lambda_value(diff_lambda[j], l)

            def ctx_mixer(h):
                q, k, v = diff_qkv(h, w_qkv, qg, kg)
                o = diff_attention(q, k, v, lam, DIFF_SCALE)
                return diff_out(o, lam_init, sub_g, w_o), (k, v)

            def lat_mixer(h):
                q, k, v = diff_qkv(h, w_qkv, qg, kg)
                q, k = rope_2d(q, row, col), rope_2d(k, row, col)
                k_all = jnp.concatenate([cache_diff_k[:, j], k], axis=1)
                v_all = jnp.concatenate([cache_diff_v[:, j], v], axis=1)
                o = diff_attention(q, k_all, v_all, lam, DIFF_SCALE)
                return diff_out(o, lam_init, sub_g, w_o), ()

            xp, (k_ctx, v_ctx) = macaron_layer(xp, mod_ctx, norm_g[l], ffn_w_in[l], ffn_w_out[l], ctx_mixer)
            new_diff_k.append(k_ctx)
            new_diff_v.append(v_ctx)
        else:
            w_down, qag, kvag = mla_w_down[j], mla_q_a_norm[j], mla_kv_a_norm[j]
            w_q_up, w_kv_up, qg, kg, w_o = mla_w_q_up[j], mla_w_kv_up[j], mla_q_norm[j], mla_k_norm[j], mla_w_o[j]

            def ctx_mixer(h):
                cq, ckv, kpe = mla_down(h, w_down, qag, kvag)
                q = mla_queries(cq, w_q_up, qg)
                k, v = mla_keys_values(ckv, kpe, w_kv_up, kg)
                return mla_out(mha(q, k, v, MLA_SCALE), w_o), (ckv, kpe)

            def lat_mixer(h):
                cq, ckv, kpe = mla_down(h, w_down, qag, kvag)
                q = rope_tail(mla_queries(cq, w_q_up, qg), row, col)
                k_lat, v_lat = mla_keys_values(ckv, kpe, w_kv_up, kg)
                k_lat = rope_tail(k_lat, row, col)
                k_ctx, v_ctx = mla_keys_values(cache_mla_ckv[:, j], cache_mla_kpe[:, j], w_kv_up, kg)
                k_all = jnp.concatenate([k_ctx, k_lat], axis=1)
                v_all = jnp.concatenate([v_ctx, v_lat], axis=1)
                return mla_out(mha(q, k_all, v_all, MLA_SCALE), w_o), ()

            xp, (ckv_ctx, kpe_ctx) = macaron_layer(xp, mod_ctx, norm_g[l], ffn_w_in[l], ffn_w_out[l], ctx_mixer)
            new_mla_ckv.append(ckv_ctx)
            new_mla_kpe.append(kpe_ctx)
        xs, _ = macaron_layer(xs, mod_lat, norm_g[l], ffn_w_in[l], ffn_w_out[l], lat_mixer)
    return (xp, xs, jnp.stack(new_diff_k, axis=1), jnp.stack(new_diff_v, axis=1),
            jnp.stack(new_mla_ckv, axis=1), jnp.stack(new_mla_kpe, axis=1))
```

```python
import functools
import math

import jax
import jax.numpy as jnp
from jax import lax
from jax.experimental import pallas as pl
from jax.experimental.pallas import tpu as pltpu

D_MODEL = 1024
BATCH = 32
SEQ = 256
DEPTH = 4
DEC_BATCH = 2
DEC_SEQ = 1024
PAST_LEN = 256
GRID_W = 64
N_MOD = 9
D_FF = 2816
DIFF_HEADS = 8
DIFF_HD = 64
DIFF_VD = 128
MLA_HEADS = 16
MLA_NOPE = 64
MLA_ROPE = 32
MLA_QK = MLA_NOPE + MLA_ROPE
MLA_VD = 64
Q_RANK = 768
KV_RANK = 256
ROPE_THETA = 10000.0
EPS = 1e-6
DIFF_SCALE = DIFF_HD ** -0.5
MLA_SCALE = MLA_QK ** -0.5

N_CTX = BATCH * SEQ
N_LAT = DEC_BATCH * DEC_SEQ
N_TOK = N_CTX + N_LAT
LAT_KV = PAST_LEN + DEC_SEQ

LANES = 128
COND_ROWS = 8
TM = 512
FF_CHUNK = 256
TQ = 256
KVP_TM = 256
MOD_TN = 2304
VMEM_LIMIT = 56 * 1024 * 1024

F32 = jnp.float32
BF16 = jnp.bfloat16


def _params(*sem):
    return pltpu.CompilerParams(dimension_semantics=sem, vmem_limit_bytes=VMEM_LIMIT)


def _dot(a, b):
    return jnp.dot(a, b, preferred_element_type=F32)


def _dot_nt(a, b):
    return lax.dot_general(a, b, (((1,), (1,)), ((), ())), preferred_element_type=F32)


def _modulate(x, g, shift, scale):
    ms = jnp.mean(x * x, axis=-1, keepdims=True)
    return (x * lax.rsqrt(ms + EPS) * g) * (1.0 + scale) + shift


def _rope(x, cos, s_up, s_dn, shift):
    return (x * cos + pltpu.roll(x, LANES - shift, 1) * s_up
            + pltpu.roll(x, shift, 1) * s_dn)


def _cond_of_tile(i, first_lat_tile, tiles_per_lat_batch):
    lat = jnp.maximum(i - first_lat_tile, 0) // tiles_per_lat_batch
    return jnp.where(i < first_lat_tile, 0, 1 + lat)


def _mod_kernel(c_ref, w_ref, b_ref, o_ref):
    c = c_ref[...]
    s = (c * jax.nn.sigmoid(c)).astype(BF16)
    o_ref[...] = _dot(s, w_ref[...].astype(BF16)) + b_ref[...]


def _modulation(conds, w_mod, b_mod):
    n_out = N_MOD * D_MODEL
    out = pl.pallas_call(
        _mod_kernel,
        grid=(DEPTH, n_out // MOD_TN),
        in_specs=[
            pl.BlockSpec((COND_ROWS, D_MODEL), lambda l, n: (0, 0)),
            pl.BlockSpec((None, D_MODEL, MOD_TN), lambda l, n: (l, 0, n)),
            pl.BlockSpec((None, 1, MOD_TN), lambda l, n: (l, 0, n)),
        ],
        out_specs=pl.BlockSpec((None, COND_ROWS, MOD_TN), lambda l, n: (l, 0, n)),
        out_shape=jax.ShapeDtypeStruct((DEPTH, COND_ROWS, n_out), F32),
        compiler_params=_params("parallel", "parallel"),
        name="modulation",
    )(conds, w_mod, b_mod.reshape(DEPTH, 1, n_out))
    return out.reshape(DEPTH, COND_ROWS, N_MOD, D_MODEL)


def _ffn_kernel(*refs, proj, i_shift, i_scale, i_gate, i_pgate):
    if proj:
        x_ref, o_ref, wo_ref, mod_ref, g_ref, win_ref, wout_ref, out_ref, a_ref = refs
    else:
        x_ref, mod_ref, g_ref, win_ref, wout_ref, out_ref, a_ref = refs
    x = x_ref[...]
    if proj:
        x = x + mod_ref[i_pgate:i_pgate + 1, :] * _dot(o_ref[...], wo_ref[...])
    h = _modulate(x, g_ref[...], mod_ref[i_shift:i_shift + 1, :],
                  mod_ref[i_scale:i_scale + 1, :]).astype(BF16)
    for c in range(D_FF // FF_CHUNK):
        lo = c * FF_CHUNK
        g = _dot(h, win_ref[:, lo:lo + FF_CHUNK])
        u = _dot(h, win_ref[:, D_FF + lo:D_FF + lo + FF_CHUNK])
        a_ref[:, lo:lo + FF_CHUNK] = ((g * jax.nn.sigmoid(g)) * u).astype(BF16)
    ff = _dot(a_ref[...], wout_ref[...])
    out_ref[...] = x + mod_ref[i_gate:i_gate + 1, :] * (0.5 * ff)


def _ffn(x, mod_l, g, w_in, w_out, which, o=None, w_o=None):
    proj = o is not None
    base = 0 if which == 0 else 6
    first_lat = N_CTX // TM
    per_lat = DEC_SEQ // TM
    row = lambda i: (i, 0)
    full = lambda i: (0, 0)
    in_specs = [pl.BlockSpec((TM, D_MODEL), row)]
    args = [x]
    if proj:
        in_specs += [pl.BlockSpec((TM, D_MODEL), row), pl.BlockSpec((D_MODEL, D_MODEL), full)]
        args += [o, w_o]
    in_specs += [
        pl.BlockSpec((None, N_MOD, D_MODEL), lambda i: (_cond_of_tile(i, first_lat, per_lat), 0, 0)),
        pl.BlockSpec((1, D_MODEL), full),
        pl.BlockSpec((D_MODEL, 2 * D_FF), full),
        pl.BlockSpec((D_FF, D_MODEL), full),
    ]
    args += [mod_l, g.reshape(1, D_MODEL), w_in, w_out]
    kern = functools.partial(_ffn_kernel, proj=proj, i_shift=base, i_scale=base + 1,
                             i_gate=base + 2, i_pgate=5)
    return pl.pallas_call(
        kern,
        grid=(N_TOK // TM,),
        in_specs=in_specs,
        out_specs=pl.BlockSpec((TM, D_MODEL), row),
        out_shape=jax.ShapeDtypeStruct((N_TOK, D_MODEL), F32),
        scratch_shapes=[pltpu.VMEM((TM, D_FF), BF16)],
        compiler_params=_params("parallel"),
        name="ffn_proj" if proj else "ffn",
    )(*args)


def _diff_pre_kernel(*refs, rope):
    if rope:
        (x_ref, mod_ref, g_ref, w_ref, qg_ref, kg_ref, cos_ref, up_ref, dn_ref,
         q_ref, k_ref, v_ref) = refs
    else:
        x_ref, mod_ref, g_ref, w_ref, qg_ref, kg_ref, q_ref, k_ref, v_ref = refs
    h = _modulate(x_ref[...], g_ref[...], mod_ref[3:4, :], mod_ref[4:5, :]).astype(BF16)
    hw = DIFF_HEADS * DIFF_VD
    v_ref[...] = _dot(h, w_ref[:, 2 * hw:3 * hw])
    lane = lax.broadcasted_iota(jnp.int32, (1, LANES), 1)
    lo = lane < DIFF_HD
    qg = qg_ref[...]
    kg = kg_ref[...]
    for part, (gain, out_ref, scale) in enumerate(((qg, q_ref, DIFF_SCALE), (kg, k_ref, None))):
        y = _dot(h, w_ref[:, part * hw:(part + 1) * hw])
        for hd in range(DIFF_HEADS):
            sl = slice(hd * LANES, (hd + 1) * LANES)
            yh = y[:, sl]
            sq = yh * yh
            s_lo = jnp.sum(jnp.where(lo, sq, 0.0), axis=-1, keepdims=True)
            s_hi = jnp.sum(jnp.where(lo, 0.0, sq), axis=-1, keepdims=True)
            ms = jnp.where(lo, s_lo, s_hi) * (1.0 / DIFF_HD)
            yn = yh * lax.rsqrt(ms + EPS) * gain
            if rope:
                yn = _rope(yn, cos_ref[...], up_ref[...], dn_ref[...], DIFF_HD // 4)
            if scale is not None:
                yn = yn * scale
            out_ref[:, sl] = yn.astype(out_ref.dtype)


def _diff_pre(x, mod_l, g, w_qkv, qg, kg, tables, lat):
    n_rows = N_LAT if lat else N_CTX
    tile0 = N_CTX // TM if lat else 0
    per_lat = DEC_SEQ // TM
    full = lambda i: (0, 0)
    xrow = lambda i: (i + tile0, 0)
    row = lambda i: (i, 0)
    if lat:
        cond = lambda i: (1 + i // per_lat, 0, 0)
    else:
        cond = lambda i: (0, 0, 0)
    in_specs = [
        pl.BlockSpec((TM, D_MODEL), xrow),
        pl.BlockSpec((None, N_MOD, D_MODEL), cond),
        pl.BlockSpec((1, D_MODEL), full),
        pl.BlockSpec((D_MODEL, 3 * D_MODEL), full),
        pl.BlockSpec((1, LANES), full),
        pl.BlockSpec((1, LANES), full),
    ]
    args = [x, mod_l, g.reshape(1, D_MODEL), w_qkv, qg, kg]
    if lat:
        pos = lambda i: (i % per_lat, 0)
        in_specs += [pl.BlockSpec((TM, LANES), pos)] * 3
        args += list(tables)
    return pl.pallas_call(
        functools.partial(_diff_pre_kernel, rope=lat),
        grid=(n_rows // TM,),
        in_specs=in_specs,
        out_specs=[pl.BlockSpec((TM, D_MODEL), row)] * 3,
        out_shape=[jax.ShapeDtypeStruct((n_rows, D_MODEL), BF16),
                   jax.ShapeDtypeStruct((n_rows, D_MODEL), F32),
                   jax.ShapeDtypeStruct((n_rows, D_MODEL), F32)],
        compiler_params=_params("parallel"),
        name="diff_pre_lat" if lat else "diff_pre_ctx",
    )(*args)


def _diff_attn_kernel(q_ref, k_ref, v_ref, lam_ref, sub_ref, o_ref, *, heads, lam_init):
    lp = lam_ref[...]
    lam = (jnp.exp(jnp.sum(lp[0:1, :] * lp[1:2, :], axis=-1, keepdims=True))
           - jnp.exp(jnp.sum(lp[2:3, :] * lp[3:4, :], axis=-1, keepdims=True)) + lam_init)
    tq = q_ref.shape[0]
    lane = lax.broadcasted_iota(jnp.int32, (1, LANES), 1)
    lo = lane < DIFF_HD
    sub_g = sub_ref[...]
    for hd in range(heads):
        sl = slice(hd * LANES, (hd + 1) * LANES)
        qh = q_ref[:, sl]
        kh = k_ref[:, sl].astype(BF16)
        vh = v_ref[:, sl].astype(BF16)
        zero = jnp.zeros_like(qh)
        qq = jnp.concatenate([jnp.where(lo, qh, zero), jnp.where(lo, zero, qh)], axis=0)
        s = _dot_nt(qq, kh)
        e = jnp.exp(s - jnp.max(s, axis=-1, keepdims=True))
        inv = 1.0 / jnp.sum(e, axis=-1, keepdims=True)
        a = e[:tq] * inv[:tq] - lam * (e[tq:] * inv[tq:])
        o = _dot(a.astype(BF16), vh)
        ms = jnp.mean(o * o, axis=-1, keepdims=True)
        o = (o * lax.rsqrt(ms + EPS) * sub_g) * (1.0 - lam_init)
        o_ref[:, sl] = o.astype(o_ref.dtype)


def _diff_attn(q, k, v, lam_p, sub_g, lam_init, heads_per_step):
    b, lq, _ = q.shape
    lk = k.shape[1]
    w = heads_per_step * LANES
    n_hg = DIFF_HEADS // heads_per_step
    tq = min(TQ, lq)
    qmap = lambda bi, hi, qi: (bi, qi, hi)
    kmap = lambda bi, hi, qi: (bi, 0, hi)
    full = lambda bi, hi, qi: (0, 0)
    return pl.pallas_call(
        functools.partial(_diff_attn_kernel, heads=heads_per_step, lam_init=lam_init),
        grid=(b, n_hg, lq // tq),
        in_specs=[
            pl.BlockSpec((None, tq, w), qmap),
            pl.BlockSpec((None, lk, w), kmap),
            pl.BlockSpec((None, lk, w), kmap),
            pl.BlockSpec((4, DIFF_HD), full),
            pl.BlockSpec((1, LANES), full),
        ],
        out_specs=pl.BlockSpec((None, tq, w), qmap),
        out_shape=jax.ShapeDtypeStruct((b, lq, D_MODEL), BF16),
        compiler_params=_params("parallel", "parallel", "parallel"),
        name="diff_attn_lat" if lq == DEC_SEQ else "diff_attn_ctx",
    )(q, k, v, lam_p, sub_g)


def _mla_pre_kernel(*refs, rope):
    if rope:
        (x_ref, mod_ref, g_ref, wdq_ref, wdkv_ref, qag_ref, kvag_ref, wq_ref, qg_ref,
         cos_ref, up_ref, dn_ref, q_ref, ckv_ref, kpe_ref) = refs
    else:
        (x_ref, mod_ref, g_ref, wdq_ref, wdkv_ref, qag_ref, kvag_ref, wq_ref, qg_ref,
         q_ref, ckv_ref, kpe_ref) = refs
    h = _modulate(x_ref[...], g_ref[...], mod_ref[3:4, :], mod_ref[4:5, :]).astype(BF16)
    d2 = _dot(h, wdkv_ref[...])
    ckv = d2[:, :KV_RANK]
    ms = jnp.mean(ckv * ckv, axis=-1, keepdims=True)
    ckv_ref[...] = ckv * lax.rsqrt(ms + EPS) * kvag_ref[...]
    kpe_ref[...] = d2[:, KV_RANK:]
    cq = _dot(h, wdq_ref[...])
    ms = jnp.mean(cq * cq, axis=-1, keepdims=True)
    cqn = (cq * lax.rsqrt(ms + EPS) * qag_ref[...]).astype(BF16)
    q = _dot(cqn, wq_ref[...])
    qg = qg_ref[...]
    for hd in range(MLA_HEADS):
        sl = slice(hd * LANES, (hd + 1) * LANES)
        qh = q[:, sl]
        ms = jnp.sum(qh * qh, axis=-1, keepdims=True) * (1.0 / MLA_QK)
        qn = qh * lax.rsqrt(ms + EPS) * qg
        if rope:
            qn = _rope(qn, cos_ref[...], up_ref[...], dn_ref[...], MLA_ROPE // 4)
        q_ref[:, sl] = (qn * MLA_SCALE).astype(q_ref.dtype)


def _mla_pre(x, mod_l, g, w_dq, w_dkv, qag, kvag, w_q, qg, tables, lat):
    n_rows = N_LAT if lat else N_CTX
    tile0 = N_CTX // TM if lat else 0
    per_lat = DEC_SEQ // TM
    full = lambda i: (0, 0)
    xrow = lambda i: (i + tile0, 0)
    row = lambda i: (i, 0)
    if lat:
        cond = lambda i: (1 + i // per_lat, 0, 0)
    else:
        cond = lambda i: (0, 0, 0)
    kvw = KV_RANK + LANES
    qw = MLA_HEADS * LANES
    in_specs = [
        pl.BlockSpec((TM, D_MODEL), xrow),
        pl.BlockSpec((None, N_MOD, D_MODEL), cond),
        pl.BlockSpec((1, D_MODEL), full),
        pl.BlockSpec((D_MODEL, Q_RANK), full),
        pl.BlockSpec((D_MODEL, kvw), full),
        pl.BlockSpec((1, Q_RANK), full),
        pl.BlockSpec((1, KV_RANK), full),
        pl.BlockSpec((Q_RANK, qw), full),
        pl.BlockSpec((1, LANES), full),
    ]
    args = [x, mod_l, g.reshape(1, D_MODEL), w_dq, w_dkv, qag.reshape(1, Q_RANK),
            kvag.reshape(1, KV_RANK), w_q, qg]
    if lat:
        pos = lambda i: (i % per_lat, 0)
        in_specs += [pl.BlockSpec((TM, LANES), pos)] * 3
        args += list(tables)
    return pl.pallas_call(
        functools.partial(_mla_pre_kernel, rope=lat),
        grid=(n_rows // TM,),
        in_specs=in_specs,
        out_specs=[pl.BlockSpec((TM, qw), row), pl.BlockSpec((TM, KV_RANK), row),
                   pl.BlockSpec((TM, LANES), row)],
        out_shape=[jax.ShapeDtypeStruct((n_rows, qw), BF16),
                   jax.ShapeDtypeStruct((n_rows, KV_RANK), F32),
                   jax.ShapeDtypeStruct((n_rows, LANES), F32)],
        compiler_params=_params("parallel"),
        name="mla_pre_lat" if lat else "mla_pre_ctx",
    )(*args)


def _mla_kv_kernel(*refs, rope):
    if rope:
        ckv_ref, kpe_ref, wk_ref, wv_ref, kg_ref, cos_ref, up_ref, dn_ref, k_ref, v_ref = refs
    else:
        ckv_ref, kpe_ref, wk_ref, wv_ref, kg_ref, k_ref, v_ref = refs
    c = ckv_ref[...].astype(BF16)
    v_ref[...] = _dot(c, wv_ref[...]).astype(v_ref.dtype)
    kk = _dot(c, wk_ref[...])
    kpe = pltpu.roll(kpe_ref[...], MLA_NOPE, 1)
    kg = kg_ref[...]
    for hd in range(MLA_HEADS):
        sl = slice(hd * LANES, (hd + 1) * LANES)
        kh = kk[:, sl] + kpe
        ms = jnp.sum(kh * kh, axis=-1, keepdims=True) * (1.0 / MLA_QK)
        kn = kh * lax.rsqrt(ms + EPS) * kg
        if rope:
            kn = _rope(kn, cos_ref[...], up_ref[...], dn_ref[...], MLA_ROPE // 4)
        k_ref[:, sl] = kn.astype(k_ref.dtype)


def _mla_kv(ckv_rows, kpe_rows, w_k, w_v, kg, tables, lat):
    n_rows = ckv_rows.shape[0]
    full = lambda i: (0, 0)
    row = lambda i: (i, 0)
    kw = MLA_HEADS * LANES
    vw = MLA_HEADS * MLA_VD
    in_specs = [
        pl.BlockSpec((KVP_TM, KV_RANK), row),
        pl.BlockSpec((KVP_TM, LANES), row),
        pl.BlockSpec((KV_RANK, kw), full),
        pl.BlockSpec((KV_RANK, vw), full),
        pl.BlockSpec((1, LANES), full),
    ]
    args = [ckv_rows, kpe_rows, w_k, w_v, kg]
    if lat:
        per = LAT_KV // KVP_TM
        pos = lambda i: (i % per, 0)
        in_specs += [pl.BlockSpec((KVP_TM, LANES), pos)] * 3
        args += list(tables)
    return pl.pallas_call(
        functools.partial(_mla_kv_kernel, rope=lat),
        grid=(n_rows // KVP_TM,),
        in_specs=in_specs,
        out_specs=[pl.BlockSpec((KVP_TM, kw), row), pl.BlockSpec((KVP_TM, vw), row)],
        out_shape=[jax.ShapeDtypeStruct((n_rows, kw), BF16),
                   jax.ShapeDtypeStruct((n_rows, vw), BF16)],
        compiler_params=_params("parallel"),
        name="mla_kv_lat" if lat else "mla_kv_ctx",
    )(*args)


def _mla_attn_kernel(q_ref, k_ref, v_ref, o_ref, *, pairs):
    lane = lax.broadcasted_iota(jnp.int32, (1, LANES), 1)
    lo = lane < MLA_VD
    for p in range(pairs):
        vp = v_ref[:, p * LANES:(p + 1) * LANES]
        outs = []
        for sub in range(2):
            hd = 2 * p + sub
            sl = slice(hd * LANES, (hd + 1) * LANES)
            s = _dot_nt(q_ref[:, sl], k_ref[:, sl])
            e = jnp.exp(s - jnp.max(s, axis=-1, keepdims=True))
            a = e * (1.0 / jnp.sum(e, axis=-1, keepdims=True))
            outs.append(_dot(a.astype(BF16), vp))
        o_ref[:, p * LANES:(p + 1) * LANES] = jnp.where(lo, outs[0], outs[1]).astype(o_ref.dtype)


def _mla_attn(q, k, v, pairs_per_step):
    b, lq, _ = q.shape
    lk = k.shape[1]
    n_pg = MLA_HEADS // 2 // pairs_per_step
    tq = min(TQ, lq)
    qkw = pairs_per_step * 2 * LANES
    ow = pairs_per_step * LANES
    qmap = lambda bi, hi, qi: (bi, qi, hi)
    kmap = lambda bi, hi, qi: (bi, 0, hi)
    return pl.pallas_call(
        functools.partial(_mla_attn_kernel, pairs=pairs_per_step),
        grid=(b, n_pg, lq // tq),
        in_specs=[
            pl.BlockSpec((None, tq, qkw), qmap),
            pl.BlockSpec((None, lk, qkw), kmap),
            pl.BlockSpec((None, lk, ow), kmap),
        ],
        out_specs=pl.BlockSpec((None, tq, ow), qmap),
        out_shape=jax.ShapeDtypeStruct((b, lq, D_MODEL), BF16),
        compiler_params=_params("parallel", "parallel", "parallel"),
        name="mla_attn_lat" if lq == DEC_SEQ else "mla_attn_ctx",
    )(q, k, v)


def _rope_tables(n_rot, lane0, identity_rows):
    rows = DEC_SEQ // GRID_W
    row = jnp.repeat(jnp.arange(rows, dtype=jnp.int32), GRID_W)
    col = jnp.tile(jnp.arange(GRID_W, dtype=jnp.int32), rows)
    n = n_rot // 2
    freqs = ROPE_THETA ** (-jnp.arange(0, n, 2, dtype=F32) / n)
    zeros = jnp.zeros((DEC_SEQ, n // 2), F32)
    cos_parts, up_parts, dn_parts = [], [], []
    for pos in (row, col):
        ang = pos.astype(F32)[:, None] * freqs[None, :]
        c, s = jnp.cos(ang), jnp.sin(ang)
        cos_parts += [c, c]
        up_parts += [-s, zeros]
        dn_parts += [zeros, s]
    cos = jnp.concatenate(cos_parts, axis=1)
    up = jnp.concatenate(up_parts, axis=1)
    dn = jnp.concatenate(dn_parts, axis=1)

    def place(t, fill):
        reps = 2 if lane0 == 0 else 1
        body = jnp.concatenate([t] * reps, axis=1)
        left = jnp.full((DEC_SEQ, lane0), fill, F32)
        right = jnp.full((DEC_SEQ, LANES - lane0 - body.shape[1]), fill, F32)
        full = jnp.concatenate([left, body, right], axis=1)
        ident = jnp.full((identity_rows, LANES), fill, F32)
        return jnp.concatenate([ident, full], axis=0)

    return place(cos, 1.0), place(up, 0.0), place(dn, 0.0)


def kernel(x_prompt, x_sample, c, cache_diff_k, cache_diff_v, cache_mla_ckv, cache_mla_kpe, c_ctx, w_mod, b_mod, norm_g, ffn_w_in, ffn_w_out, diff_w_qkv, diff_q_norm, diff_k_norm, diff_lambda, diff_subln, diff_w_o, mla_w_down, mla_q_a_norm, mla_kv_a_norm, mla_w_q_up, mla_w_kv_up, mla_q_norm, mla_k_norm, mla_w_o):
    x = jnp.concatenate([x_prompt.reshape(N_CTX, D_MODEL), x_sample.reshape(N_LAT, D_MODEL)], axis=0)
    conds = jnp.concatenate(
        [c_ctx[None, :], c, jnp.zeros((COND_ROWS - 1 - DEC_BATCH, D_MODEL), F32)], axis=0)
    mod = _modulation(conds, w_mod, b_mod)

    w_in = ffn_w_in.astype(BF16)
    w_out = ffn_w_out.astype(BF16)
    diff_tabs = _rope_tables(DIFF_HD, 0, 0)
    mla_tabs = _rope_tables(MLA_ROPE, MLA_NOPE, 0)
    mla_kv_tabs = _rope_tables(MLA_ROPE, MLA_NOPE, PAST_LEN)

    new_diff_k, new_diff_v, new_mla_ckv, new_mla_kpe = [], [], [], []
    o_prev, wo_prev = None, None
    for l in range(DEPTH):
        j = l // 2
        mod_l = mod[l]
        x = _ffn(x, mod_l, norm_g[l, 0], w_in[l, 0], w_out[l, 0], 0)
        if l % 2 == 0:
            w_qkv = diff_w_qkv[j].astype(BF16)
            qg = jnp.tile(diff_q_norm[j], 2).reshape(1, LANES)
            kg = jnp.tile(diff_k_norm[j], 2).reshape(1, LANES)
            sub_g = diff_subln[j].reshape(1, LANES)
            lam_init = 0.8 - 0.6 * math.exp(-0.3 * l)
            g1 = norm_g[l, 1]
            q_c, k_c, v_c = _diff_pre(x, mod_l, g1, w_qkv, qg, kg, None, lat=False)
            q_l, k_l, v_l = _diff_pre(x, mod_l, g1, w_qkv, qg, kg, diff_tabs, lat=True)
            new_diff_k.append(k_c.reshape(BATCH, SEQ, DIFF_HEADS, 2, DIFF_HD))
            new_diff_v.append(v_c.reshape(BATCH, SEQ, DIFF_HEADS, DIFF_VD))
            o_c = _diff_attn(q_c.reshape(BATCH, SEQ, D_MODEL), k_c.reshape(BATCH, SEQ, D_MODEL),
                             v_c.reshape(BATCH, SEQ, D_MODEL), diff_lambda[j], sub_g, lam_init,
                             heads_per_step=DIFF_HEADS)
            k_all = jnp.concatenate([cache_diff_k[:, j].reshape(DEC_BATCH, PAST_LEN, D_MODEL),
                                     k_l.reshape(DEC_BATCH, DEC_SEQ, D_MODEL)], axis=1)
            v_all = jnp.concatenate([cache_diff_v[:, j].reshape(DEC_BATCH, PAST_LEN, D_MODEL),
                                     v_l.reshape(DEC_BATCH, DEC_SEQ, D_MODEL)], axis=1)
            o_l = _diff_attn(q_l.reshape(DEC_BATCH, DEC_SEQ, D_MODEL), k_all, v_all,
                             diff_lambda[j], sub_g, lam_init, heads_per_step=1)
            w_o = diff_w_o[j].astype(BF16)
        else:
            wd = mla_w_down[j]
            w_dq = wd[:, :Q_RANK].astype(BF16)
            w_dkv = jnp.pad(wd[:, Q_RANK:], ((0, 0), (0, LANES - MLA_ROPE))).astype(BF16)
            w_q = jnp.pad(mla_w_q_up[j].reshape(Q_RANK, MLA_HEADS, MLA_QK),
                          ((0, 0), (0, 0), (0, LANES - MLA_QK))).reshape(Q_RANK, MLA_HEADS * LANES).astype(BF16)
            wkv = mla_w_kv_up[j].reshape(KV_RANK, MLA_HEADS, MLA_NOPE + MLA_VD)
            w_k = jnp.pad(wkv[:, :, :MLA_NOPE], ((0, 0), (0, 0), (0, LANES - MLA_NOPE))
                          ).reshape(KV_RANK, MLA_HEADS * LANES).astype(BF16)
            w_v = wkv[:, :, MLA_NOPE:].reshape(KV_RANK, MLA_HEADS * MLA_VD).astype(BF16)
            qg = jnp.pad(mla_q_norm[j], (0, LANES - MLA_QK)).reshape(1, LANES)
            kg = jnp.pad(mla_k_norm[j], (0, LANES - MLA_QK)).reshape(1, LANES)
            g1 = norm_g[l, 1]
            pre = functools.partial(_mla_pre, x, mod_l, g1, w_dq, w_dkv, mla_q_a_norm[j],
                                    mla_kv_a_norm[j], w_q, qg)
            q_c, ckv_c, kpe_c = pre(None, lat=False)
            q_l, ckv_l, kpe_l = pre(mla_tabs, lat=True)
            new_mla_ckv.append(ckv_c.reshape(BATCH, SEQ, KV_RANK))
            new_mla_kpe.append(kpe_c[:, :MLA_ROPE].reshape(BATCH, SEQ, MLA_ROPE))
            k_c, v_c = _mla_kv(ckv_c, kpe_c, w_k, w_v, kg, None, lat=False)
            cache_kpe = jnp.pad(cache_mla_kpe[:, j], ((0, 0), (0, 0), (0, LANES - MLA_ROPE)))
            ckv_rows = jnp.concatenate([cache_mla_ckv[:, j], ckv_l.reshape(DEC_BATCH, DEC_SEQ, KV_RANK)],
                                       axis=1).reshape(DEC_BATCH * LAT_KV, KV_RANK)
            kpe_rows = jnp.concatenate([cache_kpe, kpe_l.reshape(DEC_BATCH, DEC_SEQ, LANES)],
                                       axis=1).reshape(DEC_BATCH * LAT_KV, LANES)
            k_a, v_a = _mla_kv(ckv_rows, kpe_rows, w_k, w_v, kg, mla_kv_tabs, lat=True)
            kw = MLA_HEADS * LANES
            o_c = _mla_attn(q_c.reshape(BATCH, SEQ, kw), k_c.reshape(BATCH, SEQ, kw),
                            v_c.reshape(BATCH, SEQ, D_MODEL), pairs_per_step=MLA_HEADS // 2)
            o_l = _mla_attn(q_l.reshape(DEC_BATCH, DEC_SEQ, kw), k_a.reshape(DEC_BATCH, LAT_KV, kw),
                            v_a.reshape(DEC_BATCH, LAT_KV, D_MODEL), pairs_per_step=1)
            w_o = mla_w_o[j].astype(BF16)
        o = jnp.concatenate([o_c.reshape(N_CTX, D_MODEL), o_l.reshape(N_LAT, D_MODEL)], axis=0)
        x = _ffn(x, mod_l, norm_g[l, 2], w_in[l, 1], w_out[l, 1], 1, o=o, w_o=w_o)

    return (x[:N_CTX].reshape(BATCH, SEQ, D_MODEL), x[N_CTX:].reshape(DEC_BATCH, DEC_SEQ, D_MODEL),
            jnp.stack(new_diff_k, axis=1), jnp.stack(new_diff_v, axis=1),
            jnp.stack(new_mla_ckv, axis=1), jnp.stack(new_mla_kpe, axis=1))
```

```python
import functools
import math

import jax
import jax.numpy as jnp
from jax import lax
from jax.experimental import pallas as pl
from jax.experimental.pallas import tpu as pltpu

D_MODEL = 1024
BATCH = 32
SEQ = 256
DEPTH = 4
DEC_BATCH = 2
DEC_SEQ = 1024
PAST_LEN = 256
GRID_W = 64
N_MOD = 9
D_FF = 2816
DIFF_HEADS = 8
DIFF_HD = 64
DIFF_VD = 128
MLA_HEADS = 16
MLA_NOPE = 64
MLA_ROPE = 32
MLA_QK = MLA_NOPE + MLA_ROPE
MLA_VD = 64
Q_RANK = 768
KV_RANK = 256
ROPE_THETA = 10000.0
EPS = 1e-6
DIFF_SCALE = DIFF_HD ** -0.5
MLA_SCALE = MLA_QK ** -0.5

N_CTX = BATCH * SEQ
N_LAT = DEC_BATCH * DEC_SEQ
N_TOK = N_CTX + N_LAT
LAT_KV = PAST_LEN + DEC_SEQ

LANES = 128
COND_ROWS = 8
TM = 512
FF_CHUNK = 256
TQ = 256
LAT_HEADS_PER_STEP = 4
KVP_TM = 256
MOD_TN = 2304
VMEM_LIMIT = 56 * 1024 * 1024

F32 = jnp.float32
BF16 = jnp.bfloat16


def _params(*sem):
    return pltpu.CompilerParams(dimension_semantics=sem, vmem_limit_bytes=VMEM_LIMIT)


def _dot(a, b):
    return jnp.dot(a, b, preferred_element_type=F32)


def _dot_nt(a, b):
    return lax.dot_general(a, b, (((1,), (1,)), ((), ())), preferred_element_type=F32)


def _modulate(x, g, shift, scale):
    ms = jnp.mean(x * x, axis=-1, keepdims=True)
    return (x * lax.rsqrt(ms + EPS) * g) * (1.0 + scale) + shift


def _rope(x, cos, s_up, s_dn, shift):
    return (x * cos + pltpu.roll(x, LANES - shift, 1) * s_up
            + pltpu.roll(x, shift, 1) * s_dn)


def _cond_of_tile(i, first_lat_tile, tiles_per_lat_batch):
    lat = jnp.maximum(i - first_lat_tile, 0) // tiles_per_lat_batch
    return jnp.where(i < first_lat_tile, 0, 1 + lat)


def _mod_kernel(c_ref, w_ref, b_ref, o_ref):
    c = c_ref[...]
    s = (c * jax.nn.sigmoid(c)).astype(BF16)
    o_ref[...] = _dot(s, w_ref[...].astype(BF16)) + b_ref[...]


def _modulation(conds, w_mod, b_mod):
    n_out = N_MOD * D_MODEL
    out = pl.pallas_call(
        _mod_kernel,
        grid=(DEPTH, n_out // MOD_TN),
        in_specs=[
            pl.BlockSpec((COND_ROWS, D_MODEL), lambda l, n: (0, 0)),
            pl.BlockSpec((None, D_MODEL, MOD_TN), lambda l, n: (l, 0, n)),
            pl.BlockSpec((None, 1, MOD_TN), lambda l, n: (l, 0, n)),
        ],
        out_specs=pl.BlockSpec((None, COND_ROWS, MOD_TN), lambda l, n: (l, 0, n)),
        out_shape=jax.ShapeDtypeStruct((DEPTH, COND_ROWS, n_out), F32),
        compiler_params=_params("parallel", "parallel"),
        name="modulation",
    )(conds, w_mod, b_mod.reshape(DEPTH, 1, n_out))
    return out.reshape(DEPTH, COND_ROWS, N_MOD, D_MODEL)


FIRST_LAT_TILE = N_CTX // TM
TILES_PER_LAT_BATCH = DEC_SEQ // TM


def _ctx_tile(i):
    return (jnp.minimum(i, FIRST_LAT_TILE - 1), 0)


def _lat_tile(i):
    return (jnp.maximum(i - FIRST_LAT_TILE, 0), 0)


def _pick_rows(i, ctx_ref, lat_ref):
    rows = i * TM + lax.broadcasted_iota(jnp.int32, (TM, 1), 0)
    return jnp.where(rows < N_CTX, ctx_ref[...], lat_ref[...])


def _ffn_kernel(*refs, n_x, proj, n_out, i_shift, i_scale, i_gate, i_pgate):
    refs = list(refs)
    x_refs = [refs.pop(0) for _ in range(n_x)]
    if proj:
        oc_ref, ol_ref, wo_ref = refs[:3]
        refs = refs[3:]
    mod_ref, g_ref, win_ref, wout_ref = refs[:4]
    out_refs = refs[4:4 + n_out]
    a_ref = refs[4 + n_out]
    i = pl.program_id(0)
    x = x_refs[0][...] if n_x == 1 else _pick_rows(i, *x_refs)
    if proj:
        o = _pick_rows(i, oc_ref, ol_ref)
        x = x + mod_ref[i_pgate:i_pgate + 1, :] * _dot(o, wo_ref[...])
    h = _modulate(x, g_ref[...], mod_ref[i_shift:i_shift + 1, :],
                  mod_ref[i_scale:i_scale + 1, :]).astype(BF16)
    for c in range(D_FF // FF_CHUNK):
        lo = c * FF_CHUNK
        g = _dot(h, win_ref[:, lo:lo + FF_CHUNK])
        u = _dot(h, win_ref[:, D_FF + lo:D_FF + lo + FF_CHUNK])
        a_ref[:, lo:lo + FF_CHUNK] = ((g * jax.nn.sigmoid(g)) * u).astype(BF16)
    ff = _dot(a_ref[...], wout_ref[...])
    y = x + mod_ref[i_gate:i_gate + 1, :] * (0.5 * ff)
    if n_out == 1:
        out_refs[0][...] = y
    else:
        @pl.when(i < FIRST_LAT_TILE)
        def _():
            out_refs[0][...] = y

        @pl.when(i >= FIRST_LAT_TILE)
        def _():
            out_refs[1][...] = y


def _ffn(xs, mod_l, g, w_in, w_out, which, os=None, w_o=None, split_out=False):
    proj = os is not None
    base = 0 if which == 0 else 6
    row = lambda i: (i, 0)
    full = lambda i: (0, 0)
    tile = (TM, D_MODEL)
    if len(xs) == 1:
        in_specs = [pl.BlockSpec(tile, row)]
    else:
        in_specs = [pl.BlockSpec(tile, _ctx_tile), pl.BlockSpec(tile, _lat_tile)]
    args = list(xs)
    if proj:
        in_specs += [pl.BlockSpec(tile, _ctx_tile), pl.BlockSpec(tile, _lat_tile),
                     pl.BlockSpec((D_MODEL, D_MODEL), full)]
        args += [os[0], os[1], w_o]
    in_specs += [
        pl.BlockSpec((None, N_MOD, D_MODEL),
                     lambda i: (_cond_of_tile(i, FIRST_LAT_TILE, TILES_PER_LAT_BATCH), 0, 0)),
        pl.BlockSpec((1, D_MODEL), full),
        pl.BlockSpec((D_MODEL, 2 * D_FF), full),
        pl.BlockSpec((D_FF, D_MODEL), full),
    ]
    args += [mod_l, g.reshape(1, D_MODEL), w_in, w_out]
    if split_out:
        out_specs = [pl.BlockSpec(tile, _ctx_tile), pl.BlockSpec(tile, _lat_tile)]
        out_shape = [jax.ShapeDtypeStruct((N_CTX, D_MODEL), F32),
                     jax.ShapeDtypeStruct((N_LAT, D_MODEL), F32)]
    else:
        out_specs = [pl.BlockSpec(tile, row)]
        out_shape = [jax.ShapeDtypeStruct((N_TOK, D_MODEL), F32)]
    kern = functools.partial(_ffn_kernel, n_x=len(xs), proj=proj, n_out=len(out_shape),
                             i_shift=base, i_scale=base + 1, i_gate=base + 2, i_pgate=5)
    return pl.pallas_call(
        kern,
        grid=(N_TOK // TM,),
        in_specs=in_specs,
        out_specs=out_specs,
        out_shape=out_shape,
        scratch_shapes=[pltpu.VMEM((TM, D_FF), BF16)],
        compiler_params=_params("arbitrary"),
        name="ffn_proj" if proj else "ffn",
    )(*args)


def _diff_pre_kernel(*refs, rope):
    if rope:
        (x_ref, mod_ref, g_ref, w_ref, qg_ref, kg_ref, cos_ref, up_ref, dn_ref,
         q_ref, k_ref, v_ref) = refs
    else:
        x_ref, mod_ref, g_ref, w_ref, qg_ref, kg_ref, q_ref, k_ref, v_ref = refs
    h = _modulate(x_ref[...], g_ref[...], mod_ref[3:4, :], mod_ref[4:5, :]).astype(BF16)
    hw = DIFF_HEADS * DIFF_VD
    v_ref[...] = _dot(h, w_ref[:, 2 * hw:3 * hw])
    lane = lax.broadcasted_iota(jnp.int32, (1, LANES), 1)
    lo = lane < DIFF_HD
    qg = qg_ref[...]
    kg = kg_ref[...]
    for part, (gain, out_ref, scale) in enumerate(((qg, q_ref, DIFF_SCALE), (kg, k_ref, None))):
        y = _dot(h, w_ref[:, part * hw:(part + 1) * hw])
        for hd in range(DIFF_HEADS):
            sl = slice(hd * LANES, (hd + 1) * LANES)
            yh = y[:, sl]
            sq = yh * yh
            s_lo = jnp.sum(jnp.where(lo, sq, 0.0), axis=-1, keepdims=True)
            s_hi = jnp.sum(jnp.where(lo, 0.0, sq), axis=-1, keepdims=True)
            ms = jnp.where(lo, s_lo, s_hi) * (1.0 / DIFF_HD)
            yn = yh * lax.rsqrt(ms + EPS) * gain
            if rope:
                yn = _rope(yn, cos_ref[...], up_ref[...], dn_ref[...], DIFF_HD // 4)
            if scale is not None:
                yn = yn * scale
            out_ref[:, sl] = yn.astype(out_ref.dtype)


def _diff_pre(x, mod_l, g, w_qkv, qg, kg, tables, lat):
    n_rows = N_LAT if lat else N_CTX
    tile0 = N_CTX // TM if lat else 0
    per_lat = DEC_SEQ // TM
    full = lambda i: (0, 0)
    xrow = lambda i: (i + tile0, 0)
    row = lambda i: (i, 0)
    if lat:
        cond = lambda i: (1 + i // per_lat, 0, 0)
    else:
        cond = lambda i: (0, 0, 0)
    in_specs = [
        pl.BlockSpec((TM, D_MODEL), xrow),
        pl.BlockSpec((None, N_MOD, D_MODEL), cond),
        pl.BlockSpec((1, D_MODEL), full),
        pl.BlockSpec((D_MODEL, 3 * D_MODEL), full),
        pl.BlockSpec((1, LANES), full),
        pl.BlockSpec((1, LANES), full),
    ]
    args = [x, mod_l, g.reshape(1, D_MODEL), w_qkv, qg, kg]
    if lat:
        pos = lambda i: (i % per_lat, 0)
        in_specs += [pl.BlockSpec((TM, LANES), pos)] * 3
        args += list(tables)
    return pl.pallas_call(
        functools.partial(_diff_pre_kernel, rope=lat),
        grid=(n_rows // TM,),
        in_specs=in_specs,
        out_specs=[pl.BlockSpec((TM, D_MODEL), row)] * 3,
        out_shape=[jax.ShapeDtypeStruct((n_rows, D_MODEL), BF16),
                   jax.ShapeDtypeStruct((n_rows, D_MODEL), F32),
                   jax.ShapeDtypeStruct((n_rows, D_MODEL), F32)],
        compiler_params=_params("parallel"),
        name="diff_pre_lat" if lat else "diff_pre_ctx",
    )(*args)


def _softmax_pv(s, v_aug):
    e = jnp.exp(s - jnp.max(s, axis=-1, keepdims=True)).astype(BF16)
    r = _dot(e, v_aug)
    return r[:, :LANES] * (1.0 / r[:, LANES:])


def _diff_attn_kernel(*refs, heads, lam_init, cached):
    if cached:
        q_ref, kc_ref, k_ref, vc_ref, v_ref, lam_ref, sub_ref, o_ref = refs
    else:
        q_ref, k_ref, v_ref, lam_ref, sub_ref, o_ref = refs
    lp = lam_ref[...]
    lam = (jnp.exp(jnp.sum(lp[0:1, :] * lp[1:2, :], axis=-1, keepdims=True))
           - jnp.exp(jnp.sum(lp[2:3, :] * lp[3:4, :], axis=-1, keepdims=True)) + lam_init)
    tq = q_ref.shape[0]
    lane = lax.broadcasted_iota(jnp.int32, (1, LANES), 1)
    lo = lane < DIFF_HD
    sub_g = sub_ref[...]
    for hd in range(heads):
        sl = slice(hd * LANES, (hd + 1) * LANES)
        qh = q_ref[:, sl]
        kh = k_ref[:, sl].astype(BF16)
        vh = v_ref[:, sl].astype(BF16)
        if cached:
            kh = jnp.concatenate([kc_ref[:, sl].astype(BF16), kh], axis=0)
            vh = jnp.concatenate([vc_ref[:, sl].astype(BF16), vh], axis=0)
        v_aug = jnp.concatenate([vh, jnp.ones_like(vh)], axis=1)
        zero = jnp.zeros_like(qh)
        qq = jnp.concatenate([jnp.where(lo, qh, zero), jnp.where(lo, zero, qh)], axis=0)
        o12 = _softmax_pv(_dot_nt(qq, kh), v_aug)
        o = o12[:tq] - lam * o12[tq:]
        ms = jnp.mean(o * o, axis=-1, keepdims=True)
        o = (o * lax.rsqrt(ms + EPS) * sub_g) * (1.0 - lam_init)
        o_ref[:, sl] = o.astype(o_ref.dtype)


def _diff_attn(q, k, v, lam_p, sub_g, lam_init, heads_per_step, k_cache=None, v_cache=None):
    b, lq, _ = q.shape
    lk = k.shape[1]
    cached = k_cache is not None
    w = heads_per_step * LANES
    n_hg = DIFF_HEADS // heads_per_step
    tq = min(TQ, lq)
    qmap = lambda bi, hi, qi: (bi, qi, hi)
    kmap = lambda bi, hi, qi: (bi, 0, hi)
    full = lambda bi, hi, qi: (0, 0)
    q_spec = pl.BlockSpec((None, tq, w), qmap)
    kv_spec = pl.BlockSpec((None, lk, w), kmap)
    c_spec = pl.BlockSpec((None, PAST_LEN, w), kmap)
    if cached:
        in_specs = [q_spec, c_spec, kv_spec, c_spec, kv_spec]
        args = [q, k_cache, k, v_cache, v]
    else:
        in_specs = [q_spec, kv_spec, kv_spec]
        args = [q, k, v]
    in_specs += [pl.BlockSpec((4, DIFF_HD), full), pl.BlockSpec((1, LANES), full)]
    args += [lam_p, sub_g]
    return pl.pallas_call(
        functools.partial(_diff_attn_kernel, heads=heads_per_step, lam_init=lam_init, cached=cached),
        grid=(b, n_hg, lq // tq),
        in_specs=in_specs,
        out_specs=pl.BlockSpec((None, tq, w), qmap),
        out_shape=jax.ShapeDtypeStruct((b, lq, D_MODEL), BF16),
        compiler_params=_params("parallel", "parallel", "parallel"),
        name="diff_attn_lat" if cached else "diff_attn_ctx",
    )(*args)


def _mla_pre_kernel(*refs, rope):
    if rope:
        (x_ref, mod_ref, g_ref, wdq_ref, wdkv_ref, qag_ref, kvag_ref, wq_ref, qg_ref,
         cos_ref, up_ref, dn_ref, q_ref, ckv_ref, kpe_ref) = refs
    else:
        (x_ref, mod_ref, g_ref, wdq_ref, wdkv_ref, qag_ref, kvag_ref, wq_ref, qg_ref,
         q_ref, ckv_ref, kpe_ref) = refs
    h = _modulate(x_ref[...], g_ref[...], mod_ref[3:4, :], mod_ref[4:5, :]).astype(BF16)
    d2 = _dot(h, wdkv_ref[...])
    ckv = d2[:, :KV_RANK]
    ms = jnp.mean(ckv * ckv, axis=-1, keepdims=True)
    ckv_ref[...] = ckv * lax.rsqrt(ms + EPS) * kvag_ref[...]
    kpe_ref[...] = d2[:, KV_RANK:]
    cq = _dot(h, wdq_ref[...])
    ms = jnp.mean(cq * cq, axis=-1, keepdims=True)
    cqn = (cq * lax.rsqrt(ms + EPS) * qag_ref[...]).astype(BF16)
    q = _dot(cqn, wq_ref[...])
    qg = qg_ref[...]
    for hd in range(MLA_HEADS):
        sl = slice(hd * LANES, (hd + 1) * LANES)
        qh = q[:, sl]
        ms = jnp.sum(qh * qh, axis=-1, keepdims=True) * (1.0 / MLA_QK)
        qn = qh * lax.rsqrt(ms + EPS) * qg
        if rope:
            qn = _rope(qn, cos_ref[...], up_ref[...], dn_ref[...], MLA_ROPE // 4)
        q_ref[:, sl] = (qn * MLA_SCALE).astype(q_ref.dtype)


def _mla_pre(x, mod_l, g, w_dq, w_dkv, qag, kvag, w_q, qg, tables, lat):
    n_rows = N_LAT if lat else N_CTX
    tile0 = N_CTX // TM if lat else 0
    per_lat = DEC_SEQ // TM
    full = lambda i: (0, 0)
    xrow = lambda i: (i + tile0, 0)
    row = lambda i: (i, 0)
    if lat:
        cond = lambda i: (1 + i // per_lat, 0, 0)
    else:
        cond = lambda i: (0, 0, 0)
    kvw = KV_RANK + LANES
    qw = MLA_HEADS * LANES
    in_specs = [
        pl.BlockSpec((TM, D_MODEL), xrow),
        pl.BlockSpec((None, N_MOD, D_MODEL), cond),
        pl.BlockSpec((1, D_MODEL), full),
        pl.BlockSpec((D_MODEL, Q_RANK), full),
        pl.BlockSpec((D_MODEL, kvw), full),
        pl.BlockSpec((1, Q_RANK), full),
        pl.BlockSpec((1, KV_RANK), full),
        pl.BlockSpec((Q_RANK, qw), full),
        pl.BlockSpec((1, LANES), full),
    ]
    args = [x, mod_l, g.reshape(1, D_MODEL), w_dq, w_dkv, qag.reshape(1, Q_RANK),
            kvag.reshape(1, KV_RANK), w_q, qg]
    if lat:
        pos = lambda i: (i % per_lat, 0)
        in_specs += [pl.BlockSpec((TM, LANES), pos)] * 3
        args += list(tables)
    return pl.pallas_call(
        functools.partial(_mla_pre_kernel, rope=lat),
        grid=(n_rows // TM,),
        in_specs=in_specs,
        out_specs=[pl.BlockSpec((TM, qw), row), pl.BlockSpec((TM, KV_RANK), row),
                   pl.BlockSpec((TM, LANES), row)],
        out_shape=[jax.ShapeDtypeStruct((n_rows, qw), BF16),
                   jax.ShapeDtypeStruct((n_rows, KV_RANK), F32),
                   jax.ShapeDtypeStruct((n_rows, LANES), F32)],
        compiler_params=_params("parallel"),
        name="mla_pre_lat" if lat else "mla_pre_ctx",
    )(*args)


def _mla_kv_kernel(*refs, rope):
    if rope:
        ckv_ref, kpe_ref, wk_ref, wv_ref, kg_ref, cos_ref, up_ref, dn_ref, k_ref, v_ref = refs
    else:
        ckv_ref, kpe_ref, wk_ref, wv_ref, kg_ref, k_ref, v_ref = refs
    c = ckv_ref[...].astype(BF16)
    v_ref[...] = _dot(c, wv_ref[...]).astype(v_ref.dtype)
    kk = _dot(c, wk_ref[...])
    kpe = pltpu.roll(kpe_ref[...], MLA_NOPE, 1)
    kg = kg_ref[...]
    for hd in range(MLA_HEADS):
        sl = slice(hd * LANES, (hd + 1) * LANES)
        kh = kk[:, sl] + kpe
        ms = jnp.sum(kh * kh, axis=-1, keepdims=True) * (1.0 / MLA_QK)
        kn = kh * lax.rsqrt(ms + EPS) * kg
        if rope:
            kn = _rope(kn, cos_ref[...], up_ref[...], dn_ref[...], MLA_ROPE // 4)
        k_ref[:, sl] = kn.astype(k_ref.dtype)


def _mla_kv(ckv_rows, kpe_rows, w_k, w_v, kg, tables, lat):
    n_rows = ckv_rows.shape[0]
    full = lambda i: (0, 0)
    row = lambda i: (i, 0)
    kw = MLA_HEADS * LANES
    vw = MLA_HEADS * MLA_VD
    in_specs = [
        pl.BlockSpec((KVP_TM, KV_RANK), row),
        pl.BlockSpec((KVP_TM, LANES), row),
        pl.BlockSpec((KV_RANK, kw), full),
        pl.BlockSpec((KV_RANK, vw), full),
        pl.BlockSpec((1, LANES), full),
    ]
    args = [ckv_rows, kpe_rows, w_k, w_v, kg]
    if lat:
        per = LAT_KV // KVP_TM
        pos = lambda i: (i % per, 0)
        in_specs += [pl.BlockSpec((KVP_TM, LANES), pos)] * 3
        args += list(tables)
    return pl.pallas_call(
        functools.partial(_mla_kv_kernel, rope=lat),
        grid=(n_rows // KVP_TM,),
        in_specs=in_specs,
        out_specs=[pl.BlockSpec((KVP_TM, kw), row), pl.BlockSpec((KVP_TM, vw), row)],
        out_shape=[jax.ShapeDtypeStruct((n_rows, kw), BF16),
                   jax.ShapeDtypeStruct((n_rows, vw), BF16)],
        compiler_params=_params("parallel"),
        name="mla_kv_lat" if lat else "mla_kv_ctx",
    )(*args)


def _mla_attn_kernel(q_ref, k_ref, v_ref, o_ref, *, pairs):
    lane = lax.broadcasted_iota(jnp.int32, (1, LANES), 1)
    lo = lane < MLA_VD
    for p in range(pairs):
        vp = v_ref[:, p * LANES:(p + 1) * LANES]
        v_aug = jnp.concatenate([vp, jnp.ones_like(vp)], axis=1)
        outs = []
        for sub in range(2):
            hd = 2 * p + sub
            sl = slice(hd * LANES, (hd + 1) * LANES)
            outs.append(_softmax_pv(_dot_nt(q_ref[:, sl], k_ref[:, sl]), v_aug))
        o_ref[:, p * LANES:(p + 1) * LANES] = jnp.where(lo, outs[0], outs[1]).astype(o_ref.dtype)


def _mla_attn(q, k, v, pairs_per_step):
    b, lq, _ = q.shape
    lk = k.shape[1]
    n_pg = MLA_HEADS // 2 // pairs_per_step
    tq = min(TQ, lq)
    qkw = pairs_per_step * 2 * LANES
    ow = pairs_per_step * LANES
    qmap = lambda bi, hi, qi: (bi, qi, hi)
    kmap = lambda bi, hi, qi: (bi, 0, hi)
    return pl.pallas_call(
        functools.partial(_mla_attn_kernel, pairs=pairs_per_step),
        grid=(b, n_pg, lq // tq),
        in_specs=[
            pl.BlockSpec((None, tq, qkw), qmap),
            pl.BlockSpec((None, lk, qkw), kmap),
            pl.BlockSpec((None, lk, ow), kmap),
        ],
        out_specs=pl.BlockSpec((None, tq, ow), qmap),
        out_shape=jax.ShapeDtypeStruct((b, lq, D_MODEL), BF16),
        compiler_params=_params("parallel", "parallel", "parallel"),
        name="mla_attn_lat" if lq == DEC_SEQ else "mla_attn_ctx",
    )(q, k, v)


def _rope_tables(n_rot, lane0, identity_rows):
    rows = DEC_SEQ // GRID_W
    row = jnp.repeat(jnp.arange(rows, dtype=jnp.int32), GRID_W)
    col = jnp.tile(jnp.arange(GRID_W, dtype=jnp.int32), rows)
    n = n_rot // 2
    freqs = ROPE_THETA ** (-jnp.arange(0, n, 2, dtype=F32) / n)
    zeros = jnp.zeros((DEC_SEQ, n // 2), F32)
    cos_parts, up_parts, dn_parts = [], [], []
    for pos in (row, col):
        ang = pos.astype(F32)[:, None] * freqs[None, :]
        c, s = jnp.cos(ang), jnp.sin(ang)
        cos_parts += [c, c]
        up_parts += [-s, zeros]
        dn_parts += [zeros, s]
    cos = jnp.concatenate(cos_parts, axis=1)
    up = jnp.concatenate(up_parts, axis=1)
    dn = jnp.concatenate(dn_parts, axis=1)

    def place(t, fill):
        reps = 2 if lane0 == 0 else 1
        body = jnp.concatenate([t] * reps, axis=1)
        left = jnp.full((DEC_SEQ, lane0), fill, F32)
        right = jnp.full((DEC_SEQ, LANES - lane0 - body.shape[1]), fill, F32)
        full = jnp.concatenate([left, body, right], axis=1)
        ident = jnp.full((identity_rows, LANES), fill, F32)
        return jnp.concatenate([ident, full], axis=0)

    return place(cos, 1.0), place(up, 0.0), place(dn, 0.0)


def kernel(x_prompt, x_sample, c, cache_diff_k, cache_diff_v, cache_mla_ckv, cache_mla_kpe, c_ctx, w_mod, b_mod, norm_g, ffn_w_in, ffn_w_out, diff_w_qkv, diff_q_norm, diff_k_norm, diff_lambda, diff_subln, diff_w_o, mla_w_down, mla_q_a_norm, mla_kv_a_norm, mla_w_q_up, mla_w_kv_up, mla_q_norm, mla_k_norm, mla_w_o):
    xs = (x_prompt.reshape(N_CTX, D_MODEL), x_sample.reshape(N_LAT, D_MODEL))
    conds = jnp.concatenate(
        [c_ctx[None, :], c, jnp.zeros((COND_ROWS - 1 - DEC_BATCH, D_MODEL), F32)], axis=0)
    mod = _modulation(conds, w_mod, b_mod)

    w_in = ffn_w_in.astype(BF16)
    w_out = ffn_w_out.astype(BF16)
    diff_tabs = _rope_tables(DIFF_HD, 0, 0)
    mla_tabs = _rope_tables(MLA_ROPE, MLA_NOPE, 0)
    mla_kv_tabs = _rope_tables(MLA_ROPE, MLA_NOPE, PAST_LEN)

    new_diff_k, new_diff_v, new_mla_ckv, new_mla_kpe = [], [], [], []
    for l in range(DEPTH):
        j = l // 2
        mod_l = mod[l]
        (x,) = _ffn(xs, mod_l, norm_g[l, 0], w_in[l, 0], w_out[l, 0], 0)
        if l % 2 == 0:
            w_qkv = diff_w_qkv[j].astype(BF16)
            qg = jnp.tile(diff_q_norm[j], 2).reshape(1, LANES)
            kg = jnp.tile(diff_k_norm[j], 2).reshape(1, LANES)
            sub_g = diff_subln[j].reshape(1, LANES)
            lam_init = 0.8 - 0.6 * math.exp(-0.3 * l)
            g1 = norm_g[l, 1]
            q_c, k_c, v_c = _diff_pre(x, mod_l, g1, w_qkv, qg, kg, None, lat=False)
            q_l, k_l, v_l = _diff_pre(x, mod_l, g1, w_qkv, qg, kg, diff_tabs, lat=True)
            new_diff_k.append(k_c.reshape(BATCH, SEQ, DIFF_HEADS, 2, DIFF_HD))
            new_diff_v.append(v_c.reshape(BATCH, SEQ, DIFF_HEADS, DIFF_VD))
            o_c = _diff_attn(q_c.reshape(BATCH, SEQ, D_MODEL), k_c.reshape(BATCH, SEQ, D_MODEL),
                             v_c.reshape(BATCH, SEQ, D_MODEL), diff_lambda[j], sub_g, lam_init,
                             heads_per_step=DIFF_HEADS)
            o_l = _diff_attn(q_l.reshape(DEC_BATCH, DEC_SEQ, D_MODEL),
                             k_l.reshape(DEC_BATCH, DEC_SEQ, D_MODEL),
                             v_l.reshape(DEC_BATCH, DEC_SEQ, D_MODEL),
                             diff_lambda[j], sub_g, lam_init, heads_per_step=LAT_HEADS_PER_STEP,
                             k_cache=cache_diff_k[:, j].reshape(DEC_BATCH, PAST_LEN, D_MODEL),
                             v_cache=cache_diff_v[:, j].reshape(DEC_BATCH, PAST_LEN, D_MODEL))
            w_o = diff_w_o[j].astype(BF16)
        else:
            wd = mla_w_down[j]
            w_dq = wd[:, :Q_RANK].astype(BF16)
            w_dkv = jnp.pad(wd[:, Q_RANK:], ((0, 0), (0, LANES - MLA_ROPE))).astype(BF16)
            w_q = jnp.pad(mla_w_q_up[j].reshape(Q_RANK, MLA_HEADS, MLA_QK),
                          ((0, 0), (0, 0), (0, LANES - MLA_QK))).reshape(Q_RANK, MLA_HEADS * LANES).astype(BF16)
            wkv = mla_w_kv_up[j].reshape(KV_RANK, MLA_HEADS, MLA_NOPE + MLA_VD)
            w_k = jnp.pad(wkv[:, :, :MLA_NOPE], ((0, 0), (0, 0), (0, LANES - MLA_NOPE))
                          ).reshape(KV_RANK, MLA_HEADS * LANES).astype(BF16)
            w_v = wkv[:, :, MLA_NOPE:].reshape(KV_RANK, MLA_HEADS * MLA_VD).astype(BF16)
            qg = jnp.pad(mla_q_norm[j], (0, LANES - MLA_QK)).reshape(1, LANES)
            kg = jnp.pad(mla_k_norm[j], (0, LANES - MLA_QK)).reshape(1, LANES)
            g1 = norm_g[l, 1]
            pre = functools.partial(_mla_pre, x, mod_l, g1, w_dq, w_dkv, mla_q_a_norm[j],
                                    mla_kv_a_norm[j], w_q, qg)
            q_c, ckv_c, kpe_c = pre(None, lat=False)
            q_l, ckv_l, kpe_l = pre(mla_tabs, lat=True)
            new_mla_ckv.append(ckv_c.reshape(BATCH, SEQ, KV_RANK))
            new_mla_kpe.append(kpe_c[:, :MLA_ROPE].reshape(BATCH, SEQ, MLA_ROPE))
            k_c, v_c = _mla_kv(ckv_c, kpe_c, w_k, w_v, kg, None, lat=False)
            cache_kpe = jnp.pad(cache_mla_kpe[:, j], ((0, 0), (0, 0), (0, LANES - MLA_ROPE)))
            ckv_rows = jnp.concatenate([cache_mla_ckv[:, j], ckv_l.reshape(DEC_BATCH, DEC_SEQ, KV_RANK)],
                                       axis=1).reshape(DEC_BATCH * LAT_KV, KV_RANK)
            kpe_rows = jnp.concatenate([cache_kpe, kpe_l.reshape(DEC_BATCH, DEC_SEQ, LANES)],
                                       axis=1).reshape(DEC_BATCH * LAT_KV, LANES)
            k_a, v_a = _mla_kv(ckv_rows, kpe_rows, w_k, w_v, kg, mla_kv_tabs, lat=True)
            kw = MLA_HEADS * LANES
            o_c = _mla_attn(q_c.reshape(BATCH, SEQ, kw), k_c.reshape(BATCH, SEQ, kw),
                            v_c.reshape(BATCH, SEQ, D_MODEL), pairs_per_step=MLA_HEADS // 2)
            o_l = _mla_attn(q_l.reshape(DEC_BATCH, DEC_SEQ, kw), k_a.reshape(DEC_BATCH, LAT_KV, kw),
                            v_a.reshape(DEC_BATCH, LAT_KV, D_MODEL),
                            pairs_per_step=LAT_HEADS_PER_STEP)
            w_o = mla_w_o[j].astype(BF16)
        os = (o_c.reshape(N_CTX, D_MODEL), o_l.reshape(N_LAT, D_MODEL))
        xs = _ffn((x,), mod_l, norm_g[l, 2], w_in[l, 1], w_out[l, 1], 1, os=os, w_o=w_o,
                  split_out=(l == DEPTH - 1))

    return (xs[0].reshape(BATCH, SEQ, D_MODEL), xs[1].reshape(DEC_BATCH, DEC_SEQ, D_MODEL),
            jnp.stack(new_diff_k, axis=1), jnp.stack(new_diff_v, axis=1),
            jnp.stack(new_mla_ckv, axis=1), jnp.stack(new_mla_kpe, axis=1))
```

```python
import functools
import math

import jax
import jax.numpy as jnp
from jax import lax
from jax.experimental import pallas as pl
from jax.experimental.pallas import tpu as pltpu

D_MODEL = 1024
BATCH = 32
SEQ = 256
DEPTH = 4
DEC_BATCH = 2
DEC_SEQ = 1024
PAST_LEN = 256
GRID_W = 64
N_MOD = 9
D_FF = 2816
DIFF_HEADS = 8
DIFF_HD = 64
DIFF_VD = 128
MLA_HEADS = 16
MLA_NOPE = 64
MLA_ROPE = 32
MLA_QK = MLA_NOPE + MLA_ROPE
MLA_VD = 64
Q_RANK = 768
KV_RANK = 256
ROPE_THETA = 10000.0
EPS = 1e-6
LOG2E = math.log2(math.e)
DIFF_Q_SCALE = DIFF_HD ** -0.5 * LOG2E
MLA_Q_SCALE = MLA_QK ** -0.5 * LOG2E

N_CTX = BATCH * SEQ
N_LAT = DEC_BATCH * DEC_SEQ
N_TOK = N_CTX + N_LAT
LAT_KV = PAST_LEN + DEC_SEQ

LANES = 128
COND_ROWS = 8
TM = 512
FF_CHUNK = 256
TQ = 256
LAT_HEADS_PER_STEP = 4
KVP_TM_CTX = 1024
KVP_TM_LAT = 640
MOD_TN = 2304
VMEM_LIMIT = 56 * 1024 * 1024

F32 = jnp.float32
BF16 = jnp.bfloat16


def _params(*sem):
    return pltpu.CompilerParams(dimension_semantics=sem, vmem_limit_bytes=VMEM_LIMIT)


def _dot(a, b):
    return jnp.dot(a, b, preferred_element_type=F32)


def _dot_nt(a, b):
    return lax.dot_general(a, b, (((1,), (1,)), ((), ())), preferred_element_type=F32)


def _modulate(x, g, shift, scale):
    ms = jnp.mean(x * x, axis=-1, keepdims=True)
    return (x * lax.rsqrt(ms + EPS) * g) * (1.0 + scale) + shift


def _rope(x, cos, s_up, s_dn, shift):
    return (x * cos + pltpu.roll(x, LANES - shift, 1) * s_up
            + pltpu.roll(x, shift, 1) * s_dn)


def _cond_of_tile(i, first_lat_tile, tiles_per_lat_batch):
    lat = jnp.maximum(i - first_lat_tile, 0) // tiles_per_lat_batch
    return jnp.where(i < first_lat_tile, 0, 1 + lat)


def _mod_kernel(c_ref, w_ref, b_ref, o_ref):
    c = c_ref[...]
    s = (c * jax.nn.sigmoid(c)).astype(BF16)
    o_ref[...] = _dot(s, w_ref[...].astype(BF16)) + b_ref[...]


def _modulation(conds, w_mod, b_mod):
    n_out = N_MOD * D_MODEL
    out = pl.pallas_call(
        _mod_kernel,
        grid=(DEPTH, n_out // MOD_TN),
        in_specs=[
            pl.BlockSpec((COND_ROWS, D_MODEL), lambda l, n: (0, 0)),
            pl.BlockSpec((None, D_MODEL, MOD_TN), lambda l, n: (l, 0, n)),
            pl.BlockSpec((None, 1, MOD_TN), lambda l, n: (l, 0, n)),
        ],
        out_specs=pl.BlockSpec((None, COND_ROWS, MOD_TN), lambda l, n: (l, 0, n)),
        out_shape=jax.ShapeDtypeStruct((DEPTH, COND_ROWS, n_out), F32),
        compiler_params=_params("parallel", "parallel"),
        name="modulation",
    )(conds, w_mod, b_mod.reshape(DEPTH, 1, n_out))
    return out.reshape(DEPTH, COND_ROWS, N_MOD, D_MODEL)


FIRST_LAT_TILE = N_CTX // TM
TILES_PER_LAT_BATCH = DEC_SEQ // TM


def _ctx_tile(i):
    return (jnp.minimum(i, FIRST_LAT_TILE - 1), 0)


def _lat_tile(i):
    return (jnp.maximum(i - FIRST_LAT_TILE, 0), 0)


def _pick_rows(i, ctx_ref, lat_ref):
    rows = i * TM + lax.broadcasted_iota(jnp.int32, (TM, 1), 0)
    return jnp.where(rows < N_CTX, ctx_ref[...], lat_ref[...])


def _ffn_kernel(*refs, n_x, proj, n_out, convert, i_shift, i_scale, i_gate, i_pgate):
    refs = list(refs)
    x_refs = [refs.pop(0) for _ in range(n_x)]
    if proj:
        oc_ref, ol_ref, wo_ref = refs[:3]
        refs = refs[3:]
    mod_ref, g_ref, win_ref, wout_ref = refs[:4]
    refs = refs[4:]
    if convert:
        nin_ref, nout_ref = refs[:2]
        refs = refs[2:]
    out_refs = refs[:n_out]
    refs = refs[n_out:]
    if convert:
        nin_bf_ref, nout_bf_ref = refs[:2]
        refs = refs[2:]
    a_ref = refs[0]
    i = pl.program_id(0)
    if convert:
        @pl.when(i < FIRST_LAT_TILE)
        def _():
            nin_bf_ref[...] = nin_ref[...].astype(BF16)
            nout_bf_ref[...] = nout_ref[...].astype(BF16)
    x = x_refs[0][...] if n_x == 1 else _pick_rows(i, *x_refs)
    if proj:
        o = _pick_rows(i, oc_ref, ol_ref)
        x = x + mod_ref[i_pgate:i_pgate + 1, :] * _dot(o, wo_ref[...])
    h = _modulate(x, g_ref[...], mod_ref[i_shift:i_shift + 1, :],
                  mod_ref[i_scale:i_scale + 1, :]).astype(BF16)
    for c in range(D_FF // FF_CHUNK):
        lo = c * FF_CHUNK
        g = _dot(h, win_ref[:, lo:lo + FF_CHUNK])
        u = _dot(h, win_ref[:, D_FF + lo:D_FF + lo + FF_CHUNK])
        a_ref[:, lo:lo + FF_CHUNK] = ((g * jax.nn.sigmoid(g)) * u).astype(BF16)
    ff = _dot(a_ref[...], wout_ref[...])
    y = x + mod_ref[i_gate:i_gate + 1, :] * (0.5 * ff)
    if n_out == 1:
        out_refs[0][...] = y
    else:
        @pl.when(i < FIRST_LAT_TILE)
        def _():
            out_refs[0][...] = y

        @pl.when(i >= FIRST_LAT_TILE)
        def _():
            out_refs[1][...] = y


def _ffn(xs, mod_l, g, w_in, w_out, which, os=None, w_o=None, split_out=False, nxt=None):
    proj = os is not None
    convert = nxt is not None
    base = 0 if which == 0 else 6
    row = lambda i: (i, 0)
    full = lambda i: (0, 0)
    tile = (TM, D_MODEL)
    if len(xs) == 1:
        in_specs = [pl.BlockSpec(tile, row)]
    else:
        in_specs = [pl.BlockSpec(tile, _ctx_tile), pl.BlockSpec(tile, _lat_tile)]
    args = list(xs)
    if proj:
        in_specs += [pl.BlockSpec(tile, _ctx_tile), pl.BlockSpec(tile, _lat_tile),
                     pl.BlockSpec((D_MODEL, D_MODEL), full)]
        args += [os[0], os[1], w_o]
    in_specs += [
        pl.BlockSpec((None, N_MOD, D_MODEL),
                     lambda i: (_cond_of_tile(i, FIRST_LAT_TILE, TILES_PER_LAT_BATCH), 0, 0)),
        pl.BlockSpec((1, D_MODEL), full),
        pl.BlockSpec((D_MODEL, 2 * D_FF), full),
        pl.BlockSpec((D_FF, D_MODEL), full),
    ]
    args += [mod_l, g.reshape(1, D_MODEL), w_in, w_out]
    in_slab = (D_MODEL // FIRST_LAT_TILE, 2 * D_FF)
    out_slab = (D_FF // FIRST_LAT_TILE, D_MODEL)
    if convert:
        nw_in, nw_out, nl, nw = nxt
        slab = lambda i: (nl, nw, jnp.minimum(i, FIRST_LAT_TILE - 1), 0)
        in_specs += [pl.BlockSpec((None, None) + in_slab, slab),
                     pl.BlockSpec((None, None) + out_slab, slab)]
        args += [nw_in, nw_out]
    if split_out:
        out_specs = [pl.BlockSpec(tile, _ctx_tile), pl.BlockSpec(tile, _lat_tile)]
        out_shape = [jax.ShapeDtypeStruct((N_CTX, D_MODEL), F32),
                     jax.ShapeDtypeStruct((N_LAT, D_MODEL), F32)]
    else:
        out_specs = [pl.BlockSpec(tile, row)]
        out_shape = [jax.ShapeDtypeStruct((N_TOK, D_MODEL), F32)]
    n_out = len(out_shape)
    if convert:
        out_specs += [pl.BlockSpec(in_slab, _ctx_tile), pl.BlockSpec(out_slab, _ctx_tile)]
        out_shape += [jax.ShapeDtypeStruct((D_MODEL, 2 * D_FF), BF16),
                      jax.ShapeDtypeStruct((D_FF, D_MODEL), BF16)]
    kern = functools.partial(_ffn_kernel, n_x=len(xs), proj=proj, n_out=n_out, convert=convert,
                             i_shift=base, i_scale=base + 1, i_gate=base + 2, i_pgate=5)
    return pl.pallas_call(
        kern,
        grid=(N_TOK // TM,),
        in_specs=in_specs,
        out_specs=out_specs,
        out_shape=out_shape,
        scratch_shapes=[pltpu.VMEM((TM, D_FF), BF16)],
        compiler_params=_params("arbitrary"),
        name="ffn_proj" if proj else "ffn",
    )(*args)


def _diff_pre_kernel(*refs, rope):
    if rope:
        (x_ref, mod_ref, g_ref, w_ref, qg_ref, kg_ref, cos_ref, up_ref, dn_ref,
         q_ref, k_ref, v_ref) = refs
    else:
        x_ref, mod_ref, g_ref, w_ref, qg_ref, kg_ref, q_ref, k_ref, v_ref = refs
    h = _modulate(x_ref[...], g_ref[...], mod_ref[3:4, :], mod_ref[4:5, :]).astype(BF16)
    hw = DIFF_HEADS * DIFF_VD
    v_ref[...] = _dot(h, w_ref[:, 2 * hw:3 * hw])
    lane = lax.broadcasted_iota(jnp.int32, (1, LANES), 1)
    lo = lane < DIFF_HD
    qg = qg_ref[...]
    kg = kg_ref[...]
    for part, (gain, out_ref, scale) in enumerate(((qg, q_ref, DIFF_Q_SCALE), (kg, k_ref, None))):
        y = _dot(h, w_ref[:, part * hw:(part + 1) * hw])
        for hd in range(DIFF_HEADS):
            sl = slice(hd * LANES, (hd + 1) * LANES)
            yh = y[:, sl]
            sq = yh * yh
            s_lo = jnp.sum(jnp.where(lo, sq, 0.0), axis=-1, keepdims=True)
            s_hi = jnp.sum(jnp.where(lo, 0.0, sq), axis=-1, keepdims=True)
            ms = jnp.where(lo, s_lo, s_hi) * (1.0 / DIFF_HD)
            yn = yh * lax.rsqrt(ms + EPS) * gain
            if rope:
                yn = _rope(yn, cos_ref[...], up_ref[...], dn_ref[...], DIFF_HD // 4)
            if scale is not None:
                yn = yn * scale
            out_ref[:, sl] = yn.astype(out_ref.dtype)


def _diff_pre(x, mod_l, g, w_qkv, qg, kg, tables, lat):
    n_rows = N_LAT if lat else N_CTX
    tile0 = N_CTX // TM if lat else 0
    per_lat = DEC_SEQ // TM
    full = lambda i: (0, 0)
    xrow = lambda i: (i + tile0, 0)
    row = lambda i: (i, 0)
    if lat:
        cond = lambda i: (1 + i // per_lat, 0, 0)
    else:
        cond = lambda i: (0, 0, 0)
    in_specs = [
        pl.BlockSpec((TM, D_MODEL), xrow),
        pl.BlockSpec((None, N_MOD, D_MODEL), cond),
        pl.BlockSpec((1, D_MODEL), full),
        pl.BlockSpec((D_MODEL, 3 * D_MODEL), full),
        pl.BlockSpec((1, LANES), full),
        pl.BlockSpec((1, LANES), full),
    ]
    args = [x, mod_l, g.reshape(1, D_MODEL), w_qkv, qg, kg]
    if lat:
        pos = lambda i: (i % per_lat, 0)
        in_specs += [pl.BlockSpec((TM, LANES), pos)] * 3
        args += list(tables)
    return pl.pallas_call(
        functools.partial(_diff_pre_kernel, rope=lat),
        grid=(n_rows // TM,),
        in_specs=in_specs,
        out_specs=[pl.BlockSpec((TM, D_MODEL), row)] * 3,
        out_shape=[jax.ShapeDtypeStruct((n_rows, D_MODEL), BF16),
                   jax.ShapeDtypeStruct((n_rows, D_MODEL), F32),
                   jax.ShapeDtypeStruct((n_rows, D_MODEL), F32)],
        compiler_params=_params("parallel"),
        name="diff_pre_lat" if lat else "diff_pre_ctx",
    )(*args)


def _softmax_pv(s, v_aug):
    e = jnp.exp2(s - jnp.max(s, axis=-1, keepdims=True)).astype(BF16)
    r = _dot(e, v_aug)
    return r[:, :LANES] * (1.0 / r[:, LANES:])


def _diff_attn_kernel(*refs, heads, lam_init, cached):
    if cached:
        q_ref, kc_ref, k_ref, vc_ref, v_ref, lam_ref, sub_ref, o_ref = refs
    else:
        q_ref, k_ref, v_ref, lam_ref, sub_ref, o_ref = refs
    lp = lam_ref[...]
    lam = (jnp.exp(jnp.sum(lp[0:1, :] * lp[1:2, :], axis=-1, keepdims=True))
           - jnp.exp(jnp.sum(lp[2:3, :] * lp[3:4, :], axis=-1, keepdims=True)) + lam_init)
    tq = q_ref.shape[0]
    lane = lax.broadcasted_iota(jnp.int32, (1, LANES), 1)
    lo = lane < DIFF_HD
    sub_g = sub_ref[...]
    for hd in range(heads):
        sl = slice(hd * LANES, (hd + 1) * LANES)
        qh = q_ref[:, sl]
        kh = k_ref[:, sl].astype(BF16)
        vh = v_ref[:, sl].astype(BF16)
        if cached:
            kh = jnp.concatenate([kc_ref[:, sl].astype(BF16), kh], axis=0)
            vh = jnp.concatenate([vc_ref[:, sl].astype(BF16), vh], axis=0)
        v_aug = jnp.concatenate([vh, jnp.ones_like(vh)], axis=1)
        zero = jnp.zeros_like(qh)
        qq = jnp.concatenate([jnp.where(lo, qh, zero), jnp.where(lo, zero, qh)], axis=0)
        o12 = _softmax_pv(_dot_nt(qq, kh), v_aug)
        o = o12[:tq] - lam * o12[tq:]
        ms = jnp.mean(o * o, axis=-1, keepdims=True)
        o = (o * lax.rsqrt(ms + EPS) * sub_g) * (1.0 - lam_init)
        o_ref[:, sl] = o.astype(o_ref.dtype)


def _diff_attn(q, k, v, lam_p, sub_g, lam_init, heads_per_step, k_cache=None, v_cache=None):
    b, lq, _ = q.shape
    lk = k.shape[1]
    cached = k_cache is not None
    w = heads_per_step * LANES
    n_hg = DIFF_HEADS // heads_per_step
    tq = min(TQ, lq)
    qmap = lambda bi, hi, qi: (bi, qi, hi)
    kmap = lambda bi, hi, qi: (bi, 0, hi)
    full = lambda bi, hi, qi: (0, 0)
    q_spec = pl.BlockSpec((None, tq, w), qmap)
    kv_spec = pl.BlockSpec((None, lk, w), kmap)
    c_spec = pl.BlockSpec((None, PAST_LEN, w), kmap)
    if cached:
        in_specs = [q_spec, c_spec, kv_spec, c_spec, kv_spec]
        args = [q, k_cache, k, v_cache, v]
    else:
        in_specs = [q_spec, kv_spec, kv_spec]
        args = [q, k, v]
    in_specs += [pl.BlockSpec((4, DIFF_HD), full), pl.BlockSpec((1, LANES), full)]
    args += [lam_p, sub_g]
    return pl.pallas_call(
        functools.partial(_diff_attn_kernel, heads=heads_per_step, lam_init=lam_init, cached=cached),
        grid=(b, n_hg, lq // tq),
        in_specs=in_specs,
        out_specs=pl.BlockSpec((None, tq, w), qmap),
        out_shape=jax.ShapeDtypeStruct((b, lq, D_MODEL), BF16),
        compiler_params=_params("parallel", "parallel", "parallel"),
        name="diff_attn_lat" if cached else "diff_attn_ctx",
    )(*args)


def _mla_pre_kernel(*refs, rope):
    if rope:
        (x_ref, mod_ref, g_ref, wdq_ref, wdkv_ref, qag_ref, kvag_ref, wq_ref, qg_ref,
         cos_ref, up_ref, dn_ref, q_ref, ckv_ref, kpe_ref) = refs
    else:
        (x_ref, mod_ref, g_ref, wdq_ref, wdkv_ref, qag_ref, kvag_ref, wq_ref, qg_ref,
         q_ref, ckv_ref, kpe_ref) = refs
    h = _modulate(x_ref[...], g_ref[...], mod_ref[3:4, :], mod_ref[4:5, :]).astype(BF16)
    d2 = _dot(h, wdkv_ref[...])
    ckv = d2[:, :KV_RANK]
    ms = jnp.mean(ckv * ckv, axis=-1, keepdims=True)
    ckv_ref[...] = ckv * lax.rsqrt(ms + EPS) * kvag_ref[...]
    kpe_ref[...] = d2[:, KV_RANK:]
    cq = _dot(h, wdq_ref[...])
    ms = jnp.mean(cq * cq, axis=-1, keepdims=True)
    cqn = (cq * lax.rsqrt(ms + EPS) * qag_ref[...]).astype(BF16)
    q = _dot(cqn, wq_ref[...])
    qg = qg_ref[...]
    for hd in range(MLA_HEADS):
        sl = slice(hd * LANES, (hd + 1) * LANES)
        qh = q[:, sl]
        ms = jnp.sum(qh * qh, axis=-1, keepdims=True) * (1.0 / MLA_QK)
        qn = qh * lax.rsqrt(ms + EPS) * qg
        if rope:
            qn = _rope(qn, cos_ref[...], up_ref[...], dn_ref[...], MLA_ROPE // 4)
        q_ref[:, sl] = (qn * MLA_Q_SCALE).astype(q_ref.dtype)


def _mla_pre(x, mod_l, g, w_dq, w_dkv, qag, kvag, w_q, qg, tables, lat):
    n_rows = N_LAT if lat else N_CTX
    tile0 = N_CTX // TM if lat else 0
    per_lat = DEC_SEQ // TM
    full = lambda i: (0, 0)
    xrow = lambda i: (i + tile0, 0)
    row = lambda i: (i, 0)
    if lat:
        cond = lambda i: (1 + i // per_lat, 0, 0)
    else:
        cond = lambda i: (0, 0, 0)
    kvw = KV_RANK + LANES
    qw = MLA_HEADS * LANES
    in_specs = [
        pl.BlockSpec((TM, D_MODEL), xrow),
        pl.BlockSpec((None, N_MOD, D_MODEL), cond),
        pl.BlockSpec((1, D_MODEL), full),
        pl.BlockSpec((D_MODEL, Q_RANK), full),
        pl.BlockSpec((D_MODEL, kvw), full),
        pl.BlockSpec((1, Q_RANK), full),
        pl.BlockSpec((1, KV_RANK), full),
        pl.BlockSpec((Q_RANK, qw), full),
        pl.BlockSpec((1, LANES), full),
    ]
    args = [x, mod_l, g.reshape(1, D_MODEL), w_dq, w_dkv, qag.reshape(1, Q_RANK),
            kvag.reshape(1, KV_RANK), w_q, qg]
    if lat:
        pos = lambda i: (i % per_lat, 0)
        in_specs += [pl.BlockSpec((TM, LANES), pos)] * 3
        args += list(tables)
    return pl.pallas_call(
        functools.partial(_mla_pre_kernel, rope=lat),
        grid=(n_rows // TM,),
        in_specs=in_specs,
        out_specs=[pl.BlockSpec((TM, qw), row), pl.BlockSpec((TM, KV_RANK), row),
                   pl.BlockSpec((TM, LANES), row)],
        out_shape=[jax.ShapeDtypeStruct((n_rows, qw), BF16),
                   jax.ShapeDtypeStruct((n_rows, KV_RANK), F32),
                   jax.ShapeDtypeStruct((n_rows, LANES), F32)],
        compiler_params=_params("parallel"),
        name="mla_pre_lat" if lat else "mla_pre_ctx",
    )(*args)


def _mla_kv_kernel(*refs, rope):
    if rope:
        ckv_ref, kpe_ref, wk_ref, wv_ref, kg_ref, cos_ref, up_ref, dn_ref, k_ref, v_ref = refs
    else:
        ckv_ref, kpe_ref, wk_ref, wv_ref, kg_ref, k_ref, v_ref = refs
    c = ckv_ref[...].astype(BF16)
    v_ref[...] = _dot(c, wv_ref[...]).astype(v_ref.dtype)
    kk = _dot(c, wk_ref[...])
    kpe = pltpu.roll(kpe_ref[...], MLA_NOPE, 1)
    kg = kg_ref[...]
    for hd in range(MLA_HEADS):
        sl = slice(hd * LANES, (hd + 1) * LANES)
        kh = kk[:, sl] + kpe
        ms = jnp.sum(kh * kh, axis=-1, keepdims=True) * (1.0 / MLA_QK)
        kn = kh * lax.rsqrt(ms + EPS) * kg
        if rope:
            kn = _rope(kn, cos_ref[...], up_ref[...], dn_ref[...], MLA_ROPE // 4)
        k_ref[:, sl] = kn.astype(k_ref.dtype)


def _mla_kv(ckv_rows, kpe_rows, w_k, w_v, kg, tables, lat):
    n_rows = ckv_rows.shape[0]
    tm = KVP_TM_LAT if lat else KVP_TM_CTX
    full = lambda i: (0, 0)
    row = lambda i: (i, 0)
    kw = MLA_HEADS * LANES
    vw = MLA_HEADS * MLA_VD
    in_specs = [
        pl.BlockSpec((tm, KV_RANK), row),
        pl.BlockSpec((tm, LANES), row),
        pl.BlockSpec((KV_RANK, kw), full),
        pl.BlockSpec((KV_RANK, vw), full),
        pl.BlockSpec((1, LANES), full),
    ]
    args = [ckv_rows, kpe_rows, w_k, w_v, kg]
    if lat:
        per = LAT_KV // tm
        pos = lambda i: (i % per, 0)
        in_specs += [pl.BlockSpec((tm, LANES), pos)] * 3
        args += list(tables)
    return pl.pallas_call(
        functools.partial(_mla_kv_kernel, rope=lat),
        grid=(n_rows // tm,),
        in_specs=in_specs,
        out_specs=[pl.BlockSpec((tm, kw), row), pl.BlockSpec((tm, vw), row)],
        out_shape=[jax.ShapeDtypeStruct((n_rows, kw), BF16),
                   jax.ShapeDtypeStruct((n_rows, vw), BF16)],
        compiler_params=_params("parallel"),
        name="mla_kv_lat" if lat else "mla_kv_ctx",
    )(*args)


def _mla_attn_kernel(q_ref, k_ref, v_ref, o_ref, *, pairs):
    lane = lax.broadcasted_iota(jnp.int32, (1, LANES), 1)
    lo = lane < MLA_VD
    for p in range(pairs):
        vp = v_ref[:, p * LANES:(p + 1) * LANES]
        v_aug = jnp.concatenate([vp, jnp.ones_like(vp)], axis=1)
        outs = []
        for sub in range(2):
            hd = 2 * p + sub
            sl = slice(hd * LANES, (hd + 1) * LANES)
            outs.append(_softmax_pv(_dot_nt(q_ref[:, sl], k_ref[:, sl]), v_aug))
        o_ref[:, p * LANES:(p + 1) * LANES] = jnp.where(lo, outs[0], outs[1]).astype(o_ref.dtype)


def _mla_attn(q, k, v, pairs_per_step):
    b, lq, _ = q.shape
    lk = k.shape[1]
    n_pg = MLA_HEADS // 2 // pairs_per_step
    tq = min(TQ, lq)
    qkw = pairs_per_step * 2 * LANES
    ow = pairs_per_step * LANES
    qmap = lambda bi, hi, qi: (bi, qi, hi)
    kmap = lambda bi, hi, qi: (bi, 0, hi)
    return pl.pallas_call(
        functools.partial(_mla_attn_kernel, pairs=pairs_per_step),
        grid=(b, n_pg, lq // tq),
        in_specs=[
            pl.BlockSpec((None, tq, qkw), qmap),
            pl.BlockSpec((None, lk, qkw), kmap),
            pl.BlockSpec((None, lk, ow), kmap),
        ],
        out_specs=pl.BlockSpec((None, tq, ow), qmap),
        out_shape=jax.ShapeDtypeStruct((b, lq, D_MODEL), BF16),
        compiler_params=_params("parallel", "parallel", "parallel"),
        name="mla_attn_lat" if lq == DEC_SEQ else "mla_attn_ctx",
    )(q, k, v)


def _rope_tables(n_rot, lane0, identity_rows):
    rows = DEC_SEQ // GRID_W
    row = jnp.repeat(jnp.arange(rows, dtype=jnp.int32), GRID_W)
    col = jnp.tile(jnp.arange(GRID_W, dtype=jnp.int32), rows)
    n = n_rot // 2
    freqs = ROPE_THETA ** (-jnp.arange(0, n, 2, dtype=F32) / n)
    zeros = jnp.zeros((DEC_SEQ, n // 2), F32)
    cos_parts, up_parts, dn_parts = [], [], []
    for pos in (row, col):
        ang = pos.astype(F32)[:, None] * freqs[None, :]
        c, s = jnp.cos(ang), jnp.sin(ang)
        cos_parts += [c, c]
        up_parts += [-s, zeros]
        dn_parts += [zeros, s]
    cos = jnp.concatenate(cos_parts, axis=1)
    up = jnp.concatenate(up_parts, axis=1)
    dn = jnp.concatenate(dn_parts, axis=1)

    def place(t, fill):
        reps = 2 if lane0 == 0 else 1
        body = jnp.concatenate([t] * reps, axis=1)
        left = jnp.full((DEC_SEQ, lane0), fill, F32)
        right = jnp.full((DEC_SEQ, LANES - lane0 - body.shape[1]), fill, F32)
        full = jnp.concatenate([left, body, right], axis=1)
        ident = jnp.full((identity_rows, LANES), fill, F32)
        return jnp.concatenate([ident, full], axis=0)

    return place(cos, 1.0), place(up, 0.0), place(dn, 0.0)


def kernel(x_prompt, x_sample, c, cache_diff_k, cache_diff_v, cache_mla_ckv, cache_mla_kpe, c_ctx, w_mod, b_mod, norm_g, ffn_w_in, ffn_w_out, diff_w_qkv, diff_q_norm, diff_k_norm, diff_lambda, diff_subln, diff_w_o, mla_w_down, mla_q_a_norm, mla_kv_a_norm, mla_w_q_up, mla_w_kv_up, mla_q_norm, mla_k_norm, mla_w_o):
    xs = (x_prompt.reshape(N_CTX, D_MODEL), x_sample.reshape(N_LAT, D_MODEL))
    conds = jnp.concatenate(
        [c_ctx[None, :], c, jnp.zeros((COND_ROWS - 1 - DEC_BATCH, D_MODEL), F32)], axis=0)
    mod = _modulation(conds, w_mod, b_mod)

    w_in = ffn_w_in[0, 0].astype(BF16)
    w_out = ffn_w_out[0, 0].astype(BF16)
    diff_tabs = _rope_tables(DIFF_HD, 0, 0)
    mla_tabs = _rope_tables(MLA_ROPE, MLA_NOPE, 0)
    mla_kv_tabs = _rope_tables(MLA_ROPE, MLA_NOPE, PAST_LEN)

    new_diff_k, new_diff_v, new_mla_ckv, new_mla_kpe = [], [], [], []
    for l in range(DEPTH):
        j = l // 2
        mod_l = mod[l]
        x, w_in, w_out = _ffn(xs, mod_l, norm_g[l, 0], w_in, w_out, 0,
                              nxt=(ffn_w_in, ffn_w_out, l, 1))
        if l % 2 == 0:
            w_qkv = diff_w_qkv[j].astype(BF16)
            qg = jnp.tile(diff_q_norm[j], 2).reshape(1, LANES)
            kg = jnp.tile(diff_k_norm[j], 2).reshape(1, LANES)
            sub_g = diff_subln[j].reshape(1, LANES)
            lam_init = 0.8 - 0.6 * math.exp(-0.3 * l)
            g1 = norm_g[l, 1]
            q_c, k_c, v_c = _diff_pre(x, mod_l, g1, w_qkv, qg, kg, None, lat=False)
            q_l, k_l, v_l = _diff_pre(x, mod_l, g1, w_qkv, qg, kg, diff_tabs, lat=True)
            new_diff_k.append(k_c.reshape(BATCH, SEQ, DIFF_HEADS, 2, DIFF_HD))
            new_diff_v.append(v_c.reshape(BATCH, SEQ, DIFF_HEADS, DIFF_VD))
            o_c = _diff_attn(q_c.reshape(BATCH, SEQ, D_MODEL), k_c.reshape(BATCH, SEQ, D_MODEL),
                             v_c.reshape(BATCH, SEQ, D_MODEL), diff_lambda[j], sub_g, lam_init,
                             heads_per_step=DIFF_HEADS)
            o_l = _diff_attn(q_l.reshape(DEC_BATCH, DEC_SEQ, D_MODEL),
                             k_l.reshape(DEC_BATCH, DEC_SEQ, D_MODEL),
                             v_l.reshape(DEC_BATCH, DEC_SEQ, D_MODEL),
                             diff_lambda[j], sub_g, lam_init, heads_per_step=LAT_HEADS_PER_STEP,
                             k_cache=cache_diff_k[:, j].reshape(DEC_BATCH, PAST_LEN, D_MODEL),
                             v_cache=cache_diff_v[:, j].reshape(DEC_BATCH, PAST_LEN, D_MODEL))
            w_o = diff_w_o[j].astype(BF16)
        else:
            wd = mla_w_down[j]
            w_dq = wd[:, :Q_RANK].astype(BF16)
            w_dkv = jnp.pad(wd[:, Q_RANK:], ((0, 0), (0, LANES - MLA_ROPE))).astype(BF16)
            w_q = jnp.pad(mla_w_q_up[j].reshape(Q_RANK, MLA_HEADS, MLA_QK),
                          ((0, 0), (0, 0), (0, LANES - MLA_QK))).reshape(Q_RANK, MLA_HEADS * LANES).astype(BF16)
            wkv = mla_w_kv_up[j].reshape(KV_RANK, MLA_HEADS, MLA_NOPE + MLA_VD)
            w_k = jnp.pad(wkv[:, :, :MLA_NOPE], ((0, 0), (0, 0), (0, LANES - MLA_NOPE))
                          ).reshape(KV_RANK, MLA_HEADS * LANES).astype(BF16)
            w_v = wkv[:, :, MLA_NOPE:].reshape(KV_RANK, MLA_HEADS * MLA_VD).astype(BF16)
            qg = jnp.pad(mla_q_norm[j], (0, LANES - MLA_QK)).reshape(1, LANES)
            kg = jnp.pad(mla_k_norm[j], (0, LANES - MLA_QK)).reshape(1, LANES)
            g1 = norm_g[l, 1]
            pre = functools.partial(_mla_pre, x, mod_l, g1, w_dq, w_dkv, mla_q_a_norm[j],
                                    mla_kv_a_norm[j], w_q, qg)
            q_c, ckv_c, kpe_c = pre(None, lat=False)
            q_l, ckv_l, kpe_l = pre(mla_tabs, lat=True)
            new_mla_ckv.append(ckv_c.reshape(BATCH, SEQ, KV_RANK))
            new_mla_kpe.append(kpe_c[:, :MLA_ROPE].reshape(BATCH, SEQ, MLA_ROPE))
            k_c, v_c = _mla_kv(ckv_c, kpe_c, w_k, w_v, kg, None, lat=False)
            cache_kpe = jnp.pad(cache_mla_kpe[:, j], ((0, 0), (0, 0), (0, LANES - MLA_ROPE)))
            ckv_rows = jnp.concatenate([cache_mla_ckv[:, j], ckv_l.reshape(DEC_BATCH, DEC_SEQ, KV_RANK)],
                                       axis=1).reshape(DEC_BATCH * LAT_KV, KV_RANK)
            kpe_rows = jnp.concatenate([cache_kpe, kpe_l.reshape(DEC_BATCH, DEC_SEQ, LANES)],
                                       axis=1).reshape(DEC_BATCH * LAT_KV, LANES)
            k_a, v_a = _mla_kv(ckv_rows, kpe_rows, w_k, w_v, kg, mla_kv_tabs, lat=True)
            kw = MLA_HEADS * LANES
            o_c = _mla_attn(q_c.reshape(BATCH, SEQ, kw), k_c.reshape(BATCH, SEQ, kw),
                            v_c.reshape(BATCH, SEQ, D_MODEL), pairs_per_step=MLA_HEADS // 2)
            o_l = _mla_attn(q_l.reshape(DEC_BATCH, DEC_SEQ, kw), k_a.reshape(DEC_BATCH, LAT_KV, kw),
                            v_a.reshape(DEC_BATCH, LAT_KV, D_MODEL),
                            pairs_per_step=LAT_HEADS_PER_STEP)
            w_o = mla_w_o[j].astype(BF16)
        os = (o_c.reshape(N_CTX, D_MODEL), o_l.reshape(N_LAT, D_MODEL))
        last = l == DEPTH - 1
        xs = _ffn((x,), mod_l, norm_g[l, 2], w_in, w_out, 1, os=os, w_o=w_o, split_out=last,
                  nxt=None if last else (ffn_w_in, ffn_w_out, l + 1, 0))
        if not last:
            xs, (w_in, w_out) = xs[:1], xs[1:]

    return (xs[0].reshape(BATCH, SEQ, D_MODEL), xs[1].reshape(DEC_BATCH, DEC_SEQ, D_MODEL),
            jnp.stack(new_diff_k, axis=1), jnp.stack(new_diff_v, axis=1),
            jnp.stack(new_mla_ckv, axis=1), jnp.stack(new_mla_kpe, axis=1))
```

```python
import functools
import math

import jax
import jax.numpy as jnp
import numpy as np
from jax import lax
from jax.experimental import pallas as pl
from jax.experimental.pallas import tpu as pltpu

D_MODEL = 1024
BATCH = 32
SEQ = 256
DEPTH = 4
DEC_BATCH = 2
DEC_SEQ = 1024
PAST_LEN = 256
GRID_W = 64
N_DIFF = (DEPTH + 1) // 2
N_MLA = DEPTH // 2
N_MOD = 9
D_FF = 2816
DIFF_HEADS = 8
DIFF_HD = 64
DIFF_VD = 128
MLA_HEADS = 16
MLA_NOPE = 64
MLA_ROPE = 32
MLA_QK = MLA_NOPE + MLA_ROPE
MLA_VD = 64
Q_RANK = 768
KV_RANK = 256
ROPE_THETA = 10000.0
EPS = 1e-6
LOG2E = math.log2(math.e)
DIFF_Q_SCALE = DIFF_HD ** -0.5 * LOG2E
MLA_Q_SCALE = MLA_QK ** -0.5 * LOG2E

N_CTX = BATCH * SEQ
N_LAT = DEC_BATCH * DEC_SEQ
N_TOK = N_CTX + N_LAT
LAT_KV = PAST_LEN + DEC_SEQ

LANES = 128
COND_ROWS = 8
TM = 512
SEQS_PER_TILE = TM // SEQ
FF_CHUNK = 256
TQ = 256
LAT_HEADS_PER_STEP = 4
KVP_TM_LAT = 640
MOD_TN = 2304
VMEM_LIMIT = 56 * 1024 * 1024

F32 = jnp.float32
BF16 = jnp.bfloat16


def _params(*sem):
    return pltpu.CompilerParams(dimension_semantics=sem, vmem_limit_bytes=VMEM_LIMIT)


def _dot(a, b):
    return jnp.dot(a, b, preferred_element_type=F32)


def _dot_nt(a, b):
    return lax.dot_general(a, b, (((1,), (1,)), ((), ())), preferred_element_type=F32)


def _modulate(x, g, shift, scale):
    ms = jnp.mean(x * x, axis=-1, keepdims=True)
    return (x * lax.rsqrt(ms + EPS) * g) * (1.0 + scale) + shift


def _rope(x, cos, s_up, s_dn, shift):
    return (x * cos + pltpu.roll(x, LANES - shift, 1) * s_up
            + pltpu.roll(x, shift, 1) * s_dn)


def _cond_of_tile(i, first_lat_tile, tiles_per_lat_batch):
    lat = jnp.maximum(i - first_lat_tile, 0) // tiles_per_lat_batch
    return jnp.where(i < first_lat_tile, 0, 1 + lat)


def _mod_kernel(c_ref, w_ref, b_ref, o_ref):
    c = c_ref[...]
    s = (c * jax.nn.sigmoid(c)).astype(BF16)
    o_ref[...] = _dot(s, w_ref[...].astype(BF16)) + b_ref[...]


def _modulation(conds, w_mod, b_mod):
    n_out = N_MOD * D_MODEL
    out = pl.pallas_call(
        _mod_kernel,
        grid=(DEPTH, n_out // MOD_TN),
        in_specs=[
            pl.BlockSpec((COND_ROWS, D_MODEL), lambda l, n: (0, 0)),
            pl.BlockSpec((None, D_MODEL, MOD_TN), lambda l, n: (l, 0, n)),
            pl.BlockSpec((None, 1, MOD_TN), lambda l, n: (l, 0, n)),
        ],
        out_specs=pl.BlockSpec((None, COND_ROWS, MOD_TN), lambda l, n: (l, 0, n)),
        out_shape=jax.ShapeDtypeStruct((DEPTH, COND_ROWS, n_out), F32),
        compiler_params=_params("parallel", "parallel"),
        name="modulation",
    )(conds, w_mod, b_mod.reshape(DEPTH, 1, n_out))
    return out.reshape(DEPTH, COND_ROWS, N_MOD, D_MODEL)


FIRST_LAT_TILE = N_CTX // TM
TILES_PER_LAT_BATCH = DEC_SEQ // TM


def _ctx_tile(i):
    return (jnp.minimum(i, FIRST_LAT_TILE - 1), 0)


def _lat_tile(i):
    return (jnp.maximum(i - FIRST_LAT_TILE, 0), 0)


def _pick_rows(i, ctx_ref, lat_ref):
    rows = i * TM + lax.broadcasted_iota(jnp.int32, (TM, 1), 0)
    return jnp.where(rows < N_CTX, ctx_ref[...], lat_ref[...])


def _ffn_kernel(*refs, n_x, proj, n_out, convert, i_shift, i_scale, i_gate, i_pgate):
    refs = list(refs)
    x_refs = [refs.pop(0) for _ in range(n_x)]
    if proj:
        oc_ref, ol_ref, wo_ref = refs[:3]
        refs = refs[3:]
    mod_ref, g_ref, win_ref, wout_ref = refs[:4]
    refs = refs[4:]
    if convert:
        nin_ref, nout_ref = refs[:2]
        refs = refs[2:]
    out_refs = refs[:n_out]
    refs = refs[n_out:]
    if convert:
        nin_bf_ref, nout_bf_ref = refs[:2]
        refs = refs[2:]
    a_ref = refs[0]
    i = pl.program_id(0)
    if convert:
        @pl.when(i < FIRST_LAT_TILE)
        def _():
            nin_bf_ref[...] = nin_ref[...].astype(BF16)
            nout_bf_ref[...] = nout_ref[...].astype(BF16)
    x = x_refs[0][...] if n_x == 1 else _pick_rows(i, *x_refs)
    if proj:
        o = _pick_rows(i, oc_ref, ol_ref)
        x = x + mod_ref[i_pgate:i_pgate + 1, :] * _dot(o, wo_ref[...])
    h = _modulate(x, g_ref[...], mod_ref[i_shift:i_shift + 1, :],
                  mod_ref[i_scale:i_scale + 1, :]).astype(BF16)
    for c in range(D_FF // FF_CHUNK):
        lo = c * FF_CHUNK
        g = _dot(h, win_ref[:, lo:lo + FF_CHUNK])
        u = _dot(h, win_ref[:, D_FF + lo:D_FF + lo + FF_CHUNK])
        a_ref[:, lo:lo + FF_CHUNK] = ((g * jax.nn.sigmoid(g)) * u).astype(BF16)
    ff = _dot(a_ref[...], wout_ref[...])
    y = x + mod_ref[i_gate:i_gate + 1, :] * (0.5 * ff)
    if n_out == 1:
        out_refs[0][...] = y
    else:
        @pl.when(i < FIRST_LAT_TILE)
        def _():
            out_refs[0][...] = y

        @pl.when(i >= FIRST_LAT_TILE)
        def _():
            out_refs[1][...] = y


def _ffn(xs, mod_l, g, w_in, w_out, which, os=None, w_o=None, split_out=False, nxt=None):
    proj = os is not None
    convert = nxt is not None
    base = 0 if which == 0 else 6
    row = lambda i: (i, 0)
    full = lambda i: (0, 0)
    tile = (TM, D_MODEL)
    if len(xs) == 1:
        in_specs = [pl.BlockSpec(tile, row)]
    else:
        in_specs = [pl.BlockSpec(tile, _ctx_tile), pl.BlockSpec(tile, _lat_tile)]
    args = list(xs)
    if proj:
        in_specs += [pl.BlockSpec(tile, _ctx_tile), pl.BlockSpec(tile, _lat_tile),
                     pl.BlockSpec((D_MODEL, D_MODEL), full)]
        args += [os[0], os[1], w_o]
    in_specs += [
        pl.BlockSpec((None, N_MOD, D_MODEL),
                     lambda i: (_cond_of_tile(i, FIRST_LAT_TILE, TILES_PER_LAT_BATCH), 0, 0)),
        pl.BlockSpec((1, D_MODEL), full),
        pl.BlockSpec((D_MODEL, 2 * D_FF), full),
        pl.BlockSpec((D_FF, D_MODEL), full),
    ]
    args += [mod_l, g.reshape(1, D_MODEL), w_in, w_out]
    in_slab = (D_MODEL // FIRST_LAT_TILE, 2 * D_FF)
    out_slab = (D_FF // FIRST_LAT_TILE, D_MODEL)
    if convert:
        nw_in, nw_out, nl, nw = nxt
        slab = lambda i: (nl, nw, jnp.minimum(i, FIRST_LAT_TILE - 1), 0)
        in_specs += [pl.BlockSpec((None, None) + in_slab, slab),
                     pl.BlockSpec((None, None) + out_slab, slab)]
        args += [nw_in, nw_out]
    if split_out:
        out_specs = [pl.BlockSpec(tile, _ctx_tile), pl.BlockSpec(tile, _lat_tile)]
        out_shape = [jax.ShapeDtypeStruct((N_CTX, D_MODEL), F32),
                     jax.ShapeDtypeStruct((N_LAT, D_MODEL), F32)]
    else:
        out_specs = [pl.BlockSpec(tile, row)]
        out_shape = [jax.ShapeDtypeStruct((N_TOK, D_MODEL), F32)]
    n_out = len(out_shape)
    if convert:
        out_specs += [pl.BlockSpec(in_slab, _ctx_tile), pl.BlockSpec(out_slab, _ctx_tile)]
        out_shape += [jax.ShapeDtypeStruct((D_MODEL, 2 * D_FF), BF16),
                      jax.ShapeDtypeStruct((D_FF, D_MODEL), BF16)]
    kern = functools.partial(_ffn_kernel, n_x=len(xs), proj=proj, n_out=n_out, convert=convert,
                             i_shift=base, i_scale=base + 1, i_gate=base + 2, i_pgate=5)
    return pl.pallas_call(
        kern,
        grid=(N_TOK // TM,),
        in_specs=in_specs,
        out_specs=out_specs,
        out_shape=out_shape,
        scratch_shapes=[pltpu.VMEM((TM, D_FF), BF16)],
        compiler_params=_params("arbitrary"),
        name="ffn_proj" if proj else "ffn",
    )(*args)


def _diff_pre_kernel(*refs, rope, n_alias):
    refs = list(refs)
    x_ref, mod_ref, g_ref, w_ref, qg_ref, kg_ref = refs[:6]
    refs = refs[6:]
    if rope:
        cos_ref, up_ref, dn_ref = refs[:3]
        refs = refs[3:]
    q_ref, kt_ref, v_ref = refs[n_alias:]
    h = _modulate(x_ref[...], g_ref[...], mod_ref[3:4, :], mod_ref[4:5, :]).astype(BF16)
    hw = DIFF_HEADS * DIFF_VD
    v = _dot(h, w_ref[:, 2 * hw:3 * hw])
    if len(v_ref.shape) == 2:
        v_ref[...] = v
    else:
        for b in range(SEQS_PER_TILE):
            v_ref[b] = v[b * SEQ:(b + 1) * SEQ]
    lane = lax.broadcasted_iota(jnp.int32, (1, LANES), 1)
    lo = lane < DIFF_HD
    for part, gain_ref in enumerate((qg_ref, kg_ref)):
        gain = gain_ref[...]
        y = _dot(h, w_ref[:, part * hw:(part + 1) * hw])
        for hd in range(DIFF_HEADS):
            sl = slice(hd * LANES, (hd + 1) * LANES)
            yh = y[:, sl]
            sq = yh * yh
            s_lo = jnp.sum(jnp.where(lo, sq, 0.0), axis=-1, keepdims=True)
            s_hi = jnp.sum(jnp.where(lo, 0.0, sq), axis=-1, keepdims=True)
            ms = jnp.where(lo, s_lo, s_hi) * (1.0 / DIFF_HD)
            yn = yh * lax.rsqrt(ms + EPS) * gain
            if rope:
                yn = _rope(yn, cos_ref[...], up_ref[...], dn_ref[...], DIFF_HD // 4)
            if part == 0:
                q_ref[:, sl] = (yn * DIFF_Q_SCALE).astype(q_ref.dtype)
            elif len(kt_ref.shape) == 2:
                kt_ref[sl, :] = yn.T
            else:
                ynt = yn.T
                for b in range(SEQS_PER_TILE):
                    kt_ref[b, sl, :] = ynt[:, b * SEQ:(b + 1) * SEQ]


def _diff_pre(x, mod_l, g, w_qkv, qg, kg, tables, lat, j=0, stacks=None):
    n_rows = N_LAT if lat else N_CTX
    tile0 = N_CTX // TM if lat else 0
    per_lat = DEC_SEQ // TM
    full = lambda i: (0, 0)
    xrow = lambda i: (i + tile0, 0)
    row = lambda i: (i, 0)
    if lat:
        cond = lambda i: (1 + i // per_lat, 0, 0)
    else:
        cond = lambda i: (0, 0, 0)
    in_specs = [
        pl.BlockSpec((TM, D_MODEL), xrow),
        pl.BlockSpec((None, N_MOD, D_MODEL), cond),
        pl.BlockSpec((1, D_MODEL), full),
        pl.BlockSpec((D_MODEL, 3 * D_MODEL), full),
        pl.BlockSpec((1, LANES), full),
        pl.BlockSpec((1, LANES), full),
    ]
    args = [x, mod_l, g.reshape(1, D_MODEL), w_qkv, qg, kg]
    aliases = {}
    if lat:
        pos = lambda i: (i % per_lat, 0)
        in_specs += [pl.BlockSpec((TM, LANES), pos)] * 3
        args += list(tables)
        out_specs = [pl.BlockSpec((TM, D_MODEL), row),
                     pl.BlockSpec((None, D_MODEL, TM), lambda i: (i // per_lat, 0, i % per_lat)),
                     pl.BlockSpec((TM, D_MODEL), row)]
        out_shape = [jax.ShapeDtypeStruct((N_LAT, D_MODEL), BF16),
                     jax.ShapeDtypeStruct((DEC_BATCH, D_MODEL, DEC_SEQ), F32),
                     jax.ShapeDtypeStruct((N_LAT, D_MODEL), F32)]
    else:
        slot = lambda i: (i, j, 0, 0)
        out_specs = [pl.BlockSpec((TM, D_MODEL), row),
                     pl.BlockSpec((SEQS_PER_TILE, None, D_MODEL, SEQ), slot),
                     pl.BlockSpec((SEQS_PER_TILE, None, SEQ, D_MODEL), slot)]
        out_shape = [jax.ShapeDtypeStruct((N_CTX, D_MODEL), BF16),
                     jax.ShapeDtypeStruct((BATCH, N_DIFF, D_MODEL, SEQ), F32),
                     jax.ShapeDtypeStruct((BATCH, N_DIFF, SEQ, D_MODEL), F32)]
        if stacks is not None:
            aliases = {len(args): 1, len(args) + 1: 2}
            in_specs += [pl.BlockSpec(memory_space=pl.ANY)] * 2
            args += list(stacks)
    return pl.pallas_call(
        functools.partial(_diff_pre_kernel, rope=lat, n_alias=len(aliases)),
        grid=(n_rows // TM,),
        in_specs=in_specs,
        out_specs=out_specs,
        out_shape=out_shape,
        input_output_aliases=aliases,
        compiler_params=_params("parallel"),
        name="diff_pre_lat" if lat else "diff_pre_ctx",
    )(*args)


def _softmax_pv(s, v_aug):
    e = jnp.exp2(s - jnp.max(s, axis=-1, keepdims=True)).astype(BF16)
    r = _dot(e, v_aug)
    return r[:, :LANES] * (1.0 / r[:, LANES:])


def _diff_attn_kernel(*refs, heads, lam_init, cached):
    if cached:
        q_ref, ktc_ref, kt_ref, vc_ref, v_ref, lam_ref, sub_ref, o_ref = refs
    else:
        q_ref, kt_ref, v_ref, lam_ref, sub_ref, o_ref = refs
    lp = lam_ref[...]
    lam = (jnp.exp(jnp.sum(lp[0:1, :] * lp[1:2, :], axis=-1, keepdims=True))
           - jnp.exp(jnp.sum(lp[2:3, :] * lp[3:4, :], axis=-1, keepdims=True)) + lam_init)
    tq = q_ref.shape[0]
    lane = lax.broadcasted_iota(jnp.int32, (1, LANES), 1)
    lo = lane < DIFF_HD
    sub_g = sub_ref[...]
    for hd in range(heads):
        sl = slice(hd * LANES, (hd + 1) * LANES)
        qh = q_ref[:, sl]
        kt = kt_ref[sl, :].astype(BF16)
        vh = v_ref[:, sl].astype(BF16)
        if cached:
            kt = jnp.concatenate([ktc_ref[sl, :].astype(BF16), kt], axis=1)
            vh = jnp.concatenate([vc_ref[:, sl].astype(BF16), vh], axis=0)
        v_aug = jnp.concatenate([vh, jnp.ones_like(vh)], axis=1)
        zero = jnp.zeros_like(qh)
        qq = jnp.concatenate([jnp.where(lo, qh, zero), jnp.where(lo, zero, qh)], axis=0)
        o12 = _softmax_pv(_dot(qq, kt), v_aug)
        o = o12[:tq] - lam * o12[tq:]
        ms = jnp.mean(o * o, axis=-1, keepdims=True)
        o = (o * lax.rsqrt(ms + EPS) * sub_g) * (1.0 - lam_init)
        o_ref[:, sl] = o.astype(o_ref.dtype)


def _diff_attn(q, kt, v, lam_p, sub_g, lam_init, heads_per_step, j, caches=None):
    b, lq, _ = q.shape
    cached = caches is not None
    w = heads_per_step * LANES
    n_hg = DIFF_HEADS // heads_per_step
    tq = min(TQ, lq)
    qmap = lambda bi, hi, qi: (bi, qi, hi)
    full = lambda bi, hi, qi: (0, 0)
    q_spec = pl.BlockSpec((None, tq, w), qmap)
    kt_slot = lambda bi, hi, qi: (bi, j, hi, 0)
    v_slot = lambda bi, hi, qi: (bi, j, 0, hi)
    if cached:
        kt_cache, v_cache = caches
        in_specs = [q_spec,
                    pl.BlockSpec((None, None, w, PAST_LEN), kt_slot),
                    pl.BlockSpec((None, w, lq), lambda bi, hi, qi: (bi, hi, 0)),
                    pl.BlockSpec((None, None, PAST_LEN, w), v_slot),
                    pl.BlockSpec((None, lq, w), lambda bi, hi, qi: (bi, 0, hi))]
        args = [q, kt_cache, kt, v_cache, v]
    else:
        in_specs = [q_spec,
                    pl.BlockSpec((None, None, w, lq), kt_slot),
                    pl.BlockSpec((None, None, lq, w), v_slot)]
        args = [q, kt, v]
    in_specs += [pl.BlockSpec((4, DIFF_HD), full), pl.BlockSpec((1, LANES), full)]
    args += [lam_p, sub_g]
    return pl.pallas_call(
        functools.partial(_diff_attn_kernel, heads=heads_per_step, lam_init=lam_init, cached=cached),
        grid=(b, n_hg, lq // tq),
        in_specs=in_specs,
        out_specs=pl.BlockSpec((None, tq, w), qmap),
        out_shape=jax.ShapeDtypeStruct((b, lq, D_MODEL), BF16),
        compiler_params=_params("parallel", "parallel", "parallel"),
        name="diff_attn_lat" if cached else "diff_attn_ctx",
    )(*args)


def _mla_keys(ckv_n, kpe, wk_ref, wv_ref, kg, rope_refs, k_ref, v_ref):
    c = ckv_n.astype(BF16)
    v_ref[...] = _dot(c, wv_ref[...]).astype(v_ref.dtype)
    kk = _dot(c, wk_ref[...])
    kpe = pltpu.roll(kpe, MLA_NOPE, 1)
    for hd in range(MLA_HEADS):
        sl = slice(hd * LANES, (hd + 1) * LANES)
        kh = kk[:, sl] + kpe
        ms = jnp.sum(kh * kh, axis=-1, keepdims=True) * (1.0 / MLA_QK)
        kn = kh * lax.rsqrt(ms + EPS) * kg
        if rope_refs is not None:
            cos_ref, up_ref, dn_ref = rope_refs
            kn = _rope(kn, cos_ref[...], up_ref[...], dn_ref[...], MLA_ROPE // 4)
        k_ref[:, sl] = kn.astype(k_ref.dtype)


def _mla_pre_kernel(*refs, lat, n_alias):
    refs = list(refs)
    x_ref, mod_ref, g_ref, wdq_ref, wdkv_ref, qag_ref, kvag_ref, wq_ref, qg_ref = refs[:9]
    refs = refs[9:]
    if lat:
        cos_ref, up_ref, dn_ref = refs[:3]
        q_ref, ckv_ref, kpe_ref = refs[3:]
    else:
        wk_ref, wv_ref, kg_ref = refs[:3]
        q_ref, k_ref, v_ref, ckvs_ref, kpet_ref = refs[3 + n_alias:]
    h = _modulate(x_ref[...], g_ref[...], mod_ref[3:4, :], mod_ref[4:5, :]).astype(BF16)
    d2 = _dot(h, wdkv_ref[...])
    ckv = d2[:, :KV_RANK]
    ms = jnp.mean(ckv * ckv, axis=-1, keepdims=True)
    ckv_n = ckv * lax.rsqrt(ms + EPS) * kvag_ref[...]
    kpe = d2[:, KV_RANK:]
    if lat:
        ckv_ref[...] = ckv_n
        kpe_ref[...] = kpe
    else:
        kpe_t = kpe.T
        for b in range(SEQS_PER_TILE):
            ckvs_ref[b] = ckv_n[b * SEQ:(b + 1) * SEQ]
            kpet_ref[b] = kpe_t[:MLA_ROPE, b * SEQ:(b + 1) * SEQ]
        _mla_keys(ckv_n, kpe, wk_ref, wv_ref, kg_ref[...], None, k_ref, v_ref)
    cq = _dot(h, wdq_ref[...])
    ms = jnp.mean(cq * cq, axis=-1, keepdims=True)
    cqn = (cq * lax.rsqrt(ms + EPS) * qag_ref[...]).astype(BF16)
    q = _dot(cqn, wq_ref[...])
    qg = qg_ref[...]
    for hd in range(MLA_HEADS):
        sl = slice(hd * LANES, (hd + 1) * LANES)
        qh = q[:, sl]
        ms = jnp.sum(qh * qh, axis=-1, keepdims=True) * (1.0 / MLA_QK)
        qn = qh * lax.rsqrt(ms + EPS) * qg
        if lat:
            qn = _rope(qn, cos_ref[...], up_ref[...], dn_ref[...], MLA_ROPE // 4)
        q_ref[:, sl] = (qn * MLA_Q_SCALE).astype(q_ref.dtype)


def _mla_pre(x, mod_l, g, w_dq, w_dkv, qag, kvag, w_q, qg, lat, tables=None, kv_weights=None,
             j=0, stacks=None):
    n_rows = N_LAT if lat else N_CTX
    tile0 = N_CTX // TM if lat else 0
    per_lat = DEC_SEQ // TM
    full = lambda i: (0, 0)
    xrow = lambda i: (i + tile0, 0)
    row = lambda i: (i, 0)
    if lat:
        cond = lambda i: (1 + i // per_lat, 0, 0)
    else:
        cond = lambda i: (0, 0, 0)
    kvw = KV_RANK + LANES
    qw = MLA_HEADS * LANES
    vw = MLA_HEADS * MLA_VD
    in_specs = [
        pl.BlockSpec((TM, D_MODEL), xrow),
        pl.BlockSpec((None, N_MOD, D_MODEL), cond),
        pl.BlockSpec((1, D_MODEL), full),
        pl.BlockSpec((D_MODEL, Q_RANK), full),
        pl.BlockSpec((D_MODEL, kvw), full),
        pl.BlockSpec((1, Q_RANK), full),
        pl.BlockSpec((1, KV_RANK), full),
        pl.BlockSpec((Q_RANK, qw), full),
        pl.BlockSpec((1, LANES), full),
    ]
    args = [x, mod_l, g.reshape(1, D_MODEL), w_dq, w_dkv, qag.reshape(1, Q_RANK),
            kvag.reshape(1, KV_RANK), w_q, qg]
    aliases = {}
    if lat:
        pos = lambda i: (i % per_lat, 0)
        in_specs += [pl.BlockSpec((TM, LANES), pos)] * 3
        args += list(tables)
        out_specs = [pl.BlockSpec((TM, qw), row), pl.BlockSpec((TM, KV_RANK), row),
                     pl.BlockSpec((TM, LANES), row)]
        out_shape = [jax.ShapeDtypeStruct((n_rows, qw), BF16),
                     jax.ShapeDtypeStruct((n_rows, KV_RANK), F32),
                     jax.ShapeDtypeStruct((n_rows, LANES), F32)]
    else:
        w_k, w_v, kg = kv_weights
        in_specs += [pl.BlockSpec((KV_RANK, qw), full), pl.BlockSpec((KV_RANK, vw), full),
                     pl.BlockSpec((1, LANES), full)]
        args += [w_k, w_v, kg]
        slot = lambda i: (i, j, 0, 0)
        out_specs = [pl.BlockSpec((TM, qw), row), pl.BlockSpec((TM, qw), row),
                     pl.BlockSpec((TM, vw), row),
                     pl.BlockSpec((SEQS_PER_TILE, None, SEQ, KV_RANK), slot),
                     pl.BlockSpec((SEQS_PER_TILE, None, MLA_ROPE, SEQ), slot)]
        out_shape = [jax.ShapeDtypeStruct((n_rows, qw), BF16),
                     jax.ShapeDtypeStruct((n_rows, qw), BF16),
                     jax.ShapeDtypeStruct((n_rows, vw), BF16),
                     jax.ShapeDtypeStruct((BATCH, N_MLA, SEQ, KV_RANK), F32),
                     jax.ShapeDtypeStruct((BATCH, N_MLA, MLA_ROPE, SEQ), F32)]
        if stacks is not None:
            aliases = {len(args): 3, len(args) + 1: 4}
            in_specs += [pl.BlockSpec(memory_space=pl.ANY)] * 2
            args += list(stacks)
    return pl.pallas_call(
        functools.partial(_mla_pre_kernel, lat=lat, n_alias=len(aliases)),
        grid=(n_rows // TM,),
        in_specs=in_specs,
        out_specs=out_specs,
        out_shape=out_shape,
        input_output_aliases=aliases,
        compiler_params=_params("parallel"),
        name="mla_pre_lat" if lat else "mla_pre_ctx",
    )(*args)


def _mla_kv_kernel(ckv_ref, kpe_ref, wk_ref, wv_ref, kg_ref, cos_ref, up_ref, dn_ref, k_ref, v_ref):
    _mla_keys(ckv_ref[...], kpe_ref[...], wk_ref, wv_ref, kg_ref[...],
              (cos_ref, up_ref, dn_ref), k_ref, v_ref)


def _mla_kv_lat(ckv_rows, kpe_rows, w_k, w_v, kg, tables):
    n_rows = ckv_rows.shape[0]
    tm = KVP_TM_LAT
    per = LAT_KV // tm
    full = lambda i: (0, 0)
    row = lambda i: (i, 0)
    pos = lambda i: (i % per, 0)
    kw = MLA_HEADS * LANES
    vw = MLA_HEADS * MLA_VD
    return pl.pallas_call(
        _mla_kv_kernel,
        grid=(n_rows // tm,),
        in_specs=[
            pl.BlockSpec((tm, KV_RANK), row),
            pl.BlockSpec((tm, LANES), row),
            pl.BlockSpec((KV_RANK, kw), full),
            pl.BlockSpec((KV_RANK, vw), full),
            pl.BlockSpec((1, LANES), full),
        ] + [pl.BlockSpec((tm, LANES), pos)] * 3,
        out_specs=[pl.BlockSpec((tm, kw), row), pl.BlockSpec((tm, vw), row)],
        out_shape=[jax.ShapeDtypeStruct((n_rows, kw), BF16),
                   jax.ShapeDtypeStruct((n_rows, vw), BF16)],
        compiler_params=_params("parallel"),
        name="mla_kv_lat",
    )(ckv_rows, kpe_rows, w_k, w_v, kg, *tables)


def _mla_attn_kernel(q_ref, k_ref, v_ref, o_ref, *, pairs):
    lane = lax.broadcasted_iota(jnp.int32, (1, LANES), 1)
    lo = lane < MLA_VD
    for p in range(pairs):
        vp = v_ref[:, p * LANES:(p + 1) * LANES]
        v_aug = jnp.concatenate([vp, jnp.ones_like(vp)], axis=1)
        outs = []
        for sub in range(2):
            hd = 2 * p + sub
            sl = slice(hd * LANES, (hd + 1) * LANES)
            outs.append(_softmax_pv(_dot_nt(q_ref[:, sl], k_ref[:, sl]), v_aug))
        o_ref[:, p * LANES:(p + 1) * LANES] = jnp.where(lo, outs[0], outs[1]).astype(o_ref.dtype)


def _mla_attn(q, k, v, pairs_per_step):
    b, lq, _ = q.shape
    lk = k.shape[1]
    n_pg = MLA_HEADS // 2 // pairs_per_step
    tq = min(TQ, lq)
    qkw = pairs_per_step * 2 * LANES
    ow = pairs_per_step * LANES
    qmap = lambda bi, hi, qi: (bi, qi, hi)
    kmap = lambda bi, hi, qi: (bi, 0, hi)
    return pl.pallas_call(
        functools.partial(_mla_attn_kernel, pairs=pairs_per_step),
        grid=(b, n_pg, lq // tq),
        in_specs=[
            pl.BlockSpec((None, tq, qkw), qmap),
            pl.BlockSpec((None, lk, qkw), kmap),
            pl.BlockSpec((None, lk, ow), kmap),
        ],
        out_specs=pl.BlockSpec((None, tq, ow), qmap),
        out_shape=jax.ShapeDtypeStruct((b, lq, D_MODEL), BF16),
        compiler_params=_params("parallel", "parallel", "parallel"),
        name="mla_attn_lat" if lq == DEC_SEQ else "mla_attn_ctx",
    )(q, k, v)


def _rope_tables(n_rot, lane0, identity_rows):
    rows = DEC_SEQ // GRID_W
    row = np.repeat(np.arange(rows), GRID_W)
    col = np.tile(np.arange(GRID_W), rows)
    n = n_rot // 2
    freqs = ROPE_THETA ** (-np.arange(0, n, 2, dtype=np.float64) / n)
    zeros = np.zeros((DEC_SEQ, n // 2))
    cos_parts, up_parts, dn_parts = [], [], []
    for pos in (row, col):
        ang = pos.astype(np.float64)[:, None] * freqs[None, :]
        c, s = np.cos(ang), np.sin(ang)
        cos_parts += [c, c]
        up_parts += [-s, zeros]
        dn_parts += [zeros, s]

    def place(parts, fill):
        reps = 2 if lane0 == 0 else 1
        body = np.concatenate(parts * reps, axis=1)
        full = np.full((identity_rows + DEC_SEQ, LANES), fill)
        full[identity_rows:, lane0:lane0 + body.shape[1]] = body
        return jnp.asarray(full, dtype=F32)

    return place(cos_parts, 1.0), place(up_parts, 0.0), place(dn_parts, 0.0)


def kernel(x_prompt, x_sample, c, cache_diff_k, cache_diff_v, cache_mla_ckv, cache_mla_kpe, c_ctx, w_mod, b_mod, norm_g, ffn_w_in, ffn_w_out, diff_w_qkv, diff_q_norm, diff_k_norm, diff_lambda, diff_subln, diff_w_o, mla_w_down, mla_q_a_norm, mla_kv_a_norm, mla_w_q_up, mla_w_kv_up, mla_q_norm, mla_k_norm, mla_w_o):
    xs = (x_prompt.reshape(N_CTX, D_MODEL), x_sample.reshape(N_LAT, D_MODEL))
    conds = jnp.concatenate(
        [c_ctx[None, :], c, jnp.zeros((COND_ROWS - 1 - DEC_BATCH, D_MODEL), F32)], axis=0)
    mod = _modulation(conds, w_mod, b_mod)

    w_in = ffn_w_in[0, 0].astype(BF16)
    w_out = ffn_w_out[0, 0].astype(BF16)
    diff_tabs = _rope_tables(DIFF_HD, 0, 0)
    mla_tabs = _rope_tables(MLA_ROPE, MLA_NOPE, 0)
    mla_kv_tabs = _rope_tables(MLA_ROPE, MLA_NOPE, PAST_LEN)
    cache_kt = jnp.transpose(cache_diff_k, (0, 1, 3, 4, 5, 2)).reshape(
        DEC_BATCH, N_DIFF, D_MODEL, PAST_LEN)
    cache_v = cache_diff_v.reshape(DEC_BATCH, N_DIFF, PAST_LEN, D_MODEL)

    diff_stacks, mla_stacks = None, None
    for l in range(DEPTH):
        j = l // 2
        mod_l = mod[l]
        x, w_in, w_out = _ffn(xs, mod_l, norm_g[l, 0], w_in, w_out, 0,
                              nxt=(ffn_w_in, ffn_w_out, l, 1))
        g1 = norm_g[l, 1]
        if l % 2 == 0:
            w_qkv = diff_w_qkv[j].astype(BF16)
            qg = jnp.tile(diff_q_norm[j], 2).reshape(1, LANES)
            kg = jnp.tile(diff_k_norm[j], 2).reshape(1, LANES)
            sub_g = diff_subln[j].reshape(1, LANES)
            lam_init = 0.8 - 0.6 * math.exp(-0.3 * l)
            q_c, kt_stack, v_stack = _diff_pre(x, mod_l, g1, w_qkv, qg, kg, None, lat=False,
                                               j=j, stacks=diff_stacks)
            diff_stacks = (kt_stack, v_stack)
            q_l, kt_l, v_l = _diff_pre(x, mod_l, g1, w_qkv, qg, kg, diff_tabs, lat=True)
            o_c = _diff_attn(q_c.reshape(BATCH, SEQ, D_MODEL), kt_stack, v_stack, diff_lambda[j],
                             sub_g, lam_init, heads_per_step=DIFF_HEADS, j=j)
            o_l = _diff_attn(q_l.reshape(DEC_BATCH, DEC_SEQ, D_MODEL), kt_l,
                             v_l.reshape(DEC_BATCH, DEC_SEQ, D_MODEL), diff_lambda[j], sub_g,
                             lam_init, heads_per_step=LAT_HEADS_PER_STEP, j=j,
                             caches=(cache_kt, cache_v))
            w_o = diff_w_o[j].astype(BF16)
        else:
            wd = mla_w_down[j]
            w_dq = wd[:, :Q_RANK].astype(BF16)
            w_dkv = jnp.pad(wd[:, Q_RANK:], ((0, 0), (0, LANES - MLA_ROPE))).astype(BF16)
            w_q = jnp.pad(mla_w_q_up[j].reshape(Q_RANK, MLA_HEADS, MLA_QK),
                          ((0, 0), (0, 0), (0, LANES - MLA_QK))).reshape(Q_RANK, MLA_HEADS * LANES).astype(BF16)
            wkv = mla_w_kv_up[j].reshape(KV_RANK, MLA_HEADS, MLA_NOPE + MLA_VD)
            w_k = jnp.pad(wkv[:, :, :MLA_NOPE], ((0, 0), (0, 0), (0, LANES - MLA_NOPE))
                          ).reshape(KV_RANK, MLA_HEADS * LANES).astype(BF16)
            w_v = wkv[:, :, MLA_NOPE:].reshape(KV_RANK, MLA_HEADS * MLA_VD).astype(BF16)
            qg = jnp.pad(mla_q_norm[j], (0, LANES - MLA_QK)).reshape(1, LANES)
            kg = jnp.pad(mla_k_norm[j], (0, LANES - MLA_QK)).reshape(1, LANES)
            pre = functools.partial(_mla_pre, x, mod_l, g1, w_dq, w_dkv, mla_q_a_norm[j],
                                    mla_kv_a_norm[j], w_q, qg)
            q_c, k_c, v_c, ckv_stack, kpet_stack = pre(lat=False, kv_weights=(w_k, w_v, kg), j=j,
                                                      stacks=mla_stacks)
            mla_stacks = (ckv_stack, kpet_stack)
            q_l, ckv_l, kpe_l = pre(lat=True, tables=mla_tabs)
            cache_kpe = jnp.pad(cache_mla_kpe[:, j], ((0, 0), (0, 0), (0, LANES - MLA_ROPE)))
            ckv_rows = jnp.concatenate([cache_mla_ckv[:, j], ckv_l.reshape(DEC_BATCH, DEC_SEQ, KV_RANK)],
                                       axis=1).reshape(DEC_BATCH * LAT_KV, KV_RANK)
            kpe_rows = jnp.concatenate([cache_kpe, kpe_l.reshape(DEC_BATCH, DEC_SEQ, LANES)],
                                       axis=1).reshape(DEC_BATCH * LAT_KV, LANES)
            k_a, v_a = _mla_kv_lat(ckv_rows, kpe_rows, w_k, w_v, kg, mla_kv_tabs)
            kw = MLA_HEADS * LANES
            o_c = _mla_attn(q_c.reshape(BATCH, SEQ, kw), k_c.reshape(BATCH, SEQ, kw),
                            v_c.reshape(BATCH, SEQ, D_MODEL), pairs_per_step=MLA_HEADS // 2)
            o_l = _mla_attn(q_l.reshape(DEC_BATCH, DEC_SEQ, kw), k_a.reshape(DEC_BATCH, LAT_KV, kw),
                            v_a.reshape(DEC_BATCH, LAT_KV, D_MODEL),
                            pairs_per_step=LAT_HEADS_PER_STEP)
            w_o = mla_w_o[j].astype(BF16)
        os = (o_c.reshape(N_CTX, D_MODEL), o_l.reshape(N_LAT, D_MODEL))
        last = l == DEPTH - 1
        xs = _ffn((x,), mod_l, norm_g[l, 2], w_in, w_out, 1, os=os, w_o=w_o, split_out=last,
                  nxt=None if last else (ffn_w_in, ffn_w_out, l + 1, 0))
        if not last:
            xs, (w_in, w_out) = xs[:1], xs[1:]

    kt_stack, v_stack = diff_stacks
    ckv_stack, kpet_stack = mla_stacks
    new_diff_k = jnp.transpose(kt_stack.reshape(BATCH, N_DIFF, DIFF_HEADS, 2, DIFF_HD, SEQ),
                               (0, 1, 5, 2, 3, 4))
    new_diff_v = v_stack.reshape(BATCH, N_DIFF, SEQ, DIFF_HEADS, DIFF_VD)
    new_mla_kpe = jnp.transpose(kpet_stack, (0, 1, 3, 2))
    return (xs[0].reshape(BATCH, SEQ, D_MODEL), xs[1].reshape(DEC_BATCH, DEC_SEQ, D_MODEL),
            new_diff_k, new_diff_v, ckv_stack, new_mla_kpe)
```

```python
import functools
import math

import jax
import jax.numpy as jnp
import numpy as np
from jax import lax
from jax.experimental import pallas as pl
from jax.experimental.pallas import tpu as pltpu

D_MODEL = 1024
BATCH = 32
SEQ = 256
DEPTH = 4
DEC_BATCH = 2
DEC_SEQ = 1024
PAST_LEN = 256
GRID_W = 64
N_DIFF = (DEPTH + 1) // 2
N_MLA = DEPTH // 2
N_MOD = 9
D_FF = 2816
DIFF_HEADS = 8
DIFF_HD = 64
DIFF_VD = 128
MLA_HEADS = 16
MLA_NOPE = 64
MLA_ROPE = 32
MLA_QK = MLA_NOPE + MLA_ROPE
MLA_VD = 64
Q_RANK = 768
KV_RANK = 256
ROPE_THETA = 10000.0
EPS = 1e-6
LOG2E = math.log2(math.e)
DIFF_Q_SCALE = DIFF_HD ** -0.5 * LOG2E
MLA_Q_SCALE = MLA_QK ** -0.5 * LOG2E

N_CTX = BATCH * SEQ
N_LAT = DEC_BATCH * DEC_SEQ
N_TOK = N_CTX + N_LAT
LAT_KV = PAST_LEN + DEC_SEQ

LANES = 128
COND_ROWS = 8
TM = 512
SEQS_PER_TILE = TM // SEQ
FF_CHUNK = 256
TQ = 256
LAT_HEADS_PER_STEP = 4
KVP_TM_LAT = 640
MOD_TN = 2304
VMEM_LIMIT = 56 * 1024 * 1024

F32 = jnp.float32
BF16 = jnp.bfloat16


def _params(*sem):
    return pltpu.CompilerParams(dimension_semantics=sem, vmem_limit_bytes=VMEM_LIMIT)


def _dot(a, b):
    return jnp.dot(a, b, preferred_element_type=F32)


def _dot_nt(a, b):
    return lax.dot_general(a, b, (((1,), (1,)), ((), ())), preferred_element_type=F32)


def _modulate(x, g, shift, scale):
    ms = jnp.mean(x * x, axis=-1, keepdims=True)
    return (x * lax.rsqrt(ms + EPS) * g) * (1.0 + scale) + shift


def _rope(x, cos, s_up, s_dn, shift):
    return (x * cos + pltpu.roll(x, LANES - shift, 1) * s_up
            + pltpu.roll(x, shift, 1) * s_dn)


def _cond_of_tile(i, first_lat_tile, tiles_per_lat_batch):
    lat = jnp.maximum(i - first_lat_tile, 0) // tiles_per_lat_batch
    return jnp.where(i < first_lat_tile, 0, 1 + lat)


def _mod_kernel(c_ref, w_ref, b_ref, o_ref):
    c = c_ref[...]
    s = (c * jax.nn.sigmoid(c)).astype(BF16)
    o_ref[...] = _dot(s, w_ref[...].astype(BF16)) + b_ref[...]


def _modulation(conds, w_mod, b_mod):
    n_out = N_MOD * D_MODEL
    out = pl.pallas_call(
        _mod_kernel,
        grid=(DEPTH, n_out // MOD_TN),
        in_specs=[
            pl.BlockSpec((COND_ROWS, D_MODEL), lambda l, n: (0, 0)),
            pl.BlockSpec((None, D_MODEL, MOD_TN), lambda l, n: (l, 0, n)),
            pl.BlockSpec((None, 1, MOD_TN), lambda l, n: (l, 0, n)),
        ],
        out_specs=pl.BlockSpec((None, COND_ROWS, MOD_TN), lambda l, n: (l, 0, n)),
        out_shape=jax.ShapeDtypeStruct((DEPTH, COND_ROWS, n_out), F32),
        compiler_params=_params("parallel", "parallel"),
        name="modulation",
    )(conds, w_mod, b_mod.reshape(DEPTH, 1, n_out))
    return out.reshape(DEPTH, COND_ROWS, N_MOD, D_MODEL)


FIRST_LAT_TILE = N_CTX // TM
TILES_PER_LAT_BATCH = DEC_SEQ // TM


def _ctx_tile(i):
    return (jnp.minimum(i, FIRST_LAT_TILE - 1), 0)


def _lat_tile(i):
    return (jnp.maximum(i - FIRST_LAT_TILE, 0), 0)


def _pick_rows(tile, ctx_ref, lat_ref):
    rows = tile * TM + lax.broadcasted_iota(jnp.int32, (TM, 1), 0)
    return jnp.where(rows < N_CTX, ctx_ref[...], lat_ref[...])


def _ffn_kernel(*refs, n_x, proj, n_out, convert, i_shift, i_scale, i_gate, i_pgate):
    refs = list(refs)
    x0_ref = refs.pop(0)
    xn_refs = [refs.pop(0) for _ in range(n_x)]
    if proj:
        o0_ref, onc_ref, onl_ref, wo_ref = refs[:4]
        refs = refs[4:]
    mod_ref, modn_ref, g_ref, win_ref, wout_ref = refs[:5]
    refs = refs[5:]
    if convert:
        nin_ref, nout_ref = refs[:2]
        refs = refs[2:]
    out_refs = refs[:n_out]
    refs = refs[n_out:]
    if convert:
        nin_bf_ref, nout_bf_ref = refs[:2]
        refs = refs[2:]
    a_ref, a0_buf, h_buf, xp_buf = refs
    i = pl.program_id(0)

    def hidden_chunk(slot, c):
        lo = c * FF_CHUNK
        g = _dot(h_buf[slot], win_ref[:, lo:lo + FF_CHUNK])
        u = _dot(h_buf[slot], win_ref[:, D_FF + lo:D_FF + lo + FF_CHUNK])
        return ((g * jax.nn.sigmoid(g)) * u).astype(BF16)

    def prepare(x, o, m_ref, dst):
        if proj:
            x = x + m_ref[i_pgate:i_pgate + 1, :] * _dot(o, wo_ref[...])
        xp_buf[dst] = x
        h_buf[dst] = _modulate(x, g_ref[...], m_ref[i_shift:i_shift + 1, :],
                               m_ref[i_scale:i_scale + 1, :]).astype(BF16)
        a0_buf[dst] = hidden_chunk(dst, 0)

    @pl.when(i == 0)
    def _():
        prepare(x0_ref[...], o0_ref[...] if proj else None, mod_ref, 0)

    def step(cur, nxt):
        a_ref[:, :FF_CHUNK] = a0_buf[cur]
        for c in range(1, D_FF // FF_CHUNK):
            a_ref[:, c * FF_CHUNK:(c + 1) * FF_CHUNK] = hidden_chunk(cur, c)
        ff = _dot(a_ref[...], wout_ref[...])
        y = xp_buf[cur] + mod_ref[i_gate:i_gate + 1, :] * (0.5 * ff)
        xn = xn_refs[0][...] if n_x == 1 else _pick_rows(i + 1, *xn_refs)
        on = _pick_rows(i + 1, onc_ref, onl_ref) if proj else None
        prepare(xn, on, modn_ref, nxt)
        if convert:
            nin_bf_ref[...] = nin_ref[...].astype(BF16)
            nout_bf_ref[...] = nout_ref[...].astype(BF16)
        if n_out == 1:
            out_refs[0][...] = y
        else:
            @pl.when(i < FIRST_LAT_TILE)
            def _():
                out_refs[0][...] = y

            @pl.when(i >= FIRST_LAT_TILE)
            def _():
                out_refs[1][...] = y

    @pl.when(lax.rem(i, 2) == 0)
    def _():
        step(0, 1)

    @pl.when(lax.rem(i, 2) == 1)
    def _():
        step(1, 0)


def _ffn(xs, mod_l, g, w_in, w_out, which, os=None, w_o=None, split_out=False, nxt=None):
    proj = os is not None
    convert = nxt is not None
    base = 0 if which == 0 else 6
    n_steps = N_TOK // TM
    row = lambda i: (i, 0)
    full = lambda i: (0, 0)
    tile = (TM, D_MODEL)
    once = dict(pipeline_mode=pl.Buffered(1))
    nxt_row = lambda i: (jnp.minimum(i + 1, n_steps - 1), 0)
    nxt_ctx = lambda i: _ctx_tile(i + 1)
    nxt_lat = lambda i: (jnp.clip(i + 1 - FIRST_LAT_TILE, 0, N_LAT // TM - 1), 0)
    cond = lambda t: (_cond_of_tile(t, FIRST_LAT_TILE, TILES_PER_LAT_BATCH), 0, 0)
    in_specs = [pl.BlockSpec(tile, full, **once)]
    if len(xs) == 1:
        in_specs += [pl.BlockSpec(tile, nxt_row)]
    else:
        in_specs += [pl.BlockSpec(tile, nxt_ctx), pl.BlockSpec(tile, nxt_lat)]
    args = [xs[0]] + list(xs)
    if proj:
        in_specs += [pl.BlockSpec(tile, full, **once), pl.BlockSpec(tile, nxt_ctx),
                     pl.BlockSpec(tile, nxt_lat), pl.BlockSpec((D_MODEL, D_MODEL), full)]
        args += [os[0], os[0], os[1], w_o]
    in_specs += [
        pl.BlockSpec((None, N_MOD, D_MODEL), lambda i: cond(i)),
        pl.BlockSpec((None, N_MOD, D_MODEL), lambda i: cond(jnp.minimum(i + 1, n_steps - 1))),
        pl.BlockSpec((1, D_MODEL), full),
        pl.BlockSpec((D_MODEL, 2 * D_FF), full),
        pl.BlockSpec((D_FF, D_MODEL), full),
    ]
    args += [mod_l, mod_l, g.reshape(1, D_MODEL), w_in, w_out]
    in_slab = (D_MODEL // FIRST_LAT_TILE, 2 * D_FF)
    out_slab = (D_FF // FIRST_LAT_TILE, D_MODEL)
    if convert:
        nw_in, nw_out, nl, nw = nxt
        slab = lambda i: (nl, nw, jnp.minimum(i, FIRST_LAT_TILE - 1), 0)
        in_specs += [pl.BlockSpec((None, None) + in_slab, slab),
                     pl.BlockSpec((None, None) + out_slab, slab)]
        args += [nw_in, nw_out]
    if split_out:
        out_specs = [pl.BlockSpec(tile, _ctx_tile), pl.BlockSpec(tile, _lat_tile)]
        out_shape = [jax.ShapeDtypeStruct((N_CTX, D_MODEL), F32),
                     jax.ShapeDtypeStruct((N_LAT, D_MODEL), F32)]
    else:
        out_specs = [pl.BlockSpec(tile, row)]
        out_shape = [jax.ShapeDtypeStruct((N_TOK, D_MODEL), F32)]
    n_out = len(out_shape)
    if convert:
        out_specs += [pl.BlockSpec(in_slab, _ctx_tile), pl.BlockSpec(out_slab, _ctx_tile)]
        out_shape += [jax.ShapeDtypeStruct((D_MODEL, 2 * D_FF), BF16),
                      jax.ShapeDtypeStruct((D_FF, D_MODEL), BF16)]
    kern = functools.partial(_ffn_kernel, n_x=len(xs), proj=proj, n_out=n_out, convert=convert,
                             i_shift=base, i_scale=base + 1, i_gate=base + 2, i_pgate=5)
    return pl.pallas_call(
        kern,
        grid=(n_steps,),
        in_specs=in_specs,
        out_specs=out_specs,
        out_shape=out_shape,
        scratch_shapes=[pltpu.VMEM((TM, D_FF), BF16),
                        pltpu.VMEM((2, TM, FF_CHUNK), BF16),
                        pltpu.VMEM((2, TM, D_MODEL), BF16),
                        pltpu.VMEM((2, TM, D_MODEL), F32)],
        compiler_params=_params("arbitrary"),
        name="ffn_proj" if proj else "ffn",
    )(*args)


def _diff_pre_kernel(*refs, rope, n_alias):
    refs = list(refs)
    x_ref, mod_ref, g_ref, w_ref, qg_ref, kg_ref = refs[:6]
    refs = refs[6:]
    if rope:
        cos_ref, up_ref, dn_ref = refs[:3]
        refs = refs[3:]
    q_ref, kt_ref, v_ref = refs[n_alias:]
    h = _modulate(x_ref[...], g_ref[...], mod_ref[3:4, :], mod_ref[4:5, :]).astype(BF16)
    hw = DIFF_HEADS * DIFF_VD
    v = _dot(h, w_ref[:, 2 * hw:3 * hw])
    if len(v_ref.shape) == 2:
        v_ref[...] = v
    else:
        for b in range(SEQS_PER_TILE):
            v_ref[b] = v[b * SEQ:(b + 1) * SEQ]
    lane = lax.broadcasted_iota(jnp.int32, (1, LANES), 1)
    lo = lane < DIFF_HD
    for part, gain_ref in enumerate((qg_ref, kg_ref)):
        gain = gain_ref[...]
        y = _dot(h, w_ref[:, part * hw:(part + 1) * hw])
        for hd in range(DIFF_HEADS):
            sl = slice(hd * LANES, (hd + 1) * LANES)
            yh = y[:, sl]
            sq = yh * yh
            s_lo = jnp.sum(jnp.where(lo, sq, 0.0), axis=-1, keepdims=True)
            s_hi = jnp.sum(jnp.where(lo, 0.0, sq), axis=-1, keepdims=True)
            ms = jnp.where(lo, s_lo, s_hi) * (1.0 / DIFF_HD)
            yn = yh * lax.rsqrt(ms + EPS) * gain
            if rope:
                yn = _rope(yn, cos_ref[...], up_ref[...], dn_ref[...], DIFF_HD // 4)
            if part == 0:
                q_ref[:, sl] = (yn * DIFF_Q_SCALE).astype(q_ref.dtype)
            elif len(kt_ref.shape) == 2:
                kt_ref[sl, :] = yn.T
            else:
                ynt = yn.T
                for b in range(SEQS_PER_TILE):
                    kt_ref[b, sl, :] = ynt[:, b * SEQ:(b + 1) * SEQ]


def _diff_pre(x, mod_l, g, w_qkv, qg, kg, tables, lat, j=0, stacks=None):
    n_rows = N_LAT if lat else N_CTX
    tile0 = N_CTX // TM if lat else 0
    per_lat = DEC_SEQ // TM
    full = lambda i: (0, 0)
    xrow = lambda i: (i + tile0, 0)
    row = lambda i: (i, 0)
    if lat:
        cond = lambda i: (1 + i // per_lat, 0, 0)
    else:
        cond = lambda i: (0, 0, 0)
    in_specs = [
        pl.BlockSpec((TM, D_MODEL), xrow),
        pl.BlockSpec((None, N_MOD, D_MODEL), cond),
        pl.BlockSpec((1, D_MODEL), full),
        pl.BlockSpec((D_MODEL, 3 * D_MODEL), full),
        pl.BlockSpec((1, LANES), full),
        pl.BlockSpec((1, LANES), full),
    ]
    args = [x, mod_l, g.reshape(1, D_MODEL), w_qkv, qg, kg]
    aliases = {}
    if lat:
        pos = lambda i: (i % per_lat, 0)
        in_specs += [pl.BlockSpec((TM, LANES), pos)] * 3
        args += list(tables)
        out_specs = [pl.BlockSpec((TM, D_MODEL), row),
                     pl.BlockSpec((None, D_MODEL, TM), lambda i: (i // per_lat, 0, i % per_lat)),
                     pl.BlockSpec((TM, D_MODEL), row)]
        out_shape = [jax.ShapeDtypeStruct((N_LAT, D_MODEL), BF16),
                     jax.ShapeDtypeStruct((DEC_BATCH, D_MODEL, DEC_SEQ), F32),
                     jax.ShapeDtypeStruct((N_LAT, D_MODEL), F32)]
    else:
        slot = lambda i: (i, j, 0, 0)
        out_specs = [pl.BlockSpec((TM, D_MODEL), row),
                     pl.BlockSpec((SEQS_PER_TILE, None, D_MODEL, SEQ), slot),
                     pl.BlockSpec((SEQS_PER_TILE, None, SEQ, D_MODEL), slot)]
        out_shape = [jax.ShapeDtypeStruct((N_CTX, D_MODEL), BF16),
                     jax.ShapeDtypeStruct((BATCH, N_DIFF, D_MODEL, SEQ), F32),
                     jax.ShapeDtypeStruct((BATCH, N_DIFF, SEQ, D_MODEL), F32)]
        if stacks is not None:
            aliases = {len(args): 1, len(args) + 1: 2}
            in_specs += [pl.BlockSpec(memory_space=pl.ANY)] * 2
            args += list(stacks)
    return pl.pallas_call(
        functools.partial(_diff_pre_kernel, rope=lat, n_alias=len(aliases)),
        grid=(n_rows // TM,),
        in_specs=in_specs,
        out_specs=out_specs,
        out_shape=out_shape,
        input_output_aliases=aliases,
        compiler_params=_params("parallel"),
        name="diff_pre_lat" if lat else "diff_pre_ctx",
    )(*args)


def _softmax_pv(s, v_aug):
    e = jnp.exp2(s - jnp.max(s, axis=-1, keepdims=True)).astype(BF16)
    r = _dot(e, v_aug)
    return r[:, :LANES] * (1.0 / r[:, LANES:])


def _diff_attn_kernel(*refs, heads, lam_init, cached):
    if cached:
        q_ref, ktc_ref, kt_ref, vc_ref, v_ref, lam_ref, sub_ref, o_ref = refs
    else:
        q_ref, kt_ref, v_ref, lam_ref, sub_ref, o_ref = refs
    lp = lam_ref[...]
    lam = (jnp.exp(jnp.sum(lp[0:1, :] * lp[1:2, :], axis=-1, keepdims=True))
           - jnp.exp(jnp.sum(lp[2:3, :] * lp[3:4, :], axis=-1, keepdims=True)) + lam_init)
    tq = q_ref.shape[0]
    lane = lax.broadcasted_iota(jnp.int32, (1, LANES), 1)
    lo = lane < DIFF_HD
    sub_g = sub_ref[...]
    for hd in range(heads):
        sl = slice(hd * LANES, (hd + 1) * LANES)
        qh = q_ref[:, sl]
        kt = kt_ref[sl, :].astype(BF16)
        vh = v_ref[:, sl].astype(BF16)
        if cached:
            kt = jnp.concatenate([ktc_ref[sl, :].astype(BF16), kt], axis=1)
            vh = jnp.concatenate([vc_ref[:, sl].astype(BF16), vh], axis=0)
        v_aug = jnp.concatenate([vh, jnp.ones_like(vh)], axis=1)
        zero = jnp.zeros_like(qh)
        qq = jnp.concatenate([jnp.where(lo, qh, zero), jnp.where(lo, zero, qh)], axis=0)
        o12 = _softmax_pv(_dot(qq, kt), v_aug)
        o = o12[:tq] - lam * o12[tq:]
        ms = jnp.mean(o * o, axis=-1, keepdims=True)
        o = (o * lax.rsqrt(ms + EPS) * sub_g) * (1.0 - lam_init)
        o_ref[:, sl] = o.astype(o_ref.dtype)


def _diff_attn(q, kt, v, lam_p, sub_g, lam_init, heads_per_step, j, caches=None):
    b, lq, _ = q.shape
    cached = caches is not None
    w = heads_per_step * LANES
    n_hg = DIFF_HEADS // heads_per_step
    tq = min(TQ, lq)
    qmap = lambda bi, hi, qi: (bi, qi, hi)
    full = lambda bi, hi, qi: (0, 0)
    q_spec = pl.BlockSpec((None, tq, w), qmap)
    kt_slot = lambda bi, hi, qi: (bi, j, hi, 0)
    v_slot = lambda bi, hi, qi: (bi, j, 0, hi)
    if cached:
        kt_cache, v_cache = caches
        in_specs = [q_spec,
                    pl.BlockSpec((None, None, w, PAST_LEN), kt_slot),
                    pl.BlockSpec((None, w, lq), lambda bi, hi, qi: (bi, hi, 0)),
                    pl.BlockSpec((None, None, PAST_LEN, w), v_slot),
                    pl.BlockSpec((None, lq, w), lambda bi, hi, qi: (bi, 0, hi))]
        args = [q, kt_cache, kt, v_cache, v]
    else:
        in_specs = [q_spec,
                    pl.BlockSpec((None, None, w, lq), kt_slot),
                    pl.BlockSpec((None, None, lq, w), v_slot)]
        args = [q, kt, v]
    in_specs += [pl.BlockSpec((4, DIFF_HD), full), pl.BlockSpec((1, LANES), full)]
    args += [lam_p, sub_g]
    return pl.pallas_call(
        functools.partial(_diff_attn_kernel, heads=heads_per_step, lam_init=lam_init, cached=cached),
        grid=(b, n_hg, lq // tq),
        in_specs=in_specs,
        out_specs=pl.BlockSpec((None, tq, w), qmap),
        out_shape=jax.ShapeDtypeStruct((b, lq, D_MODEL), BF16),
        compiler_params=_params("parallel", "parallel", "parallel"),
        name="diff_attn_lat" if cached else "diff_attn_ctx",
    )(*args)


def _mla_keys(ckv_n, kpe, wk_ref, wv_ref, kg, rope_refs, k_ref, v_ref):
    c = ckv_n.astype(BF16)
    v_ref[...] = _dot(c, wv_ref[...]).astype(v_ref.dtype)
    kk = _dot(c, wk_ref[...])
    kpe = pltpu.roll(kpe, MLA_NOPE, 1)
    pe_ss = jnp.sum(kpe * kpe, axis=-1, keepdims=True)
    kpe_g = kpe * kg
    if rope_refs is not None:
        cos_ref, up_ref, dn_ref = rope_refs
        kpe_g = _rope(kpe_g, cos_ref[...], up_ref[...], dn_ref[...], MLA_ROPE // 4)
    for hd in range(MLA_HEADS):
        sl = slice(hd * LANES, (hd + 1) * LANES)
        kh = kk[:, sl]
        ms = (jnp.sum(kh * kh, axis=-1, keepdims=True) + pe_ss) * (1.0 / MLA_QK)
        kn = (kh * kg + kpe_g) * lax.rsqrt(ms + EPS)
        k_ref[:, sl] = kn.astype(k_ref.dtype)


def _mla_pre_kernel(*refs, lat, n_alias):
    refs = list(refs)
    x_ref, mod_ref, g_ref, wdq_ref, wdkv_ref, qag_ref, kvag_ref, wq_ref, qg_ref = refs[:9]
    refs = refs[9:]
    if lat:
        cos_ref, up_ref, dn_ref = refs[:3]
        q_ref, ckv_ref, kpe_ref = refs[3:]
    else:
        wk_ref, wv_ref, kg_ref = refs[:3]
        q_ref, k_ref, v_ref, ckvs_ref, kpet_ref = refs[3 + n_alias:]
    h = _modulate(x_ref[...], g_ref[...], mod_ref[3:4, :], mod_ref[4:5, :]).astype(BF16)
    d2 = _dot(h, wdkv_ref[...])
    ckv = d2[:, :KV_RANK]
    ms = jnp.mean(ckv * ckv, axis=-1, keepdims=True)
    ckv_n = ckv * lax.rsqrt(ms + EPS) * kvag_ref[...]
    kpe = d2[:, KV_RANK:]
    if lat:
        ckv_ref[...] = ckv_n
        kpe_ref[...] = kpe
    else:
        kpe_t = kpe.T
        for b in range(SEQS_PER_TILE):
            ckvs_ref[b] = ckv_n[b * SEQ:(b + 1) * SEQ]
            kpet_ref[b] = kpe_t[:MLA_ROPE, b * SEQ:(b + 1) * SEQ]
        _mla_keys(ckv_n, kpe, wk_ref, wv_ref, kg_ref[...], None, k_ref, v_ref)
    cq = _dot(h, wdq_ref[...])
    ms = jnp.mean(cq * cq, axis=-1, keepdims=True)
    cqn = (cq * lax.rsqrt(ms + EPS) * qag_ref[...]).astype(BF16)
    q = _dot(cqn, wq_ref[...])
    qg = qg_ref[...]
    for hd in range(MLA_HEADS):
        sl = slice(hd * LANES, (hd + 1) * LANES)
        qh = q[:, sl]
        ms = jnp.sum(qh * qh, axis=-1, keepdims=True) * (1.0 / MLA_QK)
        qn = qh * lax.rsqrt(ms + EPS) * qg
        if lat:
            qn = _rope(qn, cos_ref[...], up_ref[...], dn_ref[...], MLA_ROPE // 4)
        q_ref[:, sl] = (qn * MLA_Q_SCALE).astype(q_ref.dtype)


def _mla_pre(x, mod_l, g, w_dq, w_dkv, qag, kvag, w_q, qg, lat, tables=None, kv_weights=None,
             j=0, stacks=None):
    n_rows = N_LAT if lat else N_CTX
    tile0 = N_CTX // TM if lat else 0
    per_lat = DEC_SEQ // TM
    full = lambda i: (0, 0)
    xrow = lambda i: (i + tile0, 0)
    row = lambda i: (i, 0)
    if lat:
        cond = lambda i: (1 + i // per_lat, 0, 0)
    else:
        cond = lambda i: (0, 0, 0)
    kvw = KV_RANK + LANES
    qw = MLA_HEADS * LANES
    vw = MLA_HEADS * MLA_VD
    in_specs = [
        pl.BlockSpec((TM, D_MODEL), xrow),
        pl.BlockSpec((None, N_MOD, D_MODEL), cond),
        pl.BlockSpec((1, D_MODEL), full),
        pl.BlockSpec((D_MODEL, Q_RANK), full),
        pl.BlockSpec((D_MODEL, kvw), full),
        pl.BlockSpec((1, Q_RANK), full),
        pl.BlockSpec((1, KV_RANK), full),
        pl.BlockSpec((Q_RANK, qw), full),
        pl.BlockSpec((1, LANES), full),
    ]
    args = [x, mod_l, g.reshape(1, D_MODEL), w_dq, w_dkv, qag.reshape(1, Q_RANK),
            kvag.reshape(1, KV_RANK), w_q, qg]
    aliases = {}
    if lat:
        pos = lambda i: (i % per_lat, 0)
        in_specs += [pl.BlockSpec((TM, LANES), pos)] * 3
        args += list(tables)
        out_specs = [pl.BlockSpec((TM, qw), row), pl.BlockSpec((TM, KV_RANK), row),
                     pl.BlockSpec((TM, LANES), row)]
        out_shape = [jax.ShapeDtypeStruct((n_rows, qw), BF16),
                     jax.ShapeDtypeStruct((n_rows, KV_RANK), F32),
                     jax.ShapeDtypeStruct((n_rows, LANES), F32)]
    else:
        w_k, w_v, kg = kv_weights
        in_specs += [pl.BlockSpec((KV_RANK, qw), full), pl.BlockSpec((KV_RANK, vw), full),
                     pl.BlockSpec((1, LANES), full)]
        args += [w_k, w_v, kg]
        slot = lambda i: (i, j, 0, 0)
        out_specs = [pl.BlockSpec((TM, qw), row), pl.BlockSpec((TM, qw), row),
                     pl.BlockSpec((TM, vw), row),
                     pl.BlockSpec((SEQS_PER_TILE, None, SEQ, KV_RANK), slot),
                     pl.BlockSpec((SEQS_PER_TILE, None, MLA_ROPE, SEQ), slot)]
        out_shape = [jax.ShapeDtypeStruct((n_rows, qw), BF16),
                     jax.ShapeDtypeStruct((n_rows, qw), BF16),
                     jax.ShapeDtypeStruct((n_rows, vw), BF16),
                     jax.ShapeDtypeStruct((BATCH, N_MLA, SEQ, KV_RANK), F32),
                     jax.ShapeDtypeStruct((BATCH, N_MLA, MLA_ROPE, SEQ), F32)]
        if stacks is not None:
            aliases = {len(args): 3, len(args) + 1: 4}
            in_specs += [pl.BlockSpec(memory_space=pl.ANY)] * 2
            args += list(stacks)
    return pl.pallas_call(
        functools.partial(_mla_pre_kernel, lat=lat, n_alias=len(aliases)),
        grid=(n_rows // TM,),
        in_specs=in_specs,
        out_specs=out_specs,
        out_shape=out_shape,
        input_output_aliases=aliases,
        compiler_params=_params("parallel"),
        name="mla_pre_lat" if lat else "mla_pre_ctx",
    )(*args)


def _mla_kv_kernel(ckv_ref, kpe_ref, wk_ref, wv_ref, kg_ref, cos_ref, up_ref, dn_ref, k_ref, v_ref):
    _mla_keys(ckv_ref[...], kpe_ref[...], wk_ref, wv_ref, kg_ref[...],
              (cos_ref, up_ref, dn_ref), k_ref, v_ref)


def _mla_kv_lat(ckv_rows, kpe_rows, w_k, w_v, kg, tables):
    n_rows = ckv_rows.shape[0]
    tm = KVP_TM_LAT
    per = LAT_KV // tm
    full = lambda i: (0, 0)
    row = lambda i: (i, 0)
    pos = lambda i: (i % per, 0)
    kw = MLA_HEADS * LANES
    vw = MLA_HEADS * MLA_VD
    return pl.pallas_call(
        _mla_kv_kernel,
        grid=(n_rows // tm,),
        in_specs=[
            pl.BlockSpec((tm, KV_RANK), row),
            pl.BlockSpec((tm, LANES), row),
            pl.BlockSpec((KV_RANK, kw), full),
            pl.BlockSpec((KV_RANK, vw), full),
            pl.BlockSpec((1, LANES), full),
        ] + [pl.BlockSpec((tm, LANES), pos)] * 3,
        out_specs=[pl.BlockSpec((tm, kw), row), pl.BlockSpec((tm, vw), row)],
        out_shape=[jax.ShapeDtypeStruct((n_rows, kw), BF16),
                   jax.ShapeDtypeStruct((n_rows, vw), BF16)],
        compiler_params=_params("parallel"),
        name="mla_kv_lat",
    )(ckv_rows, kpe_rows, w_k, w_v, kg, *tables)


def _mla_attn_kernel(q_ref, k_ref, v_ref, o_ref, *, pairs):
    lane = lax.broadcasted_iota(jnp.int32, (1, LANES), 1)
    lo = lane < MLA_VD
    for p in range(pairs):
        vp = v_ref[:, p * LANES:(p + 1) * LANES]
        v_aug = jnp.concatenate([vp, jnp.ones_like(vp)], axis=1)
        outs = []
        for sub in range(2):
            hd = 2 * p + sub
            sl = slice(hd * LANES, (hd + 1) * LANES)
            outs.append(_softmax_pv(_dot_nt(q_ref[:, sl], k_ref[:, sl]), v_aug))
        o_ref[:, p * LANES:(p + 1) * LANES] = jnp.where(lo, outs[0], outs[1]).astype(o_ref.dtype)


def _mla_attn(q, k, v, pairs_per_step):
    b, lq, _ = q.shape
    lk = k.shape[1]
    n_pg = MLA_HEADS // 2 // pairs_per_step
    tq = min(TQ, lq)
    qkw = pairs_per_step * 2 * LANES
    ow = pairs_per_step * LANES
    qmap = lambda bi, hi, qi: (bi, qi, hi)
    kmap = lambda bi, hi, qi: (bi, 0, hi)
    return pl.pallas_call(
        functools.partial(_mla_attn_kernel, pairs=pairs_per_step),
        grid=(b, n_pg, lq // tq),
        in_specs=[
            pl.BlockSpec((None, tq, qkw), qmap),
            pl.BlockSpec((None, lk, qkw), kmap),
            pl.BlockSpec((None, lk, ow), kmap),
        ],
        out_specs=pl.BlockSpec((None, tq, ow), qmap),
        out_shape=jax.ShapeDtypeStruct((b, lq, D_MODEL), BF16),
        compiler_params=_params("parallel", "parallel", "parallel"),
        name="mla_attn_lat" if lq == DEC_SEQ else "mla_attn_ctx",
    )(q, k, v)


def _rope_tables(n_rot, lane0, identity_rows):
    rows = DEC_SEQ // GRID_W
    row = np.repeat(np.arange(rows), GRID_W)
    col = np.tile(np.arange(GRID_W), rows)
    n = n_rot // 2
    freqs = ROPE_THETA ** (-np.arange(0, n, 2, dtype=np.float64) / n)
    zeros = np.zeros((DEC_SEQ, n // 2))
    cos_parts, up_parts, dn_parts = [], [], []
    for pos in (row, col):
        ang = pos.astype(np.float64)[:, None] * freqs[None, :]
        c, s = np.cos(ang), np.sin(ang)
        cos_parts += [c, c]
        up_parts += [-s, zeros]
        dn_parts += [zeros, s]

    def place(parts, fill):
        reps = 2 if lane0 == 0 else 1
        body = np.concatenate(parts * reps, axis=1)
        full = np.full((identity_rows + DEC_SEQ, LANES), fill)
        full[identity_rows:, lane0:lane0 + body.shape[1]] = body
        return jnp.asarray(full, dtype=F32)

    return place(cos_parts, 1.0), place(up_parts, 0.0), place(dn_parts, 0.0)


def kernel(x_prompt, x_sample, c, cache_diff_k, cache_diff_v, cache_mla_ckv, cache_mla_kpe, c_ctx, w_mod, b_mod, norm_g, ffn_w_in, ffn_w_out, diff_w_qkv, diff_q_norm, diff_k_norm, diff_lambda, diff_subln, diff_w_o, mla_w_down, mla_q_a_norm, mla_kv_a_norm, mla_w_q_up, mla_w_kv_up, mla_q_norm, mla_k_norm, mla_w_o):
    xs = (x_prompt.reshape(N_CTX, D_MODEL), x_sample.reshape(N_LAT, D_MODEL))
    conds = jnp.concatenate(
        [c_ctx[None, :], c, jnp.zeros((COND_ROWS - 1 - DEC_BATCH, D_MODEL), F32)], axis=0)
    mod = _modulation(conds, w_mod, b_mod)

    w_in = ffn_w_in[0, 0].astype(BF16)
    w_out = ffn_w_out[0, 0].astype(BF16)
    diff_tabs = _rope_tables(DIFF_HD, 0, 0)
    mla_tabs = _rope_tables(MLA_ROPE, MLA_NOPE, 0)
    mla_kv_tabs = _rope_tables(MLA_ROPE, MLA_NOPE, PAST_LEN)
    cache_kt = jnp.transpose(cache_diff_k, (0, 1, 3, 4, 5, 2)).reshape(
        DEC_BATCH, N_DIFF, D_MODEL, PAST_LEN)
    cache_v = cache_diff_v.reshape(DEC_BATCH, N_DIFF, PAST_LEN, D_MODEL)

    diff_stacks, mla_stacks = None, None
    for l in range(DEPTH):
        j = l // 2
        mod_l = mod[l]
        x, w_in, w_out = _ffn(xs, mod_l, norm_g[l, 0], w_in, w_out, 0,
                              nxt=(ffn_w_in, ffn_w_out, l, 1))
        g1 = norm_g[l, 1]
        if l % 2 == 0:
            w_qkv = diff_w_qkv[j].astype(BF16)
            qg = jnp.tile(diff_q_norm[j], 2).reshape(1, LANES)
            kg = jnp.tile(diff_k_norm[j], 2).reshape(1, LANES)
            sub_g = diff_subln[j].reshape(1, LANES)
            lam_init = 0.8 - 0.6 * math.exp(-0.3 * l)
            q_c, kt_stack, v_stack = _diff_pre(x, mod_l, g1, w_qkv, qg, kg, None, lat=False,
                                               j=j, stacks=diff_stacks)
            diff_stacks = (kt_stack, v_stack)
            q_l, kt_l, v_l = _diff_pre(x, mod_l, g1, w_qkv, qg, kg, diff_tabs, lat=True)
            o_c = _diff_attn(q_c.reshape(BATCH, SEQ, D_MODEL), kt_stack, v_stack, diff_lambda[j],
                             sub_g, lam_init, heads_per_step=DIFF_HEADS, j=j)
            o_l = _diff_attn(q_l.reshape(DEC_BATCH, DEC_SEQ, D_MODEL), kt_l,
                             v_l.reshape(DEC_BATCH, DEC_SEQ, D_MODEL), diff_lambda[j], sub_g,
                             lam_init, heads_per_step=LAT_HEADS_PER_STEP, j=j,
                             caches=(cache_kt, cache_v))
            w_o = diff_w_o[j].astype(BF16)
        else:
            wd = mla_w_down[j]
            w_dq = wd[:, :Q_RANK].astype(BF16)
            w_dkv = jnp.pad(wd[:, Q_RANK:], ((0, 0), (0, LANES - MLA_ROPE))).astype(BF16)
            w_q = jnp.pad(mla_w_q_up[j].reshape(Q_RANK, MLA_HEADS, MLA_QK),
                          ((0, 0), (0, 0), (0, LANES - MLA_QK))).reshape(Q_RANK, MLA_HEADS * LANES).astype(BF16)
            wkv = mla_w_kv_up[j].reshape(KV_RANK, MLA_HEADS, MLA_NOPE + MLA_VD)
            w_k = jnp.pad(wkv[:, :, :MLA_NOPE], ((0, 0), (0, 0), (0, LANES - MLA_NOPE))
                          ).reshape(KV_RANK, MLA_HEADS * LANES).astype(BF16)
            w_v = wkv[:, :, MLA_NOPE:].reshape(KV_RANK, MLA_HEADS * MLA_VD).astype(BF16)
            qg = jnp.pad(mla_q_norm[j], (0, LANES - MLA_QK)).reshape(1, LANES)
            kg = jnp.pad(mla_k_norm[j], (0, LANES - MLA_QK)).reshape(1, LANES)
            pre = functools.partial(_mla_pre, x, mod_l, g1, w_dq, w_dkv, mla_q_a_norm[j],
                                    mla_kv_a_norm[j], w_q, qg)
            q_c, k_c, v_c, ckv_stack, kpet_stack = pre(lat=False, kv_weights=(w_k, w_v, kg), j=j,
                                                      stacks=mla_stacks)
            mla_stacks = (ckv_stack, kpet_stack)
            q_l, ckv_l, kpe_l = pre(lat=True, tables=mla_tabs)
            cache_kpe = jnp.pad(cache_mla_kpe[:, j], ((0, 0), (0, 0), (0, LANES - MLA_ROPE)))
            ckv_rows = jnp.concatenate([cache_mla_ckv[:, j], ckv_l.reshape(DEC_BATCH, DEC_SEQ, KV_RANK)],
                                       axis=1).reshape(DEC_BATCH * LAT_KV, KV_RANK)
            kpe_rows = jnp.concatenate([cache_kpe, kpe_l.reshape(DEC_BATCH, DEC_SEQ, LANES)],
                                       axis=1).reshape(DEC_BATCH * LAT_KV, LANES)
            k_a, v_a = _mla_kv_lat(ckv_rows, kpe_rows, w_k, w_v, kg, mla_kv_tabs)
            kw = MLA_HEADS * LANES
            o_c = _mla_attn(q_c.reshape(BATCH, SEQ, kw), k_c.reshape(BATCH, SEQ, kw),
                            v_c.reshape(BATCH, SEQ, D_MODEL), pairs_per_step=MLA_HEADS // 2)
            o_l = _mla_attn(q_l.reshape(DEC_BATCH, DEC_SEQ, kw), k_a.reshape(DEC_BATCH, LAT_KV, kw),
                            v_a.reshape(DEC_BATCH, LAT_KV, D_MODEL),
                            pairs_per_step=LAT_HEADS_PER_STEP)
            w_o = mla_w_o[j].astype(BF16)
        os = (o_c.reshape(N_CTX, D_MODEL), o_l.reshape(N_LAT, D_MODEL))
        last = l == DEPTH - 1
        xs = _ffn((x,), mod_l, norm_g[l, 2], w_in, w_out, 1, os=os, w_o=w_o, split_out=last,
                  nxt=None if last else (ffn_w_in, ffn_w_out, l + 1, 0))
        if not last:
            xs, (w_in, w_out) = xs[:1], xs[1:]

    kt_stack, v_stack = diff_stacks
    ckv_stack, kpet_stack = mla_stacks
    new_diff_k = jnp.transpose(kt_stack.reshape(BATCH, N_DIFF, DIFF_HEADS, 2, DIFF_HD, SEQ),
                               (0, 1, 5, 2, 3, 4))
    new_diff_v = v_stack.reshape(BATCH, N_DIFF, SEQ, DIFF_HEADS, DIFF_VD)
    new_mla_kpe = jnp.transpose(kpet_stack, (0, 1, 3, 2))
    return (xs[0].reshape(BATCH, SEQ, D_MODEL), xs[1].reshape(DEC_BATCH, DEC_SEQ, D_MODEL),
            new_diff_k, new_diff_v, ckv_stack, new_mla_kpe)
```

```python
import functools
import math

import jax
import jax.numpy as jnp
import numpy as np
from jax import lax
from jax.experimental import pallas as pl
from jax.experimental.pallas import tpu as pltpu

D_MODEL = 1024
BATCH = 32
SEQ = 256
DEPTH = 4
DEC_BATCH = 2
DEC_SEQ = 1024
PAST_LEN = 256
GRID_W = 64
N_DIFF = (DEPTH + 1) // 2
N_MLA = DEPTH // 2
N_MOD = 9
D_FF = 2816
DIFF_HEADS = 8
DIFF_HD = 64
DIFF_VD = 128
MLA_HEADS = 16
MLA_NOPE = 64
MLA_ROPE = 32
MLA_QK = MLA_NOPE + MLA_ROPE
MLA_VD = 64
Q_RANK = 768
KV_RANK = 256
ROPE_THETA = 10000.0
EPS = 1e-6
LOG2E = math.log2(math.e)
DIFF_Q_SCALE = DIFF_HD ** -0.5 * LOG2E
MLA_Q_SCALE = MLA_QK ** -0.5 * LOG2E

N_CTX = BATCH * SEQ
N_LAT = DEC_BATCH * DEC_SEQ
N_TOK = N_CTX + N_LAT
LAT_KV = PAST_LEN + DEC_SEQ

LANES = 128
COND_ROWS = 8
TM = 512
SEQS_PER_TILE = TM // SEQ
FF_CHUNK = 256
TQ = 256
LAT_HEADS_PER_STEP = 4
KVP_TM_LAT = 640
MOD_TN = 2304
VMEM_LIMIT = 56 * 1024 * 1024

F32 = jnp.float32
BF16 = jnp.bfloat16


def _params(*sem):
    return pltpu.CompilerParams(dimension_semantics=sem, vmem_limit_bytes=VMEM_LIMIT)


def _dot(a, b):
    return jnp.dot(a, b, preferred_element_type=F32)


def _dot_nt(a, b):
    return lax.dot_general(a, b, (((1,), (1,)), ((), ())), preferred_element_type=F32)


def _modulate(x, g, shift, scale):
    ms = jnp.mean(x * x, axis=-1, keepdims=True)
    return (x * lax.rsqrt(ms + EPS) * g) * (1.0 + scale) + shift


def _rope(x, cos, s_up, s_dn, shift):
    return (x * cos + pltpu.roll(x, LANES - shift, 1) * s_up
            + pltpu.roll(x, shift, 1) * s_dn)


def _cond_of_tile(i, first_lat_tile, tiles_per_lat_batch):
    lat = jnp.maximum(i - first_lat_tile, 0) // tiles_per_lat_batch
    return jnp.where(i < first_lat_tile, 0, 1 + lat)


def _mod_kernel(c_ref, w_ref, b_ref, o_ref):
    c = c_ref[...]
    s = (c * jax.nn.sigmoid(c)).astype(BF16)
    o_ref[...] = _dot(s, w_ref[...].astype(BF16)) + b_ref[...]


def _modulation(conds, w_mod, b_mod):
    n_out = N_MOD * D_MODEL
    out = pl.pallas_call(
        _mod_kernel,
        grid=(DEPTH, n_out // MOD_TN),
        in_specs=[
            pl.BlockSpec((COND_ROWS, D_MODEL), lambda l, n: (0, 0)),
            pl.BlockSpec((None, D_MODEL, MOD_TN), lambda l, n: (l, 0, n)),
            pl.BlockSpec((None, 1, MOD_TN), lambda l, n: (l, 0, n)),
        ],
        out_specs=pl.BlockSpec((None, COND_ROWS, MOD_TN), lambda l, n: (l, 0, n)),
        out_shape=jax.ShapeDtypeStruct((DEPTH, COND_ROWS, n_out), F32),
        compiler_params=_params("parallel", "parallel"),
        name="modulation",
    )(conds, w_mod, b_mod.reshape(DEPTH, 1, n_out))
    return out.reshape(DEPTH, COND_ROWS, N_MOD, D_MODEL)


FIRST_LAT_TILE = N_CTX // TM
TILES_PER_LAT_BATCH = DEC_SEQ // TM


def _ctx_tile(i):
    return (jnp.minimum(i, FIRST_LAT_TILE - 1), 0)


def _lat_tile(i):
    return (jnp.maximum(i - FIRST_LAT_TILE, 0), 0)


def _pick_rows(i, ctx_ref, lat_ref):
    rows = i * TM + lax.broadcasted_iota(jnp.int32, (TM, 1), 0)
    return jnp.where(rows < N_CTX, ctx_ref[...], lat_ref[...])


def _ffn_kernel(*refs, n_x, proj, n_out, convert, i_shift, i_scale, i_gate, i_pgate):
    refs = list(refs)
    x_refs = [refs.pop(0) for _ in range(n_x)]
    if proj:
        oc_ref, ol_ref, wo_ref = refs[:3]
        refs = refs[3:]
    mod_ref, g_ref, win_ref, wout_ref = refs[:4]
    refs = refs[4:]
    if convert:
        nin_ref, nout_ref = refs[:2]
        refs = refs[2:]
    out_refs = refs[:n_out]
    refs = refs[n_out:]
    if convert:
        nin_bf_ref, nout_bf_ref = refs[:2]
        refs = refs[2:]
    a_ref = refs[0]
    i = pl.program_id(0)
    if convert:
        @pl.when(i < FIRST_LAT_TILE)
        def _():
            nin_bf_ref[...] = nin_ref[...].astype(BF16)
            nout_bf_ref[...] = nout_ref[...].astype(BF16)
    x = x_refs[0][...] if n_x == 1 else _pick_rows(i, *x_refs)
    if proj:
        o = _pick_rows(i, oc_ref, ol_ref)
        x = x + mod_ref[i_pgate:i_pgate + 1, :] * _dot(o, wo_ref[...])
    h = _modulate(x, g_ref[...], mod_ref[i_shift:i_shift + 1, :],
                  mod_ref[i_scale:i_scale + 1, :]).astype(BF16)
    for c in range(D_FF // FF_CHUNK):
        lo = c * FF_CHUNK
        g = _dot(h, win_ref[:, lo:lo + FF_CHUNK])
        u = _dot(h, win_ref[:, D_FF + lo:D_FF + lo + FF_CHUNK])
        a_ref[:, lo:lo + FF_CHUNK] = ((g * jax.nn.sigmoid(g)) * u).astype(BF16)
    ff = _dot(a_ref[...], wout_ref[...])
    y = x + mod_ref[i_gate:i_gate + 1, :] * (0.5 * ff)
    if n_out == 1:
        out_refs[0][...] = y
    else:
        @pl.when(i < FIRST_LAT_TILE)
        def _():
            out_refs[0][...] = y

        @pl.when(i >= FIRST_LAT_TILE)
        def _():
            out_refs[1][...] = y


def _ffn(xs, mod_l, g, w_in, w_out, which, os=None, w_o=None, split_out=False, nxt=None):
    proj = os is not None
    convert = nxt is not None
    base = 0 if which == 0 else 6
    row = lambda i: (i, 0)
    full = lambda i: (0, 0)
    tile = (TM, D_MODEL)
    if len(xs) == 1:
        in_specs = [pl.BlockSpec(tile, row)]
    else:
        in_specs = [pl.BlockSpec(tile, _ctx_tile), pl.BlockSpec(tile, _lat_tile)]
    args = list(xs)
    if proj:
        in_specs += [pl.BlockSpec(tile, _ctx_tile), pl.BlockSpec(tile, _lat_tile),
                     pl.BlockSpec((D_MODEL, D_MODEL), full)]
        args += [os[0], os[1], w_o]
    in_specs += [
        pl.BlockSpec((None, N_MOD, D_MODEL),
                     lambda i: (_cond_of_tile(i, FIRST_LAT_TILE, TILES_PER_LAT_BATCH), 0, 0)),
        pl.BlockSpec((1, D_MODEL), full),
        pl.BlockSpec((D_MODEL, 2 * D_FF), full),
        pl.BlockSpec((D_FF, D_MODEL), full),
    ]
    args += [mod_l, g.reshape(1, D_MODEL), w_in, w_out]
    in_slab = (D_MODEL // FIRST_LAT_TILE, 2 * D_FF)
    out_slab = (D_FF // FIRST_LAT_TILE, D_MODEL)
    if convert:
        nw_in, nw_out, nl, nw = nxt
        slab = lambda i: (nl, nw, jnp.minimum(i, FIRST_LAT_TILE - 1), 0)
        in_specs += [pl.BlockSpec((None, None) + in_slab, slab),
                     pl.BlockSpec((None, None) + out_slab, slab)]
        args += [nw_in, nw_out]
    if split_out:
        out_specs = [pl.BlockSpec(tile, _ctx_tile), pl.BlockSpec(tile, _lat_tile)]
        out_shape = [jax.ShapeDtypeStruct((N_CTX, D_MODEL), F32),
                     jax.ShapeDtypeStruct((N_LAT, D_MODEL), F32)]
    else:
        out_specs = [pl.BlockSpec(tile, row)]
        out_shape = [jax.ShapeDtypeStruct((N_TOK, D_MODEL), F32)]
    n_out = len(out_shape)
    if convert:
        out_specs += [pl.BlockSpec(in_slab, _ctx_tile), pl.BlockSpec(out_slab, _ctx_tile)]
        out_shape += [jax.ShapeDtypeStruct((D_MODEL, 2 * D_FF), BF16),
                      jax.ShapeDtypeStruct((D_FF, D_MODEL), BF16)]
    kern = functools.partial(_ffn_kernel, n_x=len(xs), proj=proj, n_out=n_out, convert=convert,
                             i_shift=base, i_scale=base + 1, i_gate=base + 2, i_pgate=5)
    return pl.pallas_call(
        kern,
        grid=(N_TOK // TM,),
        in_specs=in_specs,
        out_specs=out_specs,
        out_shape=out_shape,
        scratch_shapes=[pltpu.VMEM((TM, D_FF), BF16)],
        compiler_params=_params("arbitrary"),
        name="ffn_proj" if proj else "ffn",
    )(*args)


def _diff_pre_kernel(*refs, rope, n_alias):
    refs = list(refs)
    x_ref, mod_ref, g_ref, w_ref, qg_ref, kg_ref = refs[:6]
    refs = refs[6:]
    if rope:
        cos_ref, up_ref, dn_ref = refs[:3]
        refs = refs[3:]
    q_ref, kt_ref, v_ref = refs[n_alias:]
    h = _modulate(x_ref[...], g_ref[...], mod_ref[3:4, :], mod_ref[4:5, :]).astype(BF16)
    hw = DIFF_HEADS * DIFF_VD
    lane = lax.broadcasted_iota(jnp.int32, (1, LANES), 1)
    lo = lane < DIFF_HD
    for part, gain_ref in ((1, kg_ref), (0, qg_ref)):
        gain = gain_ref[...]
        y = _dot(h, w_ref[:, part * hw:(part + 1) * hw])
        for hd in range(DIFF_HEADS):
            sl = slice(hd * LANES, (hd + 1) * LANES)
            yh = y[:, sl]
            sq = yh * yh
            s_lo = jnp.sum(jnp.where(lo, sq, 0.0), axis=-1, keepdims=True)
            s_hi = jnp.sum(jnp.where(lo, 0.0, sq), axis=-1, keepdims=True)
            ms = jnp.where(lo, s_lo, s_hi) * (1.0 / DIFF_HD)
            yn = yh * lax.rsqrt(ms + EPS) * gain
            if rope:
                yn = _rope(yn, cos_ref[...], up_ref[...], dn_ref[...], DIFF_HD // 4)
            if part == 0:
                q_ref[:, sl] = (yn * DIFF_Q_SCALE).astype(q_ref.dtype)
            elif len(kt_ref.shape) == 2:
                kt_ref[sl, :] = yn.T
            else:
                ynt = yn.T
                for b in range(SEQS_PER_TILE):
                    kt_ref[b, sl, :] = ynt[:, b * SEQ:(b + 1) * SEQ]
    v = _dot(h, w_ref[:, 2 * hw:3 * hw])
    if len(v_ref.shape) == 2:
        v_ref[...] = v
    else:
        for b in range(SEQS_PER_TILE):
            v_ref[b] = v[b * SEQ:(b + 1) * SEQ]


def _diff_pre(x, mod_l, g, w_qkv, qg, kg, tables, lat, j=0, stacks=None):
    n_rows = N_LAT if lat else N_CTX
    tile0 = N_CTX // TM if lat else 0
    per_lat = DEC_SEQ // TM
    full = lambda i: (0, 0)
    xrow = lambda i: (i + tile0, 0)
    row = lambda i: (i, 0)
    if lat:
        cond = lambda i: (1 + i // per_lat, 0, 0)
    else:
        cond = lambda i: (0, 0, 0)
    in_specs = [
        pl.BlockSpec((TM, D_MODEL), xrow),
        pl.BlockSpec((None, N_MOD, D_MODEL), cond),
        pl.BlockSpec((1, D_MODEL), full),
        pl.BlockSpec((D_MODEL, 3 * D_MODEL), full),
        pl.BlockSpec((1, LANES), full),
        pl.BlockSpec((1, LANES), full),
    ]
    args = [x, mod_l, g.reshape(1, D_MODEL), w_qkv, qg, kg]
    aliases = {}
    if lat:
        pos = lambda i: (i % per_lat, 0)
        in_specs += [pl.BlockSpec((TM, LANES), pos)] * 3
        args += list(tables)
        out_specs = [pl.BlockSpec((TM, D_MODEL), row),
                     pl.BlockSpec((None, D_MODEL, TM), lambda i: (i // per_lat, 0, i % per_lat)),
                     pl.BlockSpec((TM, D_MODEL), row)]
        out_shape = [jax.ShapeDtypeStruct((N_LAT, D_MODEL), BF16),
                     jax.ShapeDtypeStruct((DEC_BATCH, D_MODEL, DEC_SEQ), F32),
                     jax.ShapeDtypeStruct((N_LAT, D_MODEL), F32)]
    else:
        slot = lambda i: (i, j, 0, 0)
        out_specs = [pl.BlockSpec((TM, D_MODEL), row),
                     pl.BlockSpec((SEQS_PER_TILE, None, D_MODEL, SEQ), slot),
                     pl.BlockSpec((SEQS_PER_TILE, None, SEQ, D_MODEL), slot)]
        out_shape = [jax.ShapeDtypeStruct((N_CTX, D_MODEL), BF16),
                     jax.ShapeDtypeStruct((BATCH, N_DIFF, D_MODEL, SEQ), F32),
                     jax.ShapeDtypeStruct((BATCH, N_DIFF, SEQ, D_MODEL), F32)]
        if stacks is not None:
            aliases = {len(args): 1, len(args) + 1: 2}
            in_specs += [pl.BlockSpec(memory_space=pl.ANY)] * 2
            args += list(stacks)
    return pl.pallas_call(
        functools.partial(_diff_pre_kernel, rope=lat, n_alias=len(aliases)),
        grid=(n_rows // TM,),
        in_specs=in_specs,
        out_specs=out_specs,
        out_shape=out_shape,
        input_output_aliases=aliases,
        compiler_params=_params("parallel"),
        name="diff_pre_lat" if lat else "diff_pre_ctx",
    )(*args)


def _softmax_pv(s, v_aug):
    e = jnp.exp2(s - jnp.max(s, axis=-1, keepdims=True)).astype(BF16)
    r = _dot(e, v_aug)
    return r[:, :LANES] * (1.0 / r[:, LANES:])


def _diff_attn_kernel(*refs, heads, lam_init, cached):
    if cached:
        q_ref, ktc_ref, kt_ref, vc_ref, v_ref, lam_ref, sub_ref, o_ref = refs
    else:
        q_ref, kt_ref, v_ref, lam_ref, sub_ref, o_ref = refs
    lp = lam_ref[...]
    lam = (jnp.exp(jnp.sum(lp[0:1, :] * lp[1:2, :], axis=-1, keepdims=True))
           - jnp.exp(jnp.sum(lp[2:3, :] * lp[3:4, :], axis=-1, keepdims=True)) + lam_init)
    tq = q_ref.shape[0]
    lane = lax.broadcasted_iota(jnp.int32, (1, LANES), 1)
    lo = lane < DIFF_HD
    sub_g = sub_ref[...]

    def scores(hd):
        sl = slice(hd * LANES, (hd + 1) * LANES)
        qh = q_ref[:, sl]
        kt = kt_ref[sl, :].astype(BF16)
        if cached:
            kt = jnp.concatenate([ktc_ref[sl, :].astype(BF16), kt], axis=1)
        zero = jnp.zeros_like(qh)
        qq = jnp.concatenate([jnp.where(lo, qh, zero), jnp.where(lo, zero, qh)], axis=0)
        return _dot(qq, kt)

    s_next = scores(0)
    for hd in range(heads):
        sl = slice(hd * LANES, (hd + 1) * LANES)
        s = s_next
        if hd + 1 < heads:
            s_next = scores(hd + 1)
        vh = v_ref[:, sl].astype(BF16)
        if cached:
            vh = jnp.concatenate([vc_ref[:, sl].astype(BF16), vh], axis=0)
        v_aug = jnp.concatenate([vh, jnp.ones_like(vh)], axis=1)
        o12 = _softmax_pv(s, v_aug)
        o = o12[:tq] - lam * o12[tq:]
        ms = jnp.mean(o * o, axis=-1, keepdims=True)
        o = (o * lax.rsqrt(ms + EPS) * sub_g) * (1.0 - lam_init)
        o_ref[:, sl] = o.astype(o_ref.dtype)


def _diff_attn(q, kt, v, lam_p, sub_g, lam_init, heads_per_step, j, caches=None):
    b, lq, _ = q.shape
    cached = caches is not None
    w = heads_per_step * LANES
    n_hg = DIFF_HEADS // heads_per_step
    tq = min(TQ, lq)
    qmap = lambda bi, hi, qi: (bi, qi, hi)
    full = lambda bi, hi, qi: (0, 0)
    q_spec = pl.BlockSpec((None, tq, w), qmap)
    kt_slot = lambda bi, hi, qi: (bi, j, hi, 0)
    v_slot = lambda bi, hi, qi: (bi, j, 0, hi)
    if cached:
        kt_cache, v_cache = caches
        in_specs = [q_spec,
                    pl.BlockSpec((None, None, w, PAST_LEN), kt_slot),
                    pl.BlockSpec((None, w, lq), lambda bi, hi, qi: (bi, hi, 0)),
                    pl.BlockSpec((None, None, PAST_LEN, w), v_slot),
                    pl.BlockSpec((None, lq, w), lambda bi, hi, qi: (bi, 0, hi))]
        args = [q, kt_cache, kt, v_cache, v]
    else:
        in_specs = [q_spec,
                    pl.BlockSpec((None, None, w, lq), kt_slot),
                    pl.BlockSpec((None, None, lq, w), v_slot)]
        args = [q, kt, v]
    in_specs += [pl.BlockSpec((4, DIFF_HD), full), pl.BlockSpec((1, LANES), full)]
    args += [lam_p, sub_g]
    return pl.pallas_call(
        functools.partial(_diff_attn_kernel, heads=heads_per_step, lam_init=lam_init, cached=cached),
        grid=(b, n_hg, lq // tq),
        in_specs=in_specs,
        out_specs=pl.BlockSpec((None, tq, w), qmap),
        out_shape=jax.ShapeDtypeStruct((b, lq, D_MODEL), BF16),
        compiler_params=_params("parallel", "parallel", "parallel"),
        name="diff_attn_lat" if cached else "diff_attn_ctx",
    )(*args)


def _mla_keys(ckv_n, kpe, wk_ref, wv_ref, kg, rope_refs, k_ref, v_ref):
    c = ckv_n.astype(BF16)
    kk = _dot(c, wk_ref[...])
    kpe = pltpu.roll(kpe, MLA_NOPE, 1)
    pe_ss = jnp.sum(kpe * kpe, axis=-1, keepdims=True)
    kpe_g = kpe * kg
    if rope_refs is not None:
        cos_ref, up_ref, dn_ref = rope_refs
        kpe_g = _rope(kpe_g, cos_ref[...], up_ref[...], dn_ref[...], MLA_ROPE // 4)
    for hd in range(MLA_HEADS):
        sl = slice(hd * LANES, (hd + 1) * LANES)
        kh = kk[:, sl]
        ms = (jnp.sum(kh * kh, axis=-1, keepdims=True) + pe_ss) * (1.0 / MLA_QK)
        kn = (kh * kg + kpe_g) * lax.rsqrt(ms + EPS)
        k_ref[:, sl] = kn.astype(k_ref.dtype)
    v_ref[...] = _dot(c, wv_ref[...]).astype(v_ref.dtype)


def _mla_pre_kernel(*refs, lat, n_alias):
    refs = list(refs)
    x_ref, mod_ref, g_ref, wdq_ref, wdkv_ref, qag_ref, kvag_ref, wq_ref, qg_ref = refs[:9]
    refs = refs[9:]
    if lat:
        cos_ref, up_ref, dn_ref = refs[:3]
        q_ref, ckv_ref, kpe_ref = refs[3:]
    else:
        wk_ref, wv_ref, kg_ref = refs[:3]
        q_ref, k_ref, v_ref, ckvs_ref, kpet_ref = refs[3 + n_alias:]
    h = _modulate(x_ref[...], g_ref[...], mod_ref[3:4, :], mod_ref[4:5, :]).astype(BF16)
    d2 = _dot(h, wdkv_ref[...])
    ckv = d2[:, :KV_RANK]
    ms = jnp.mean(ckv * ckv, axis=-1, keepdims=True)
    ckv_n = ckv * lax.rsqrt(ms + EPS) * kvag_ref[...]
    kpe = d2[:, KV_RANK:]
    if lat:
        ckv_ref[...] = ckv_n
        kpe_ref[...] = kpe
    cq = _dot(h, wdq_ref[...])
    ms = jnp.mean(cq * cq, axis=-1, keepdims=True)
    cqn = (cq * lax.rsqrt(ms + EPS) * qag_ref[...]).astype(BF16)
    q = _dot(cqn, wq_ref[...])
    qg = qg_ref[...]
    for hd in range(MLA_HEADS):
        sl = slice(hd * LANES, (hd + 1) * LANES)
        qh = q[:, sl]
        ms = jnp.sum(qh * qh, axis=-1, keepdims=True) * (1.0 / MLA_QK)
        qn = qh * lax.rsqrt(ms + EPS) * qg
        if lat:
            qn = _rope(qn, cos_ref[...], up_ref[...], dn_ref[...], MLA_ROPE // 4)
        q_ref[:, sl] = (qn * MLA_Q_SCALE).astype(q_ref.dtype)
    if not lat:
        kpe_t = kpe.T
        for b in range(SEQS_PER_TILE):
            ckvs_ref[b] = ckv_n[b * SEQ:(b + 1) * SEQ]
            kpet_ref[b] = kpe_t[:MLA_ROPE, b * SEQ:(b + 1) * SEQ]
        _mla_keys(ckv_n, kpe, wk_ref, wv_ref, kg_ref[...], None, k_ref, v_ref)


def _mla_pre(x, mod_l, g, w_dq, w_dkv, qag, kvag, w_q, qg, lat, tables=None, kv_weights=None,
             j=0, stacks=None):
    n_rows = N_LAT if lat else N_CTX
    tile0 = N_CTX // TM if lat else 0
    per_lat = DEC_SEQ // TM
    full = lambda i: (0, 0)
    xrow = lambda i: (i + tile0, 0)
    row = lambda i: (i, 0)
    if lat:
        cond = lambda i: (1 + i // per_lat, 0, 0)
    else:
        cond = lambda i: (0, 0, 0)
    kvw = KV_RANK + LANES
    qw = MLA_HEADS * LANES
    vw = MLA_HEADS * MLA_VD
    in_specs = [
        pl.BlockSpec((TM, D_MODEL), xrow),
        pl.BlockSpec((None, N_MOD, D_MODEL), cond),
        pl.BlockSpec((1, D_MODEL), full),
        pl.BlockSpec((D_MODEL, Q_RANK), full),
        pl.BlockSpec((D_MODEL, kvw), full),
        pl.BlockSpec((1, Q_RANK), full),
        pl.BlockSpec((1, KV_RANK), full),
        pl.BlockSpec((Q_RANK, qw), full),
        pl.BlockSpec((1, LANES), full),
    ]
    args = [x, mod_l, g.reshape(1, D_MODEL), w_dq, w_dkv, qag.reshape(1, Q_RANK),
            kvag.reshape(1, KV_RANK), w_q, qg]
    aliases = {}
    if lat:
        pos = lambda i: (i % per_lat, 0)
        in_specs += [pl.BlockSpec((TM, LANES), pos)] * 3
        args += list(tables)
        out_specs = [pl.BlockSpec((TM, qw), row), pl.BlockSpec((TM, KV_RANK), row),
                     pl.BlockSpec((TM, LANES), row)]
        out_shape = [jax.ShapeDtypeStruct((n_rows, qw), BF16),
                     jax.ShapeDtypeStruct((n_rows, KV_RANK), F32),
                     jax.ShapeDtypeStruct((n_rows, LANES), F32)]
    else:
        w_k, w_v, kg = kv_weights
        in_specs += [pl.BlockSpec((KV_RANK, qw), full), pl.BlockSpec((KV_RANK, vw), full),
                     pl.BlockSpec((1, LANES), full)]
        args += [w_k, w_v, kg]
        slot = lambda i: (i, j, 0, 0)
        out_specs = [pl.BlockSpec((TM, qw), row), pl.BlockSpec((TM, qw), row),
                     pl.BlockSpec((TM, vw), row),
                     pl.BlockSpec((SEQS_PER_TILE, None, SEQ, KV_RANK), slot),
                     pl.BlockSpec((SEQS_PER_TILE, None, MLA_ROPE, SEQ), slot)]
        out_shape = [jax.ShapeDtypeStruct((n_rows, qw), BF16),
                     jax.ShapeDtypeStruct((n_rows, qw), BF16),
                     jax.ShapeDtypeStruct((n_rows, vw), BF16),
                     jax.ShapeDtypeStruct((BATCH, N_MLA, SEQ, KV_RANK), F32),
                     jax.ShapeDtypeStruct((BATCH, N_MLA, MLA_ROPE, SEQ), F32)]
        if stacks is not None:
            aliases = {len(args): 3, len(args) + 1: 4}
            in_specs += [pl.BlockSpec(memory_space=pl.ANY)] * 2
            args += list(stacks)
    return pl.pallas_call(
        functools.partial(_mla_pre_kernel, lat=lat, n_alias=len(aliases)),
        grid=(n_rows // TM,),
        in_specs=in_specs,
        out_specs=out_specs,
        out_shape=out_shape,
        input_output_aliases=aliases,
        compiler_params=_params("parallel"),
        name="mla_pre_lat" if lat else "mla_pre_ctx",
    )(*args)


def _mla_kv_kernel(ckv_ref, kpe_ref, wk_ref, wv_ref, kg_ref, cos_ref, up_ref, dn_ref, k_ref, v_ref):
    _mla_keys(ckv_ref[...], kpe_ref[...], wk_ref, wv_ref, kg_ref[...],
              (cos_ref, up_ref, dn_ref), k_ref, v_ref)


def _mla_kv_lat(ckv_rows, kpe_rows, w_k, w_v, kg, tables):
    n_rows = ckv_rows.shape[0]
    tm = KVP_TM_LAT
    per = LAT_KV // tm
    full = lambda i: (0, 0)
    row = lambda i: (i, 0)
    pos = lambda i: (i % per, 0)
    kw = MLA_HEADS * LANES
    vw = MLA_HEADS * MLA_VD
    return pl.pallas_call(
        _mla_kv_kernel,
        grid=(n_rows // tm,),
        in_specs=[
            pl.BlockSpec((tm, KV_RANK), row),
            pl.BlockSpec((tm, LANES), row),
            pl.BlockSpec((KV_RANK, kw), full),
            pl.BlockSpec((KV_RANK, vw), full),
            pl.BlockSpec((1, LANES), full),
        ] + [pl.BlockSpec((tm, LANES), pos)] * 3,
        out_specs=[pl.BlockSpec((tm, kw), row), pl.BlockSpec((tm, vw), row)],
        out_shape=[jax.ShapeDtypeStruct((n_rows, kw), BF16),
                   jax.ShapeDtypeStruct((n_rows, vw), BF16)],
        compiler_params=_params("parallel"),
        name="mla_kv_lat",
    )(ckv_rows, kpe_rows, w_k, w_v, kg, *tables)


def _mla_attn_kernel(q_ref, k_ref, v_ref, o_ref, *, pairs):
    lane = lax.broadcasted_iota(jnp.int32, (1, LANES), 1)
    lo = lane < MLA_VD

    def scores(hd):
        sl = slice(hd * LANES, (hd + 1) * LANES)
        return _dot_nt(q_ref[:, sl], k_ref[:, sl])

    s_next = scores(0)
    for p in range(pairs):
        vp = v_ref[:, p * LANES:(p + 1) * LANES]
        v_aug = jnp.concatenate([vp, jnp.ones_like(vp)], axis=1)
        outs = []
        for sub in range(2):
            s = s_next
            if 2 * p + sub + 1 < 2 * pairs:
                s_next = scores(2 * p + sub + 1)
            outs.append(_softmax_pv(s, v_aug))
        o_ref[:, p * LANES:(p + 1) * LANES] = jnp.where(lo, outs[0], outs[1]).astype(o_ref.dtype)


def _mla_attn(q, k, v, pairs_per_step):
    b, lq, _ = q.shape
    lk = k.shape[1]
    n_pg = MLA_HEADS // 2 // pairs_per_step
    tq = min(TQ, lq)
    qkw = pairs_per_step * 2 * LANES
    ow = pairs_per_step * LANES
    qmap = lambda bi, hi, qi: (bi, qi, hi)
    kmap = lambda bi, hi, qi: (bi, 0, hi)
    return pl.pallas_call(
        functools.partial(_mla_attn_kernel, pairs=pairs_per_step),
        grid=(b, n_pg, lq // tq),
        in_specs=[
            pl.BlockSpec((None, tq, qkw), qmap),
            pl.BlockSpec((None, lk, qkw), kmap),
            pl.BlockSpec((None, lk, ow), kmap),
        ],
        out_specs=pl.BlockSpec((None, tq, ow), qmap),
        out_shape=jax.ShapeDtypeStruct((b, lq, D_MODEL), BF16),
        compiler_params=_params("parallel", "parallel", "parallel"),
        name="mla_attn_lat" if lq == DEC_SEQ else "mla_attn_ctx",
    )(q, k, v)


def _rope_tables(n_rot, lane0, identity_rows):
    rows = DEC_SEQ // GRID_W
    row = np.repeat(np.arange(rows), GRID_W)
    col = np.tile(np.arange(GRID_W), rows)
    n = n_rot // 2
    freqs = ROPE_THETA ** (-np.arange(0, n, 2, dtype=np.float64) / n)
    zeros = np.zeros((DEC_SEQ, n // 2))
    cos_parts, up_parts, dn_parts = [], [], []
    for pos in (row, col):
        ang = pos.astype(np.float64)[:, None] * freqs[None, :]
        c, s = np.cos(ang), np.sin(ang)
        cos_parts += [c, c]
        up_parts += [-s, zeros]
        dn_parts += [zeros, s]

    def place(parts, fill):
        reps = 2 if lane0 == 0 else 1
        body = np.concatenate(parts * reps, axis=1)
        full = np.full((identity_rows + DEC_SEQ, LANES), fill)
        full[identity_rows:, lane0:lane0 + body.shape[1]] = body
        return jnp.asarray(full, dtype=F32)

    return place(cos_parts, 1.0), place(up_parts, 0.0), place(dn_parts, 0.0)


def kernel(x_prompt, x_sample, c, cache_diff_k, cache_diff_v, cache_mla_ckv, cache_mla_kpe, c_ctx, w_mod, b_mod, norm_g, ffn_w_in, ffn_w_out, diff_w_qkv, diff_q_norm, diff_k_norm, diff_lambda, diff_subln, diff_w_o, mla_w_down, mla_q_a_norm, mla_kv_a_norm, mla_w_q_up, mla_w_kv_up, mla_q_norm, mla_k_norm, mla_w_o):
    xs = (x_prompt.reshape(N_CTX, D_MODEL), x_sample.reshape(N_LAT, D_MODEL))
    conds = jnp.concatenate(
        [c_ctx[None, :], c, jnp.zeros((COND_ROWS - 1 - DEC_BATCH, D_MODEL), F32)], axis=0)
    mod = _modulation(conds, w_mod, b_mod)

    w_in = ffn_w_in[0, 0].astype(BF16)
    w_out = ffn_w_out[0, 0].astype(BF16)
    diff_tabs = _rope_tables(DIFF_HD, 0, 0)
    mla_tabs = _rope_tables(MLA_ROPE, MLA_NOPE, 0)
    mla_kv_tabs = _rope_tables(MLA_ROPE, MLA_NOPE, PAST_LEN)
    cache_kt = jnp.transpose(cache_diff_k, (0, 1, 3, 4, 5, 2)).reshape(
        DEC_BATCH, N_DIFF, D_MODEL, PAST_LEN)
    cache_v = cache_diff_v.reshape(DEC_BATCH, N_DIFF, PAST_LEN, D_MODEL)

    diff_stacks, mla_stacks = None, None
    for l in range(DEPTH):
        j = l // 2
        mod_l = mod[l]
        x, w_in, w_out = _ffn(xs, mod_l, norm_g[l, 0], w_in, w_out, 0,
                              nxt=(ffn_w_in, ffn_w_out, l, 1))
        g1 = norm_g[l, 1]
        if l % 2 == 0:
            w_qkv = diff_w_qkv[j].astype(BF16)
            qg = jnp.tile(diff_q_norm[j], 2).reshape(1, LANES)
            kg = jnp.tile(diff_k_norm[j], 2).reshape(1, LANES)
            sub_g = diff_subln[j].reshape(1, LANES)
            lam_init = 0.8 - 0.6 * math.exp(-0.3 * l)
            q_c, kt_stack, v_stack = _diff_pre(x, mod_l, g1, w_qkv, qg, kg, None, lat=False,
                                               j=j, stacks=diff_stacks)
            diff_stacks = (kt_stack, v_stack)
            q_l, kt_l, v_l = _diff_pre(x, mod_l, g1, w_qkv, qg, kg, diff_tabs, lat=True)
            o_c = _diff_attn(q_c.reshape(BATCH, SEQ, D_MODEL), kt_stack, v_stack, diff_lambda[j],
                             sub_g, lam_init, heads_per_step=DIFF_HEADS, j=j)
            o_l = _diff_attn(q_l.reshape(DEC_BATCH, DEC_SEQ, D_MODEL), kt_l,
                             v_l.reshape(DEC_BATCH, DEC_SEQ, D_MODEL), diff_lambda[j], sub_g,
                             lam_init, heads_per_step=LAT_HEADS_PER_STEP, j=j,
                             caches=(cache_kt, cache_v))
            w_o = diff_w_o[j].astype(BF16)
        else:
            wd = mla_w_down[j]
            w_dq = wd[:, :Q_RANK].astype(BF16)
            w_dkv = jnp.pad(wd[:, Q_RANK:], ((0, 0), (0, LANES - MLA_ROPE))).astype(BF16)
            w_q = jnp.pad(mla_w_q_up[j].reshape(Q_RANK, MLA_HEADS, MLA_QK),
                          ((0, 0), (0, 0), (0, LANES - MLA_QK))).reshape(Q_RANK, MLA_HEADS * LANES).astype(BF16)
            wkv = mla_w_kv_up[j].reshape(KV_RANK, MLA_HEADS, MLA_NOPE + MLA_VD)
            w_k = jnp.pad(wkv[:, :, :MLA_NOPE], ((0, 0), (0, 0), (0, LANES - MLA_NOPE))
                          ).reshape(KV_RANK, MLA_HEADS * LANES).astype(BF16)
            w_v = wkv[:, :, MLA_NOPE:].reshape(KV_RANK, MLA_HEADS * MLA_VD).astype(BF16)
            qg = jnp.pad(mla_q_norm[j], (0, LANES - MLA_QK)).reshape(1, LANES)
            kg = jnp.pad(mla_k_norm[j], (0, LANES - MLA_QK)).reshape(1, LANES)
            pre = functools.partial(_mla_pre, x, mod_l, g1, w_dq, w_dkv, mla_q_a_norm[j],
                                    mla_kv_a_norm[j], w_q, qg)
            q_c, k_c, v_c, ckv_stack, kpet_stack = pre(lat=False, kv_weights=(w_k, w_v, kg), j=j,
                                                      stacks=mla_stacks)
            mla_stacks = (ckv_stack, kpet_stack)
            q_l, ckv_l, kpe_l = pre(lat=True, tables=mla_tabs)
            cache_kpe = jnp.pad(cache_mla_kpe[:, j], ((0, 0), (0, 0), (0, LANES - MLA_ROPE)))
            ckv_rows = jnp.concatenate([cache_mla_ckv[:, j], ckv_l.reshape(DEC_BATCH, DEC_SEQ, KV_RANK)],
                                       axis=1).reshape(DEC_BATCH * LAT_KV, KV_RANK)
            kpe_rows = jnp.concatenate([cache_kpe, kpe_l.reshape(DEC_BATCH, DEC_SEQ, LANES)],
                                       axis=1).reshape(DEC_BATCH * LAT_KV, LANES)
            k_a, v_a = _mla_kv_lat(ckv_rows, kpe_rows, w_k, w_v, kg, mla_kv_tabs)
            kw = MLA_HEADS * LANES
            o_c = _mla_attn(q_c.reshape(BATCH, SEQ, kw), k_c.reshape(BATCH, SEQ, kw),
                            v_c.reshape(BATCH, SEQ, D_MODEL), pairs_per_step=MLA_HEADS // 2)
            o_l = _mla_attn(q_l.reshape(DEC_BATCH, DEC_SEQ, kw), k_a.reshape(DEC_BATCH, LAT_KV, kw),
                            v_a.reshape(DEC_BATCH, LAT_KV, D_MODEL),
                            pairs_per_step=LAT_HEADS_PER_STEP)
            w_o = mla_w_o[j].astype(BF16)
        os = (o_c.reshape(N_CTX, D_MODEL), o_l.reshape(N_LAT, D_MODEL))
        last = l == DEPTH - 1
        xs = _ffn((x,), mod_l, norm_g[l, 2], w_in, w_out, 1, os=os, w_o=w_o, split_out=last,
                  nxt=None if last else (ffn_w_in, ffn_w_out, l + 1, 0))
        if not last:
            xs, (w_in, w_out) = xs[:1], xs[1:]

    kt_stack, v_stack = diff_stacks
    ckv_stack, kpet_stack = mla_stacks
    new_diff_k = jnp.transpose(kt_stack.reshape(BATCH, N_DIFF, DIFF_HEADS, 2, DIFF_HD, SEQ),
                               (0, 1, 5, 2, 3, 4))
    new_diff_v = v_stack.reshape(BATCH, N_DIFF, SEQ, DIFF_HEADS, DIFF_VD)
    new_mla_kpe = jnp.transpose(kpet_stack, (0, 1, 3, 2))
    return (xs[0].reshape(BATCH, SEQ, D_MODEL), xs[1].reshape(DEC_BATCH, DEC_SEQ, D_MODEL),
            new_diff_k, new_diff_v, ckv_stack, new_mla_kpe)
```

```python
import functools
import math

import jax
import jax.numpy as jnp
import numpy as np
from jax import lax
from jax.experimental import pallas as pl
from jax.experimental.pallas import tpu as pltpu

D_MODEL = 1024
BATCH = 32
SEQ = 256
DEPTH = 4
DEC_BATCH = 2
DEC_SEQ = 1024
PAST_LEN = 256
GRID_W = 64
N_DIFF = (DEPTH + 1) // 2
N_MLA = DEPTH // 2
N_MOD = 9
D_FF = 2816
DIFF_HEADS = 8
DIFF_HD = 64
DIFF_VD = 128
MLA_HEADS = 16
MLA_NOPE = 64
MLA_ROPE = 32
MLA_QK = MLA_NOPE + MLA_ROPE
MLA_VD = 64
Q_RANK = 768
KV_RANK = 256
ROPE_THETA = 10000.0
EPS = 1e-6
LOG2E = math.log2(math.e)
DIFF_Q_SCALE = DIFF_HD ** -0.5 * LOG2E
MLA_Q_SCALE = MLA_QK ** -0.5 * LOG2E

N_CTX = BATCH * SEQ
N_LAT = DEC_BATCH * DEC_SEQ
N_TOK = N_CTX + N_LAT
LAT_KV = PAST_LEN + DEC_SEQ

LANES = 128
COND_ROWS = 8
TM = 512
SEQS_PER_TILE = TM // SEQ
FF_CHUNK = 256
TQ = 256
LAT_HEADS_PER_STEP = 8
CTX_SEQS_PER_STEP = 4
KVP_TM_LAT = 640
MOD_TN = 2304
VMEM_LIMIT = 56 * 1024 * 1024

F32 = jnp.float32
BF16 = jnp.bfloat16


def _params(*sem):
    return pltpu.CompilerParams(dimension_semantics=sem, vmem_limit_bytes=VMEM_LIMIT)


def _dot(a, b):
    return jnp.dot(a, b, preferred_element_type=F32)


def _dot_nt(a, b):
    return lax.dot_general(a, b, (((1,), (1,)), ((), ())), preferred_element_type=F32)


def _modulate(x, g, shift, scale):
    ms = jnp.mean(x * x, axis=-1, keepdims=True)
    return (x * lax.rsqrt(ms + EPS) * g) * (1.0 + scale) + shift


def _rope(x, cos, s_up, s_dn, shift):
    return (x * cos + pltpu.roll(x, LANES - shift, 1) * s_up
            + pltpu.roll(x, shift, 1) * s_dn)


def _cond_of_tile(i, first_lat_tile, tiles_per_lat_batch):
    lat = jnp.maximum(i - first_lat_tile, 0) // tiles_per_lat_batch
    return jnp.where(i < first_lat_tile, 0, 1 + lat)


def _mod_kernel(c_ref, w_ref, b_ref, o_ref):
    c = c_ref[...]
    s = (c * jax.nn.sigmoid(c)).astype(BF16)
    o_ref[...] = _dot(s, w_ref[...].astype(BF16)) + b_ref[...]


def _modulation(conds, w_mod, b_mod):
    n_out = N_MOD * D_MODEL
    out = pl.pallas_call(
        _mod_kernel,
        grid=(DEPTH, n_out // MOD_TN),
        in_specs=[
            pl.BlockSpec((COND_ROWS, D_MODEL), lambda l, n: (0, 0)),
            pl.BlockSpec((None, D_MODEL, MOD_TN), lambda l, n: (l, 0, n)),
            pl.BlockSpec((None, 1, MOD_TN), lambda l, n: (l, 0, n)),
        ],
        out_specs=pl.BlockSpec((None, COND_ROWS, MOD_TN), lambda l, n: (l, 0, n)),
        out_shape=jax.ShapeDtypeStruct((DEPTH, COND_ROWS, n_out), F32),
        compiler_params=_params("parallel", "parallel"),
        name="modulation",
    )(conds, w_mod, b_mod.reshape(DEPTH, 1, n_out))
    return out.reshape(DEPTH, COND_ROWS, N_MOD, D_MODEL)


FIRST_LAT_TILE = N_CTX // TM
TILES_PER_LAT_BATCH = DEC_SEQ // TM


def _ctx_tile(i):
    return (jnp.minimum(i, FIRST_LAT_TILE - 1), 0)


def _lat_tile(i):
    return (jnp.maximum(i - FIRST_LAT_TILE, 0), 0)


def _pick_rows(i, ctx_ref, lat_ref):
    rows = i * TM + lax.broadcasted_iota(jnp.int32, (TM, 1), 0)
    return jnp.where(rows < N_CTX, ctx_ref[...], lat_ref[...])


def _ffn_kernel(*refs, n_x, proj, n_out, convert, i_shift, i_scale, i_gate, i_pgate):
    refs = list(refs)
    x_refs = [refs.pop(0) for _ in range(n_x)]
    if proj:
        oc_ref, ol_ref, wo_ref = refs[:3]
        refs = refs[3:]
    mod_ref, g_ref, win_ref, wout_ref = refs[:4]
    refs = refs[4:]
    if convert:
        nin_ref, nout_ref = refs[:2]
        refs = refs[2:]
    out_refs = refs[:n_out]
    refs = refs[n_out:]
    if convert:
        nin_bf_ref, nout_bf_ref = refs[:2]
        refs = refs[2:]
    a_ref = refs[0]
    i = pl.program_id(0)
    if convert:
        @pl.when(i < FIRST_LAT_TILE)
        def _():
            nin_bf_ref[...] = nin_ref[...].astype(BF16)
            nout_bf_ref[...] = nout_ref[...].astype(BF16)
    x = x_refs[0][...] if n_x == 1 else _pick_rows(i, *x_refs)
    if proj:
        o = _pick_rows(i, oc_ref, ol_ref)
        x = x + mod_ref[i_pgate:i_pgate + 1, :] * _dot(o, wo_ref[...])
    h = _modulate(x, g_ref[...], mod_ref[i_shift:i_shift + 1, :],
                  mod_ref[i_scale:i_scale + 1, :]).astype(BF16)
    for c in range(D_FF // FF_CHUNK):
        lo = c * FF_CHUNK
        g = _dot(h, win_ref[:, lo:lo + FF_CHUNK])
        u = _dot(h, win_ref[:, D_FF + lo:D_FF + lo + FF_CHUNK])
        a_ref[:, lo:lo + FF_CHUNK] = ((g * jax.nn.sigmoid(g)) * u).astype(BF16)
    ff = _dot(a_ref[...], wout_ref[...])
    y = x + mod_ref[i_gate:i_gate + 1, :] * (0.5 * ff)
    if n_out == 1:
        out_refs[0][...] = y
    else:
        @pl.when(i < FIRST_LAT_TILE)
        def _():
            out_refs[0][...] = y

        @pl.when(i >= FIRST_LAT_TILE)
        def _():
            out_refs[1][...] = y


def _ffn(xs, mod_l, g, w_in, w_out, which, os=None, w_o=None, split_out=False, nxt=None):
    proj = os is not None
    convert = nxt is not None
    base = 0 if which == 0 else 6
    row = lambda i: (i, 0)
    full = lambda i: (0, 0)
    tile = (TM, D_MODEL)
    if len(xs) == 1:
        in_specs = [pl.BlockSpec(tile, row)]
    else:
        in_specs = [pl.BlockSpec(tile, _ctx_tile), pl.BlockSpec(tile, _lat_tile)]
    args = list(xs)
    if proj:
        in_specs += [pl.BlockSpec(tile, _ctx_tile), pl.BlockSpec(tile, _lat_tile),
                     pl.BlockSpec((D_MODEL, D_MODEL), full)]
        args += [os[0], os[1], w_o]
    in_specs += [
        pl.BlockSpec((None, N_MOD, D_MODEL),
                     lambda i: (_cond_of_tile(i, FIRST_LAT_TILE, TILES_PER_LAT_BATCH), 0, 0)),
        pl.BlockSpec((1, D_MODEL), full),
        pl.BlockSpec((D_MODEL, 2 * D_FF), full),
        pl.BlockSpec((D_FF, D_MODEL), full),
    ]
    args += [mod_l, g.reshape(1, D_MODEL), w_in, w_out]
    in_slab = (D_MODEL // FIRST_LAT_TILE, 2 * D_FF)
    out_slab = (D_FF // FIRST_LAT_TILE, D_MODEL)
    if convert:
        nw_in, nw_out, nl, nw = nxt
        slab = lambda i: (nl, nw, jnp.minimum(i, FIRST_LAT_TILE - 1), 0)
        in_specs += [pl.BlockSpec((None, None) + in_slab, slab),
                     pl.BlockSpec((None, None) + out_slab, slab)]
        args += [nw_in, nw_out]
    if split_out:
        out_specs = [pl.BlockSpec(tile, _ctx_tile), pl.BlockSpec(tile, _lat_tile)]
        out_shape = [jax.ShapeDtypeStruct((N_CTX, D_MODEL), F32),
                     jax.ShapeDtypeStruct((N_LAT, D_MODEL), F32)]
    else:
        out_specs = [pl.BlockSpec(tile, row)]
        out_shape = [jax.ShapeDtypeStruct((N_TOK, D_MODEL), F32)]
    n_out = len(out_shape)
    if convert:
        out_specs += [pl.BlockSpec(in_slab, _ctx_tile), pl.BlockSpec(out_slab, _ctx_tile)]
        out_shape += [jax.ShapeDtypeStruct((D_MODEL, 2 * D_FF), BF16),
                      jax.ShapeDtypeStruct((D_FF, D_MODEL), BF16)]
    kern = functools.partial(_ffn_kernel, n_x=len(xs), proj=proj, n_out=n_out, convert=convert,
                             i_shift=base, i_scale=base + 1, i_gate=base + 2, i_pgate=5)
    return pl.pallas_call(
        kern,
        grid=(N_TOK // TM,),
        in_specs=in_specs,
        out_specs=out_specs,
        out_shape=out_shape,
        scratch_shapes=[pltpu.VMEM((TM, D_FF), BF16)],
        compiler_params=_params("arbitrary"),
        name="ffn_proj" if proj else "ffn",
    )(*args)


def _diff_pre_kernel(*refs, rope, n_alias):
    refs = list(refs)
    x_ref, mod_ref, g_ref, w_ref, qg_ref, kg_ref = refs[:6]
    refs = refs[6:]
    if rope:
        cos_ref, up_ref, dn_ref = refs[:3]
        refs = refs[3:]
    q_ref, kt_ref, v_ref = refs[n_alias:]
    h = _modulate(x_ref[...], g_ref[...], mod_ref[3:4, :], mod_ref[4:5, :]).astype(BF16)
    hw = DIFF_HEADS * DIFF_VD
    lane = lax.broadcasted_iota(jnp.int32, (1, LANES), 1)
    lo = lane < DIFF_HD
    for part, gain_ref in ((1, kg_ref), (0, qg_ref)):
        gain = gain_ref[...]
        y = _dot(h, w_ref[:, part * hw:(part + 1) * hw])
        for hd in range(DIFF_HEADS):
            sl = slice(hd * LANES, (hd + 1) * LANES)
            yh = y[:, sl]
            sq = yh * yh
            s_lo = jnp.sum(jnp.where(lo, sq, 0.0), axis=-1, keepdims=True)
            s_hi = jnp.sum(jnp.where(lo, 0.0, sq), axis=-1, keepdims=True)
            ms = jnp.where(lo, s_lo, s_hi) * (1.0 / DIFF_HD)
            yn = yh * lax.rsqrt(ms + EPS) * gain
            if rope:
                yn = _rope(yn, cos_ref[...], up_ref[...], dn_ref[...], DIFF_HD // 4)
            if part == 0:
                q_ref[:, sl] = yn.astype(q_ref.dtype)
            elif len(kt_ref.shape) == 2:
                kt_ref[sl, :] = yn.T
            else:
                ynt = yn.T
                for b in range(SEQS_PER_TILE):
                    kt_ref[b, sl, :] = ynt[:, b * SEQ:(b + 1) * SEQ]
    v = _dot(h, w_ref[:, 2 * hw:3 * hw])
    if len(v_ref.shape) == 2:
        v_ref[...] = v
    else:
        for b in range(SEQS_PER_TILE):
            v_ref[b] = v[b * SEQ:(b + 1) * SEQ]


def _diff_pre(x, mod_l, g, w_qkv, qg, kg, tables, lat, j=0, stacks=None):
    n_rows = N_LAT if lat else N_CTX
    tile0 = N_CTX // TM if lat else 0
    per_lat = DEC_SEQ // TM
    full = lambda i: (0, 0)
    xrow = lambda i: (i + tile0, 0)
    row = lambda i: (i, 0)
    if lat:
        cond = lambda i: (1 + i // per_lat, 0, 0)
    else:
        cond = lambda i: (0, 0, 0)
    in_specs = [
        pl.BlockSpec((TM, D_MODEL), xrow),
        pl.BlockSpec((None, N_MOD, D_MODEL), cond),
        pl.BlockSpec((1, D_MODEL), full),
        pl.BlockSpec((D_MODEL, 3 * D_MODEL), full),
        pl.BlockSpec((1, LANES), full),
        pl.BlockSpec((1, LANES), full),
    ]
    args = [x, mod_l, g.reshape(1, D_MODEL), w_qkv, qg, kg]
    aliases = {}
    if lat:
        pos = lambda i: (i % per_lat, 0)
        in_specs += [pl.BlockSpec((TM, LANES), pos)] * 3
        args += list(tables)
        out_specs = [pl.BlockSpec((TM, D_MODEL), row),
                     pl.BlockSpec((None, D_MODEL, TM), lambda i: (i // per_lat, 0, i % per_lat)),
                     pl.BlockSpec((TM, D_MODEL), row)]
        out_shape = [jax.ShapeDtypeStruct((N_LAT, D_MODEL), BF16),
                     jax.ShapeDtypeStruct((DEC_BATCH, D_MODEL, DEC_SEQ), F32),
                     jax.ShapeDtypeStruct((N_LAT, D_MODEL), F32)]
    else:
        slot = lambda i: (i, j, 0, 0)
        out_specs = [pl.BlockSpec((TM, D_MODEL), row),
                     pl.BlockSpec((SEQS_PER_TILE, None, D_MODEL, SEQ), slot),
                     pl.BlockSpec((SEQS_PER_TILE, None, SEQ, D_MODEL), slot)]
        out_shape = [jax.ShapeDtypeStruct((N_CTX, D_MODEL), BF16),
                     jax.ShapeDtypeStruct((BATCH, N_DIFF, D_MODEL, SEQ), F32),
                     jax.ShapeDtypeStruct((BATCH, N_DIFF, SEQ, D_MODEL), F32)]
        if stacks is not None:
            aliases = {len(args): 1, len(args) + 1: 2}
            in_specs += [pl.BlockSpec(memory_space=pl.ANY)] * 2
            args += list(stacks)
    return pl.pallas_call(
        functools.partial(_diff_pre_kernel, rope=lat, n_alias=len(aliases)),
        grid=(n_rows // TM,),
        in_specs=in_specs,
        out_specs=out_specs,
        out_shape=out_shape,
        input_output_aliases=aliases,
        compiler_params=_params("parallel"),
        name="diff_pre_lat" if lat else "diff_pre_ctx",
    )(*args)


def _softmax_pv(s, v_aug):
    e = jnp.exp2(s - jnp.max(s, axis=-1, keepdims=True)).astype(BF16)
    r = _dot(e, v_aug)
    return r[:, :LANES] * (1.0 / r[:, LANES:])


def _diff_attn_kernel(*refs, heads, lam_init, cached):
    if cached:
        q_ref, ktc_ref, kt_ref, vc_ref, v_ref, lam_ref, sub_ref, o_ref = refs
    else:
        q_ref, kt_ref, v_ref, lam_ref, sub_ref, o_ref = refs
    lp = lam_ref[...]
    lam = (jnp.exp(jnp.sum(lp[0:1, :] * lp[1:2, :], axis=-1, keepdims=True))
           - jnp.exp(jnp.sum(lp[2:3, :] * lp[3:4, :], axis=-1, keepdims=True)) + lam_init)
    n_b, tq = q_ref.shape[:2]
    lane = lax.broadcasted_iota(jnp.int32, (1, LANES), 1)
    lo = lane < DIFF_HD
    sub_g = sub_ref[...]
    items = [(b, hd) for b in range(n_b) for hd in range(heads)]

    def scores(item):
        b, hd = item
        sl = slice(hd * LANES, (hd + 1) * LANES)
        qh = q_ref[b, :, sl]
        kt = kt_ref[b, sl, :].astype(BF16)
        if cached:
            kt = jnp.concatenate([ktc_ref[b, sl, :].astype(BF16), kt], axis=1)
        zero = jnp.zeros_like(qh)
        qq = jnp.concatenate([jnp.where(lo, qh, zero), jnp.where(lo, zero, qh)], axis=0)
        return _dot(qq, kt)

    s_next = scores(items[0])
    for n, (b, hd) in enumerate(items):
        sl = slice(hd * LANES, (hd + 1) * LANES)
        s = s_next
        if n + 1 < len(items):
            s_next = scores(items[n + 1])
        vh = v_ref[b, :, sl].astype(BF16)
        if cached:
            vh = jnp.concatenate([vc_ref[b, :, sl].astype(BF16), vh], axis=0)
        v_aug = jnp.concatenate([vh, jnp.ones_like(vh)], axis=1)
        o12 = _softmax_pv(s, v_aug)
        o = o12[:tq] - lam * o12[tq:]
        ms = jnp.mean(o * o, axis=-1, keepdims=True)
        o = (o * lax.rsqrt(ms + EPS) * sub_g) * (1.0 - lam_init)
        o_ref[b, :, sl] = o.astype(o_ref.dtype)


def _diff_attn(q, kt, v, lam_p, sub_g, lam_init, heads_per_step, j, caches=None):
    b, lq, _ = q.shape
    cached = caches is not None
    nb = 1 if cached else CTX_SEQS_PER_STEP
    w = heads_per_step * LANES
    n_hg = DIFF_HEADS // heads_per_step
    tq = min(TQ, lq)
    qmap = lambda bi, hi, qi: (bi, qi, hi)
    full = lambda bi, hi, qi: (0, 0)
    q_spec = pl.BlockSpec((nb, tq, w), qmap)
    kt_slot = lambda bi, hi, qi: (bi, j, hi, 0)
    v_slot = lambda bi, hi, qi: (bi, j, 0, hi)
    if cached:
        kt_cache, v_cache = caches
        in_specs = [q_spec,
                    pl.BlockSpec((nb, None, w, PAST_LEN), kt_slot),
                    pl.BlockSpec((nb, w, lq), lambda bi, hi, qi: (bi, hi, 0)),
                    pl.BlockSpec((nb, None, PAST_LEN, w), v_slot),
                    pl.BlockSpec((nb, lq, w), lambda bi, hi, qi: (bi, 0, hi))]
        args = [q, kt_cache, kt, v_cache, v]
    else:
        in_specs = [q_spec,
                    pl.BlockSpec((nb, None, w, lq), kt_slot),
                    pl.BlockSpec((nb, None, lq, w), v_slot)]
        args = [q, kt, v]
    in_specs += [pl.BlockSpec((4, DIFF_HD), full), pl.BlockSpec((1, LANES), full)]
    args += [lam_p, sub_g]
    return pl.pallas_call(
        functools.partial(_diff_attn_kernel, heads=heads_per_step, lam_init=lam_init, cached=cached),
        grid=(b // nb, n_hg, lq // tq),
        in_specs=in_specs,
        out_specs=pl.BlockSpec((nb, tq, w), qmap),
        out_shape=jax.ShapeDtypeStruct((b, lq, D_MODEL), BF16),
        compiler_params=_params("parallel", "parallel", "parallel"),
        name="diff_attn_lat" if cached else "diff_attn_ctx",
    )(*args)


def _mla_keys(ckv_n, kpe, wk_ref, wv_ref, kg, rope_refs, k_ref, v_ref):
    c = ckv_n.astype(BF16)
    kk = _dot(c, wk_ref[...])
    kpe = pltpu.roll(kpe, MLA_NOPE, 1)
    pe_ss = jnp.sum(kpe * kpe, axis=-1, keepdims=True)
    kpe_g = kpe * kg
    if rope_refs is not None:
        cos_ref, up_ref, dn_ref = rope_refs
        kpe_g = _rope(kpe_g, cos_ref[...], up_ref[...], dn_ref[...], MLA_ROPE // 4)
    for hd in range(MLA_HEADS):
        sl = slice(hd * LANES, (hd + 1) * LANES)
        kh = kk[:, sl]
        ms = (jnp.sum(kh * kh, axis=-1, keepdims=True) + pe_ss) * (1.0 / MLA_QK)
        kn = (kh * kg + kpe_g) * lax.rsqrt(ms + EPS)
        k_ref[:, sl] = kn.astype(k_ref.dtype)
    v_ref[...] = _dot(c, wv_ref[...]).astype(v_ref.dtype)


def _mla_pre_kernel(*refs, lat, n_alias):
    refs = list(refs)
    x_ref, mod_ref, g_ref, wdq_ref, wdkv_ref, qag_ref, kvag_ref, wq_ref, qg_ref = refs[:9]
    refs = refs[9:]
    if lat:
        cos_ref, up_ref, dn_ref = refs[:3]
        q_ref, ckv_ref, kpe_ref = refs[3:]
    else:
        wk_ref, wv_ref, kg_ref = refs[:3]
        q_ref, k_ref, v_ref, ckvs_ref, kpet_ref = refs[3 + n_alias:]
    h = _modulate(x_ref[...], g_ref[...], mod_ref[3:4, :], mod_ref[4:5, :]).astype(BF16)
    d2 = _dot(h, wdkv_ref[...])
    ckv = d2[:, :KV_RANK]
    ms = jnp.mean(ckv * ckv, axis=-1, keepdims=True)
    ckv_n = ckv * lax.rsqrt(ms + EPS) * kvag_ref[...]
    kpe = d2[:, KV_RANK:]
    if lat:
        ckv_ref[...] = ckv_n
        kpe_ref[...] = kpe
    cq = _dot(h, wdq_ref[...])
    ms = jnp.mean(cq * cq, axis=-1, keepdims=True)
    cqn = (cq * lax.rsqrt(ms + EPS) * qag_ref[...]).astype(BF16)
    q = _dot(cqn, wq_ref[...])
    qg = qg_ref[...]
    for hd in range(MLA_HEADS):
        sl = slice(hd * LANES, (hd + 1) * LANES)
        qh = q[:, sl]
        ms = jnp.sum(qh * qh, axis=-1, keepdims=True) * (1.0 / MLA_QK)
        qn = qh * lax.rsqrt(ms + EPS) * qg
        if lat:
            qn = _rope(qn, cos_ref[...], up_ref[...], dn_ref[...], MLA_ROPE // 4)
        q_ref[:, sl] = qn.astype(q_ref.dtype)
    if not lat:
        kpe_t = kpe.T
        for b in range(SEQS_PER_TILE):
            ckvs_ref[b] = ckv_n[b * SEQ:(b + 1) * SEQ]
            kpet_ref[b] = kpe_t[:MLA_ROPE, b * SEQ:(b + 1) * SEQ]
        _mla_keys(ckv_n, kpe, wk_ref, wv_ref, kg_ref[...], None, k_ref, v_ref)


def _mla_pre(x, mod_l, g, w_dq, w_dkv, qag, kvag, w_q, qg, lat, tables=None, kv_weights=None,
             j=0, stacks=None):
    n_rows = N_LAT if lat else N_CTX
    tile0 = N_CTX // TM if lat else 0
    per_lat = DEC_SEQ // TM
    full = lambda i: (0, 0)
    xrow = lambda i: (i + tile0, 0)
    row = lambda i: (i, 0)
    if lat:
        cond = lambda i: (1 + i // per_lat, 0, 0)
    else:
        cond = lambda i: (0, 0, 0)
    kvw = KV_RANK + LANES
    qw = MLA_HEADS * LANES
    vw = MLA_HEADS * MLA_VD
    in_specs = [
        pl.BlockSpec((TM, D_MODEL), xrow),
        pl.BlockSpec((None, N_MOD, D_MODEL), cond),
        pl.BlockSpec((1, D_MODEL), full),
        pl.BlockSpec((D_MODEL, Q_RANK), full),
        pl.BlockSpec((D_MODEL, kvw), full),
        pl.BlockSpec((1, Q_RANK), full),
        pl.BlockSpec((1, KV_RANK), full),
        pl.BlockSpec((Q_RANK, qw), full),
        pl.BlockSpec((1, LANES), full),
    ]
    args = [x, mod_l, g.reshape(1, D_MODEL), w_dq, w_dkv, qag.reshape(1, Q_RANK),
            kvag.reshape(1, KV_RANK), w_q, qg]
    aliases = {}
    if lat:
        pos = lambda i: (i % per_lat, 0)
        in_specs += [pl.BlockSpec((TM, LANES), pos)] * 3
        args += list(tables)
        out_specs = [pl.BlockSpec((TM, qw), row), pl.BlockSpec((TM, KV_RANK), row),
                     pl.BlockSpec((TM, LANES), row)]
        out_shape = [jax.ShapeDtypeStruct((n_rows, qw), BF16),
                     jax.ShapeDtypeStruct((n_rows, KV_RANK), F32),
                     jax.ShapeDtypeStruct((n_rows, LANES), F32)]
    else:
        w_k, w_v, kg = kv_weights
        in_specs += [pl.BlockSpec((KV_RANK, qw), full), pl.BlockSpec((KV_RANK, vw), full),
                     pl.BlockSpec((1, LANES), full)]
        args += [w_k, w_v, kg]
        slot = lambda i: (i, j, 0, 0)
        out_specs = [pl.BlockSpec((TM, qw), row), pl.BlockSpec((TM, qw), row),
                     pl.BlockSpec((TM, vw), row),
                     pl.BlockSpec((SEQS_PER_TILE, None, SEQ, KV_RANK), slot),
                     pl.BlockSpec((SEQS_PER_TILE, None, MLA_ROPE, SEQ), slot)]
        out_shape = [jax.ShapeDtypeStruct((n_rows, qw), BF16),
                     jax.ShapeDtypeStruct((n_rows, qw), BF16),
                     jax.ShapeDtypeStruct((n_rows, vw), BF16),
                     jax.ShapeDtypeStruct((BATCH, N_MLA, SEQ, KV_RANK), F32),
                     jax.ShapeDtypeStruct((BATCH, N_MLA, MLA_ROPE, SEQ), F32)]
        if stacks is not None:
            aliases = {len(args): 3, len(args) + 1: 4}
            in_specs += [pl.BlockSpec(memory_space=pl.ANY)] * 2
            args += list(stacks)
    return pl.pallas_call(
        functools.partial(_mla_pre_kernel, lat=lat, n_alias=len(aliases)),
        grid=(n_rows // TM,),
        in_specs=in_specs,
        out_specs=out_specs,
        out_shape=out_shape,
        input_output_aliases=aliases,
        compiler_params=_params("parallel"),
        name="mla_pre_lat" if lat else "mla_pre_ctx",
    )(*args)


def _mla_kv_kernel(ckv_ref, kpe_ref, wk_ref, wv_ref, kg_ref, cos_ref, up_ref, dn_ref, k_ref, v_ref):
    _mla_keys(ckv_ref[...], kpe_ref[...], wk_ref, wv_ref, kg_ref[...],
              (cos_ref, up_ref, dn_ref), k_ref, v_ref)


def _mla_kv_lat(ckv_rows, kpe_rows, w_k, w_v, kg, tables):
    n_rows = ckv_rows.shape[0]
    tm = KVP_TM_LAT
    per = LAT_KV // tm
    full = lambda i: (0, 0)
    row = lambda i: (i, 0)
    pos = lambda i: (i % per, 0)
    kw = MLA_HEADS * LANES
    vw = MLA_HEADS * MLA_VD
    return pl.pallas_call(
        _mla_kv_kernel,
        grid=(n_rows // tm,),
        in_specs=[
            pl.BlockSpec((tm, KV_RANK), row),
            pl.BlockSpec((tm, LANES), row),
            pl.BlockSpec((KV_RANK, kw), full),
            pl.BlockSpec((KV_RANK, vw), full),
            pl.BlockSpec((1, LANES), full),
        ] + [pl.BlockSpec((tm, LANES), pos)] * 3,
        out_specs=[pl.BlockSpec((tm, kw), row), pl.BlockSpec((tm, vw), row)],
        out_shape=[jax.ShapeDtypeStruct((n_rows, kw), BF16),
                   jax.ShapeDtypeStruct((n_rows, vw), BF16)],
        compiler_params=_params("parallel"),
        name="mla_kv_lat",
    )(ckv_rows, kpe_rows, w_k, w_v, kg, *tables)


def _mla_attn_kernel(q_ref, k_ref, v_ref, o_ref, *, pairs):
    lane = lax.broadcasted_iota(jnp.int32, (1, LANES), 1)
    lo = lane < MLA_VD

    items = [(b, hd) for b in range(q_ref.shape[0]) for hd in range(2 * pairs)]

    def scores(item):
        b, hd = item
        sl = slice(hd * LANES, (hd + 1) * LANES)
        return _dot_nt(q_ref[b, :, sl], k_ref[b, :, sl])

    s_next = scores(items[0])
    outs = []
    for n, (b, hd) in enumerate(items):
        s = s_next
        if n + 1 < len(items):
            s_next = scores(items[n + 1])
        pl_ = slice((hd // 2) * LANES, (hd // 2 + 1) * LANES)
        vp = v_ref[b, :, pl_]
        outs.append(_softmax_pv(s, jnp.concatenate([vp, jnp.ones_like(vp)], axis=1)))
        if hd % 2 == 1:
            o_ref[b, :, pl_] = jnp.where(lo, outs[0], outs[1]).astype(o_ref.dtype)
            outs = []


def _mla_attn(q, k, v, pairs_per_step):
    b, lq, _ = q.shape
    lk = k.shape[1]
    nb = CTX_SEQS_PER_STEP if lq == SEQ else 1
    n_pg = MLA_HEADS // 2 // pairs_per_step
    tq = min(TQ, lq)
    qkw = pairs_per_step * 2 * LANES
    ow = pairs_per_step * LANES
    qmap = lambda bi, hi, qi: (bi, qi, hi)
    kmap = lambda bi, hi, qi: (bi, 0, hi)
    return pl.pallas_call(
        functools.partial(_mla_attn_kernel, pairs=pairs_per_step),
        grid=(b // nb, n_pg, lq // tq),
        in_specs=[
            pl.BlockSpec((nb, tq, qkw), qmap),
            pl.BlockSpec((nb, lk, qkw), kmap),
            pl.BlockSpec((nb, lk, ow), kmap),
        ],
        out_specs=pl.BlockSpec((nb, tq, ow), qmap),
        out_shape=jax.ShapeDtypeStruct((b, lq, D_MODEL), BF16),
        compiler_params=_params("parallel", "parallel", "parallel"),
        name="mla_attn_lat" if lq == DEC_SEQ else "mla_attn_ctx",
    )(q, k, v)


def _rope_tables(n_rot, lane0, identity_rows):
    rows = DEC_SEQ // GRID_W
    row = np.repeat(np.arange(rows), GRID_W)
    col = np.tile(np.arange(GRID_W), rows)
    n = n_rot // 2
    freqs = ROPE_THETA ** (-np.arange(0, n, 2, dtype=np.float64) / n)
    zeros = np.zeros((DEC_SEQ, n // 2))
    cos_parts, up_parts, dn_parts = [], [], []
    for pos in (row, col):
        ang = pos.astype(np.float64)[:, None] * freqs[None, :]
        c, s = np.cos(ang), np.sin(ang)
        cos_parts += [c, c]
        up_parts += [-s, zeros]
        dn_parts += [zeros, s]

    def place(parts, fill):
        reps = 2 if lane0 == 0 else 1
        body = np.concatenate(parts * reps, axis=1)
        full = np.full((identity_rows + DEC_SEQ, LANES), fill)
        full[identity_rows:, lane0:lane0 + body.shape[1]] = body
        return jnp.asarray(full, dtype=F32)

    return place(cos_parts, 1.0), place(up_parts, 0.0), place(dn_parts, 0.0)


def kernel(x_prompt, x_sample, c, cache_diff_k, cache_diff_v, cache_mla_ckv, cache_mla_kpe, c_ctx, w_mod, b_mod, norm_g, ffn_w_in, ffn_w_out, diff_w_qkv, diff_q_norm, diff_k_norm, diff_lambda, diff_subln, diff_w_o, mla_w_down, mla_q_a_norm, mla_kv_a_norm, mla_w_q_up, mla_w_kv_up, mla_q_norm, mla_k_norm, mla_w_o):
    xs = (x_prompt.reshape(N_CTX, D_MODEL), x_sample.reshape(N_LAT, D_MODEL))
    conds = jnp.concatenate(
        [c_ctx[None, :], c, jnp.zeros((COND_ROWS - 1 - DEC_BATCH, D_MODEL), F32)], axis=0)
    mod = _modulation(conds, w_mod, b_mod)

    w_in = ffn_w_in[0, 0].astype(BF16)
    w_out = ffn_w_out[0, 0].astype(BF16)
    diff_tabs = _rope_tables(DIFF_HD, 0, 0)
    mla_tabs = _rope_tables(MLA_ROPE, MLA_NOPE, 0)
    mla_kv_tabs = _rope_tables(MLA_ROPE, MLA_NOPE, PAST_LEN)
    cache_kt = jnp.transpose(cache_diff_k, (0, 1, 3, 4, 5, 2)).reshape(
        DEC_BATCH, N_DIFF, D_MODEL, PAST_LEN)
    cache_v = cache_diff_v.reshape(DEC_BATCH, N_DIFF, PAST_LEN, D_MODEL)

    diff_stacks, mla_stacks = None, None
    for l in range(DEPTH):
        j = l // 2
        mod_l = mod[l]
        x, w_in, w_out = _ffn(xs, mod_l, norm_g[l, 0], w_in, w_out, 0,
                              nxt=(ffn_w_in, ffn_w_out, l, 1))
        g1 = norm_g[l, 1]
        if l % 2 == 0:
            w_qkv = diff_w_qkv[j].astype(BF16)
            qg = jnp.tile(diff_q_norm[j] * DIFF_Q_SCALE, 2).reshape(1, LANES)
            kg = jnp.tile(diff_k_norm[j], 2).reshape(1, LANES)
            sub_g = diff_subln[j].reshape(1, LANES)
            lam_init = 0.8 - 0.6 * math.exp(-0.3 * l)
            q_c, kt_stack, v_stack = _diff_pre(x, mod_l, g1, w_qkv, qg, kg, None, lat=False,
                                               j=j, stacks=diff_stacks)
            diff_stacks = (kt_stack, v_stack)
            q_l, kt_l, v_l = _diff_pre(x, mod_l, g1, w_qkv, qg, kg, diff_tabs, lat=True)
            o_c = _diff_attn(q_c.reshape(BATCH, SEQ, D_MODEL), kt_stack, v_stack, diff_lambda[j],
                             sub_g, lam_init, heads_per_step=DIFF_HEADS, j=j)
            o_l = _diff_attn(q_l.reshape(DEC_BATCH, DEC_SEQ, D_MODEL), kt_l,
                             v_l.reshape(DEC_BATCH, DEC_SEQ, D_MODEL), diff_lambda[j], sub_g,
                             lam_init, heads_per_step=LAT_HEADS_PER_STEP, j=j,
                             caches=(cache_kt, cache_v))
            w_o = diff_w_o[j].astype(BF16)
        else:
            wd = mla_w_down[j]
            w_dq = wd[:, :Q_RANK].astype(BF16)
            w_dkv = jnp.pad(wd[:, Q_RANK:], ((0, 0), (0, LANES - MLA_ROPE))).astype(BF16)
            w_q = jnp.pad(mla_w_q_up[j].reshape(Q_RANK, MLA_HEADS, MLA_QK),
                          ((0, 0), (0, 0), (0, LANES - MLA_QK))).reshape(Q_RANK, MLA_HEADS * LANES).astype(BF16)
            wkv = mla_w_kv_up[j].reshape(KV_RANK, MLA_HEADS, MLA_NOPE + MLA_VD)
            w_k = jnp.pad(wkv[:, :, :MLA_NOPE], ((0, 0), (0, 0), (0, LANES - MLA_NOPE))
                          ).reshape(KV_RANK, MLA_HEADS * LANES).astype(BF16)
            w_v = wkv[:, :, MLA_NOPE:].reshape(KV_RANK, MLA_HEADS * MLA_VD).astype(BF16)
            qg = jnp.pad(mla_q_norm[j] * MLA_Q_SCALE, (0, LANES - MLA_QK)).reshape(1, LANES)
            kg = jnp.pad(mla_k_norm[j], (0, LANES - MLA_QK)).reshape(1, LANES)
            pre = functools.partial(_mla_pre, x, mod_l, g1, w_dq, w_dkv, mla_q_a_norm[j],
                                    mla_kv_a_norm[j], w_q, qg)
            q_c, k_c, v_c, ckv_stack, kpet_stack = pre(lat=False, kv_weights=(w_k, w_v, kg), j=j,
                                                      stacks=mla_stacks)
            mla_stacks = (ckv_stack, kpet_stack)
            q_l, ckv_l, kpe_l = pre(lat=True, tables=mla_tabs)
            cache_kpe = jnp.pad(cache_mla_kpe[:, j], ((0, 0), (0, 0), (0, LANES - MLA_ROPE)))
            ckv_rows = jnp.concatenate([cache_mla_ckv[:, j], ckv_l.reshape(DEC_BATCH, DEC_SEQ, KV_RANK)],
                                       axis=1).reshape(DEC_BATCH * LAT_KV, KV_RANK)
            kpe_rows = jnp.concatenate([cache_kpe, kpe_l.reshape(DEC_BATCH, DEC_SEQ, LANES)],
                                       axis=1).reshape(DEC_BATCH * LAT_KV, LANES)
            k_a, v_a = _mla_kv_lat(ckv_rows, kpe_rows, w_k, w_v, kg, mla_kv_tabs)
            kw = MLA_HEADS * LANES
            o_c = _mla_attn(q_c.reshape(BATCH, SEQ, kw), k_c.reshape(BATCH, SEQ, kw),
                            v_c.reshape(BATCH, SEQ, D_MODEL), pairs_per_step=MLA_HEADS // 2)
            o_l = _mla_attn(q_l.reshape(DEC_BATCH, DEC_SEQ, kw), k_a.reshape(DEC_BATCH, LAT_KV, kw),
                            v_a.reshape(DEC_BATCH, LAT_KV, D_MODEL),
                            pairs_per_step=LAT_HEADS_PER_STEP)
            w_o = mla_w_o[j].astype(BF16)
        os = (o_c.reshape(N_CTX, D_MODEL), o_l.reshape(N_LAT, D_MODEL))
        last = l == DEPTH - 1
        xs = _ffn((x,), mod_l, norm_g[l, 2], w_in, w_out, 1, os=os, w_o=w_o, split_out=last,
                  nxt=None if last else (ffn_w_in, ffn_w_out, l + 1, 0))
        if not last:
            xs, (w_in, w_out) = xs[:1], xs[1:]

    kt_stack, v_stack = diff_stacks
    ckv_stack, kpet_stack = mla_stacks
    new_diff_k = jnp.transpose(kt_stack.reshape(BATCH, N_DIFF, DIFF_HEADS, 2, DIFF_HD, SEQ),
                               (0, 1, 5, 2, 3, 4))
    new_diff_v = v_stack.reshape(BATCH, N_DIFF, SEQ, DIFF_HEADS, DIFF_VD)
    new_mla_kpe = jnp.transpose(kpet_stack, (0, 1, 3, 2))
    return (xs[0].reshape(BATCH, SEQ, D_MODEL), xs[1].reshape(DEC_BATCH, DEC_SEQ, D_MODEL),
            new_diff_k, new_diff_v, ckv_stack, new_mla_kpe)
```

```python
import functools
import math

import jax
import jax.numpy as jnp
import numpy as np
from jax import lax
from jax.experimental import pallas as pl
from jax.experimental.pallas import tpu as pltpu

D_MODEL = 1024
BATCH = 32
SEQ = 256
DEPTH = 4
DEC_BATCH = 2
DEC_SEQ = 1024
PAST_LEN = 256
GRID_W = 64
N_DIFF = (DEPTH + 1) // 2
N_MLA = DEPTH // 2
N_MOD = 9
D_FF = 2816
DIFF_HEADS = 8
DIFF_HD = 64
DIFF_VD = 128
MLA_HEADS = 16
MLA_NOPE = 64
MLA_ROPE = 32
MLA_QK = MLA_NOPE + MLA_ROPE
MLA_VD = 64
Q_RANK = 768
KV_RANK = 256
ROPE_THETA = 10000.0
EPS = 1e-6
LOG2E = math.log2(math.e)
DIFF_Q_SCALE = DIFF_HD ** -0.5 * LOG2E
MLA_Q_SCALE = MLA_QK ** -0.5 * LOG2E

N_CTX = BATCH * SEQ
N_LAT = DEC_BATCH * DEC_SEQ
N_TOK = N_CTX + N_LAT
LAT_KV = PAST_LEN + DEC_SEQ

LANES = 128
COND_ROWS = 8
TM = 512
SEQS_PER_TILE = TM // SEQ
FF_CHUNK = 256
TQ = 256
LAT_TQ = 512
LAT_HEADS_PER_STEP = 8
CTX_SEQS_PER_STEP = 4
KVP_TM_LAT = 640
MOD_TN = 2304
VMEM_LIMIT = 56 * 1024 * 1024

F32 = jnp.float32
BF16 = jnp.bfloat16


def _params(*sem):
    return pltpu.CompilerParams(dimension_semantics=sem, vmem_limit_bytes=VMEM_LIMIT)


def _dot(a, b):
    return jnp.dot(a, b, preferred_element_type=F32)


def _dot_nt(a, b):
    return lax.dot_general(a, b, (((1,), (1,)), ((), ())), preferred_element_type=F32)


def _modulate(x, g, shift, scale):
    ms = jnp.mean(x * x, axis=-1, keepdims=True)
    return (x * lax.rsqrt(ms + EPS) * g) * (1.0 + scale) + shift


def _rope(x, cos, s_up, s_dn, shift):
    return (x * cos + pltpu.roll(x, LANES - shift, 1) * s_up
            + pltpu.roll(x, shift, 1) * s_dn)


def _cond_of_tile(i, first_lat_tile, tiles_per_lat_batch):
    lat = jnp.maximum(i - first_lat_tile, 0) // tiles_per_lat_batch
    return jnp.where(i < first_lat_tile, 0, 1 + lat)


def _mod_kernel(c_ref, w_ref, b_ref, o_ref):
    c = c_ref[...]
    s = (c * jax.nn.sigmoid(c)).astype(BF16)
    o_ref[...] = _dot(s, w_ref[...].astype(BF16)) + b_ref[...]


def _modulation(conds, w_mod, b_mod):
    n_out = N_MOD * D_MODEL
    out = pl.pallas_call(
        _mod_kernel,
        grid=(DEPTH, n_out // MOD_TN),
        in_specs=[
            pl.BlockSpec((COND_ROWS, D_MODEL), lambda l, n: (0, 0)),
            pl.BlockSpec((None, D_MODEL, MOD_TN), lambda l, n: (l, 0, n)),
            pl.BlockSpec((None, 1, MOD_TN), lambda l, n: (l, 0, n)),
        ],
        out_specs=pl.BlockSpec((None, COND_ROWS, MOD_TN), lambda l, n: (l, 0, n)),
        out_shape=jax.ShapeDtypeStruct((DEPTH, COND_ROWS, n_out), F32),
        compiler_params=_params("parallel", "parallel"),
        name="modulation",
    )(conds, w_mod, b_mod.reshape(DEPTH, 1, n_out))
    return out.reshape(DEPTH, COND_ROWS, N_MOD, D_MODEL)


FIRST_LAT_TILE = N_CTX // TM
TILES_PER_LAT_BATCH = DEC_SEQ // TM


def _ctx_tile(i):
    return (jnp.minimum(i, FIRST_LAT_TILE - 1), 0)


def _lat_tile(i):
    return (jnp.maximum(i - FIRST_LAT_TILE, 0), 0)


def _pick_rows(i, ctx_ref, lat_ref):
    rows = i * TM + lax.broadcasted_iota(jnp.int32, (TM, 1), 0)
    return jnp.where(rows < N_CTX, ctx_ref[...], lat_ref[...])


def _ffn_kernel(*refs, n_x, proj, n_out, convert, i_shift, i_scale, i_gate, i_pgate):
    refs = list(refs)
    x_refs = [refs.pop(0) for _ in range(n_x)]
    if proj:
        oc_ref, ol_ref, wo_ref = refs[:3]
        refs = refs[3:]
    mod_ref, g_ref, win_ref, wout_ref = refs[:4]
    refs = refs[4:]
    if convert:
        nin_ref, nout_ref = refs[:2]
        refs = refs[2:]
    out_refs = refs[:n_out]
    refs = refs[n_out:]
    if convert:
        nin_bf_ref, nout_bf_ref = refs[:2]
        refs = refs[2:]
    a_ref = refs[0]
    i = pl.program_id(0)
    if convert:
        nin_bf_ref[...] = nin_ref[...].astype(BF16)
        nout_bf_ref[...] = nout_ref[...].astype(BF16)
    x = x_refs[0][...] if n_x == 1 else _pick_rows(i, *x_refs)
    if proj:
        o = _pick_rows(i, oc_ref, ol_ref)
        x = x + mod_ref[i_pgate:i_pgate + 1, :] * _dot(o, wo_ref[...])
    h = _modulate(x, g_ref[...], mod_ref[i_shift:i_shift + 1, :],
                  mod_ref[i_scale:i_scale + 1, :]).astype(BF16)
    for c in range(D_FF // FF_CHUNK):
        lo = c * FF_CHUNK
        g = _dot(h, win_ref[:, lo:lo + FF_CHUNK])
        u = _dot(h, win_ref[:, D_FF + lo:D_FF + lo + FF_CHUNK])
        a_ref[:, lo:lo + FF_CHUNK] = ((g * jax.nn.sigmoid(g)) * u).astype(BF16)
    ff = _dot(a_ref[...], wout_ref[...])
    y = x + mod_ref[i_gate:i_gate + 1, :] * (0.5 * ff)
    if n_out == 1:
        out_refs[0][...] = y
    else:
        @pl.when(i < FIRST_LAT_TILE)
        def _():
            out_refs[0][...] = y

        @pl.when(i >= FIRST_LAT_TILE)
        def _():
            out_refs[1][...] = y


def _ffn(xs, mod_l, g, w_in, w_out, which, os=None, w_o=None, split_out=False, nxt=None):
    proj = os is not None
    convert = nxt is not None
    base = 0 if which == 0 else 6
    row = lambda i: (i, 0)
    full = lambda i: (0, 0)
    tile = (TM, D_MODEL)
    if len(xs) == 1:
        in_specs = [pl.BlockSpec(tile, row)]
    else:
        in_specs = [pl.BlockSpec(tile, _ctx_tile), pl.BlockSpec(tile, _lat_tile)]
    args = list(xs)
    if proj:
        in_specs += [pl.BlockSpec(tile, _ctx_tile), pl.BlockSpec(tile, _lat_tile),
                     pl.BlockSpec((D_MODEL, D_MODEL), full)]
        args += [os[0], os[1], w_o]
    in_specs += [
        pl.BlockSpec((None, N_MOD, D_MODEL),
                     lambda i: (_cond_of_tile(i, FIRST_LAT_TILE, TILES_PER_LAT_BATCH), 0, 0)),
        pl.BlockSpec((1, D_MODEL), full),
        pl.BlockSpec((D_MODEL, 2 * D_FF), full),
        pl.BlockSpec((D_FF, D_MODEL), full),
    ]
    args += [mod_l, g.reshape(1, D_MODEL), w_in, w_out]
    in_slab = (D_MODEL // FIRST_LAT_TILE, 2 * D_FF)
    out_slab = (D_FF // FIRST_LAT_TILE, D_MODEL)
    if convert:
        nw_in, nw_out, nl, nw = nxt
        slab = lambda i: (nl, nw, jnp.minimum(i, FIRST_LAT_TILE - 1), 0)
        in_specs += [pl.BlockSpec((None, None) + in_slab, slab),
                     pl.BlockSpec((None, None) + out_slab, slab)]
        args += [nw_in, nw_out]
    if split_out:
        out_specs = [pl.BlockSpec(tile, _ctx_tile), pl.BlockSpec(tile, _lat_tile)]
        out_shape = [jax.ShapeDtypeStruct((N_CTX, D_MODEL), F32),
                     jax.ShapeDtypeStruct((N_LAT, D_MODEL), F32)]
    else:
        out_specs = [pl.BlockSpec(tile, row)]
        out_shape = [jax.ShapeDtypeStruct((N_TOK, D_MODEL), F32)]
    n_out = len(out_shape)
    if convert:
        out_specs += [pl.BlockSpec(in_slab, _ctx_tile), pl.BlockSpec(out_slab, _ctx_tile)]
        out_shape += [jax.ShapeDtypeStruct((D_MODEL, 2 * D_FF), BF16),
                      jax.ShapeDtypeStruct((D_FF, D_MODEL), BF16)]
    kern = functools.partial(_ffn_kernel, n_x=len(xs), proj=proj, n_out=n_out, convert=convert,
                             i_shift=base, i_scale=base + 1, i_gate=base + 2, i_pgate=5)
    return pl.pallas_call(
        kern,
        grid=(N_TOK // TM,),
        in_specs=in_specs,
        out_specs=out_specs,
        out_shape=out_shape,
        scratch_shapes=[pltpu.VMEM((TM, D_FF), BF16)],
        compiler_params=_params("arbitrary"),
        name="ffn_proj" if proj else "ffn",
    )(*args)


def _diff_pre_kernel(*refs, rope, n_alias):
    refs = list(refs)
    x_ref, mod_ref, g_ref, w_ref, qg_ref, kg_ref = refs[:6]
    refs = refs[6:]
    if rope:
        cos_ref, up_ref, dn_ref = refs[:3]
        refs = refs[3:]
    q_ref, kt_ref, v_ref = refs[n_alias:]
    h = _modulate(x_ref[...], g_ref[...], mod_ref[3:4, :], mod_ref[4:5, :]).astype(BF16)
    hw = DIFF_HEADS * DIFF_VD
    lane = lax.broadcasted_iota(jnp.int32, (1, LANES), 1)
    lo = lane < DIFF_HD
    for part, gain_ref in ((1, kg_ref), (0, qg_ref)):
        gain = gain_ref[...]
        y = _dot(h, w_ref[:, part * hw:(part + 1) * hw])
        for hd in range(DIFF_HEADS):
            sl = slice(hd * LANES, (hd + 1) * LANES)
            yh = y[:, sl]
            sq = yh * yh
            s_lo = jnp.sum(jnp.where(lo, sq, 0.0), axis=-1, keepdims=True)
            s_hi = jnp.sum(jnp.where(lo, 0.0, sq), axis=-1, keepdims=True)
            ms = jnp.where(lo, s_lo, s_hi) * (1.0 / DIFF_HD)
            yn = yh * lax.rsqrt(ms + EPS) * gain
            if rope:
                yn = _rope(yn, cos_ref[...], up_ref[...], dn_ref[...], DIFF_HD // 4)
            if part == 0:
                q_ref[:, sl] = yn.astype(q_ref.dtype)
            elif len(kt_ref.shape) == 2:
                kt_ref[sl, :] = yn.T
            else:
                ynt = yn.T
                for b in range(SEQS_PER_TILE):
                    kt_ref[b, sl, :] = ynt[:, b * SEQ:(b + 1) * SEQ]
    v = _dot(h, w_ref[:, 2 * hw:3 * hw])
    if len(v_ref.shape) == 2:
        v_ref[...] = v
    else:
        for b in range(SEQS_PER_TILE):
            v_ref[b] = v[b * SEQ:(b + 1) * SEQ]


def _diff_pre(x, mod_l, g, w_qkv, qg, kg, tables, lat, j=0, stacks=None):
    n_rows = N_LAT if lat else N_CTX
    tile0 = N_CTX // TM if lat else 0
    per_lat = DEC_SEQ // TM
    full = lambda i: (0, 0)
    xrow = lambda i: (i + tile0, 0)
    row = lambda i: (i, 0)
    if lat:
        cond = lambda i: (1 + i // per_lat, 0, 0)
    else:
        cond = lambda i: (0, 0, 0)
    in_specs = [
        pl.BlockSpec((TM, D_MODEL), xrow),
        pl.BlockSpec((None, N_MOD, D_MODEL), cond),
        pl.BlockSpec((1, D_MODEL), full),
        pl.BlockSpec((D_MODEL, 3 * D_MODEL), full),
        pl.BlockSpec((1, LANES), full),
        pl.BlockSpec((1, LANES), full),
    ]
    args = [x, mod_l, g.reshape(1, D_MODEL), w_qkv, qg, kg]
    aliases = {}
    if lat:
        pos = lambda i: (i % per_lat, 0)
        in_specs += [pl.BlockSpec((TM, LANES), pos)] * 3
        args += list(tables)
        out_specs = [pl.BlockSpec((TM, D_MODEL), row),
                     pl.BlockSpec((None, D_MODEL, TM), lambda i: (i // per_lat, 0, i % per_lat)),
                     pl.BlockSpec((TM, D_MODEL), row)]
        out_shape = [jax.ShapeDtypeStruct((N_LAT, D_MODEL), BF16),
                     jax.ShapeDtypeStruct((DEC_BATCH, D_MODEL, DEC_SEQ), F32),
                     jax.ShapeDtypeStruct((N_LAT, D_MODEL), F32)]
    else:
        slot = lambda i: (i, j, 0, 0)
        out_specs = [pl.BlockSpec((TM, D_MODEL), row),
                     pl.BlockSpec((SEQS_PER_TILE, None, D_MODEL, SEQ), slot),
                     pl.BlockSpec((SEQS_PER_TILE, None, SEQ, D_MODEL), slot)]
        out_shape = [jax.ShapeDtypeStruct((N_CTX, D_MODEL), BF16),
                     jax.ShapeDtypeStruct((BATCH, N_DIFF, D_MODEL, SEQ), F32),
                     jax.ShapeDtypeStruct((BATCH, N_DIFF, SEQ, D_MODEL), F32)]
        if stacks is not None:
            aliases = {len(args): 1, len(args) + 1: 2}
            in_specs += [pl.BlockSpec(memory_space=pl.ANY)] * 2
            args += list(stacks)
    return pl.pallas_call(
        functools.partial(_diff_pre_kernel, rope=lat, n_alias=len(aliases)),
        grid=(n_rows // TM,),
        in_specs=in_specs,
        out_specs=out_specs,
        out_shape=out_shape,
        input_output_aliases=aliases,
        compiler_params=_params("parallel"),
        name="diff_pre_lat" if lat else "diff_pre_ctx",
    )(*args)


def _softmax_pv(s, v_aug):
    e = jnp.exp2(s - jnp.max(s, axis=-1, keepdims=True)).astype(BF16)
    r = _dot(e, v_aug)
    return r[:, :LANES] * (1.0 / r[:, LANES:])


def _diff_attn_kernel(*refs, heads, lam_init, cached):
    if cached:
        q_ref, ktc_ref, kt_ref, vc_ref, v_ref, lam_ref, sub_ref, o_ref = refs
    else:
        q_ref, kt_ref, v_ref, lam_ref, sub_ref, o_ref = refs
    lp = lam_ref[...]
    lam = (jnp.exp(jnp.sum(lp[0:1, :] * lp[1:2, :], axis=-1, keepdims=True))
           - jnp.exp(jnp.sum(lp[2:3, :] * lp[3:4, :], axis=-1, keepdims=True)) + lam_init)
    n_b, tq = q_ref.shape[:2]
    lane = lax.broadcasted_iota(jnp.int32, (1, LANES), 1)
    lo = lane < DIFF_HD
    sub_g = sub_ref[...]
    items = [(b, hd) for b in range(n_b) for hd in range(heads)]

    def scores(item):
        b, hd = item
        sl = slice(hd * LANES, (hd + 1) * LANES)
        qh = q_ref[b, :, sl]
        kt = kt_ref[b, sl, :].astype(BF16)
        if cached:
            kt = jnp.concatenate([ktc_ref[b, sl, :].astype(BF16), kt], axis=1)
        zero = jnp.zeros_like(qh)
        qq = jnp.concatenate([jnp.where(lo, qh, zero), jnp.where(lo, zero, qh)], axis=0)
        return _dot(qq, kt)

    s_next = scores(items[0])
    for n, (b, hd) in enumerate(items):
        sl = slice(hd * LANES, (hd + 1) * LANES)
        s = s_next
        if n + 1 < len(items):
            s_next = scores(items[n + 1])
        vh = v_ref[b, :, sl].astype(BF16)
        if cached:
            vh = jnp.concatenate([vc_ref[b, :, sl].astype(BF16), vh], axis=0)
        v_aug = jnp.concatenate([vh, jnp.ones_like(vh)], axis=1)
        o12 = _softmax_pv(s, v_aug)
        o = o12[:tq] - lam * o12[tq:]
        ms = jnp.mean(o * o, axis=-1, keepdims=True)
        o = (o * lax.rsqrt(ms + EPS) * sub_g) * (1.0 - lam_init)
        o_ref[b, :, sl] = o.astype(o_ref.dtype)


def _diff_attn(q, kt, v, lam_p, sub_g, lam_init, heads_per_step, j, caches=None):
    b, lq, _ = q.shape
    cached = caches is not None
    nb = 1 if cached else CTX_SEQS_PER_STEP
    w = heads_per_step * LANES
    n_hg = DIFF_HEADS // heads_per_step
    tq = min(TQ, lq) if lq == SEQ else LAT_TQ
    qmap = lambda bi, hi, qi: (bi, qi, hi)
    full = lambda bi, hi, qi: (0, 0)
    q_spec = pl.BlockSpec((nb, tq, w), qmap)
    kt_slot = lambda bi, hi, qi: (bi, j, hi, 0)
    v_slot = lambda bi, hi, qi: (bi, j, 0, hi)
    if cached:
        kt_cache, v_cache = caches
        in_specs = [q_spec,
                    pl.BlockSpec((nb, None, w, PAST_LEN), kt_slot),
                    pl.BlockSpec((nb, w, lq), lambda bi, hi, qi: (bi, hi, 0)),
                    pl.BlockSpec((nb, None, PAST_LEN, w), v_slot),
                    pl.BlockSpec((nb, lq, w), lambda bi, hi, qi: (bi, 0, hi))]
        args = [q, kt_cache, kt, v_cache, v]
    else:
        in_specs = [q_spec,
                    pl.BlockSpec((nb, None, w, lq), kt_slot),
                    pl.BlockSpec((nb, None, lq, w), v_slot)]
        args = [q, kt, v]
    in_specs += [pl.BlockSpec((4, DIFF_HD), full), pl.BlockSpec((1, LANES), full)]
    args += [lam_p, sub_g]
    return pl.pallas_call(
        functools.partial(_diff_attn_kernel, heads=heads_per_step, lam_init=lam_init, cached=cached),
        grid=(b // nb, n_hg, lq // tq),
        in_specs=in_specs,
        out_specs=pl.BlockSpec((nb, tq, w), qmap),
        out_shape=jax.ShapeDtypeStruct((b, lq, D_MODEL), BF16),
        compiler_params=_params("parallel", "parallel", "parallel"),
        name="diff_attn_lat" if cached else "diff_attn_ctx",
    )(*args)


def _mla_keys(ckv_n, kpe, wk_ref, wv_ref, kg, rope_refs, k_ref, v_ref):
    c = ckv_n.astype(BF16)
    kk = _dot(c, wk_ref[...])
    kpe = pltpu.roll(kpe, MLA_NOPE, 1)
    pe_ss = jnp.sum(kpe * kpe, axis=-1, keepdims=True)
    kpe_g = kpe * kg
    if rope_refs is not None:
        cos_ref, up_ref, dn_ref = rope_refs
        kpe_g = _rope(kpe_g, cos_ref[...], up_ref[...], dn_ref[...], MLA_ROPE // 4)
    for hd in range(MLA_HEADS):
        sl = slice(hd * LANES, (hd + 1) * LANES)
        kh = kk[:, sl]
        ms = (jnp.sum(kh * kh, axis=-1, keepdims=True) + pe_ss) * (1.0 / MLA_QK)
        kn = (kh * kg + kpe_g) * lax.rsqrt(ms + EPS)
        k_ref[:, sl] = kn.astype(k_ref.dtype)
    v_ref[...] = _dot(c, wv_ref[...]).astype(v_ref.dtype)


def _mla_pre_kernel(*refs, lat, n_alias):
    refs = list(refs)
    x_ref, mod_ref, g_ref, wdq_ref, wdkv_ref, qag_ref, kvag_ref, wq_ref, qg_ref = refs[:9]
    refs = refs[9:]
    if lat:
        cos_ref, up_ref, dn_ref = refs[:3]
        q_ref, ckv_ref, kpe_ref = refs[3:]
    else:
        wk_ref, wv_ref, kg_ref = refs[:3]
        q_ref, k_ref, v_ref, ckvs_ref, kpet_ref = refs[3 + n_alias:]
    h = _modulate(x_ref[...], g_ref[...], mod_ref[3:4, :], mod_ref[4:5, :]).astype(BF16)
    d2 = _dot(h, wdkv_ref[...])
    ckv = d2[:, :KV_RANK]
    ms = jnp.mean(ckv * ckv, axis=-1, keepdims=True)
    ckv_n = ckv * lax.rsqrt(ms + EPS) * kvag_ref[...]
    kpe = d2[:, KV_RANK:]
    if lat:
        ckv_ref[...] = ckv_n
        kpe_ref[...] = kpe
    cq = _dot(h, wdq_ref[...])
    ms = jnp.mean(cq * cq, axis=-1, keepdims=True)
    cqn = (cq * lax.rsqrt(ms + EPS) * qag_ref[...]).astype(BF16)
    q = _dot(cqn, wq_ref[...])
    qg = qg_ref[...]
    for hd in range(MLA_HEADS):
        sl = slice(hd * LANES, (hd + 1) * LANES)
        qh = q[:, sl]
        ms = jnp.sum(qh * qh, axis=-1, keepdims=True) * (1.0 / MLA_QK)
        qn = qh * lax.rsqrt(ms + EPS) * qg
        if lat:
            qn = _rope(qn, cos_ref[...], up_ref[...], dn_ref[...], MLA_ROPE // 4)
        q_ref[:, sl] = qn.astype(q_ref.dtype)
    if not lat:
        kpe_t = kpe.T
        for b in range(SEQS_PER_TILE):
            ckvs_ref[b] = ckv_n[b * SEQ:(b + 1) * SEQ]
            kpet_ref[b] = kpe_t[:MLA_ROPE, b * SEQ:(b + 1) * SEQ]
        _mla_keys(ckv_n, kpe, wk_ref, wv_ref, kg_ref[...], None, k_ref, v_ref)


def _mla_pre(x, mod_l, g, w_dq, w_dkv, qag, kvag, w_q, qg, lat, tables=None, kv_weights=None,
             j=0, stacks=None):
    n_rows = N_LAT if lat else N_CTX
    tile0 = N_CTX // TM if lat else 0
    per_lat = DEC_SEQ // TM
    full = lambda i: (0, 0)
    xrow = lambda i: (i + tile0, 0)
    row = lambda i: (i, 0)
    if lat:
        cond = lambda i: (1 + i // per_lat, 0, 0)
    else:
        cond = lambda i: (0, 0, 0)
    kvw = KV_RANK + LANES
    qw = MLA_HEADS * LANES
    vw = MLA_HEADS * MLA_VD
    in_specs = [
        pl.BlockSpec((TM, D_MODEL), xrow),
        pl.BlockSpec((None, N_MOD, D_MODEL), cond),
        pl.BlockSpec((1, D_MODEL), full),
        pl.BlockSpec((D_MODEL, Q_RANK), full),
        pl.BlockSpec((D_MODEL, kvw), full),
        pl.BlockSpec((1, Q_RANK), full),
        pl.BlockSpec((1, KV_RANK), full),
        pl.BlockSpec((Q_RANK, qw), full),
        pl.BlockSpec((1, LANES), full),
    ]
    args = [x, mod_l, g.reshape(1, D_MODEL), w_dq, w_dkv, qag.reshape(1, Q_RANK),
            kvag.reshape(1, KV_RANK), w_q, qg]
    aliases = {}
    if lat:
        pos = lambda i: (i % per_lat, 0)
        in_specs += [pl.BlockSpec((TM, LANES), pos)] * 3
        args += list(tables)
        out_specs = [pl.BlockSpec((TM, qw), row), pl.BlockSpec((TM, KV_RANK), row),
                     pl.BlockSpec((TM, LANES), row)]
        out_shape = [jax.ShapeDtypeStruct((n_rows, qw), BF16),
                     jax.ShapeDtypeStruct((n_rows, KV_RANK), F32),
                     jax.ShapeDtypeStruct((n_rows, LANES), F32)]
    else:
        w_k, w_v, kg = kv_weights
        in_specs += [pl.BlockSpec((KV_RANK, qw), full), pl.BlockSpec((KV_RANK, vw), full),
                     pl.BlockSpec((1, LANES), full)]
        args += [w_k, w_v, kg]
        slot = lambda i: (i, j, 0, 0)
        out_specs = [pl.BlockSpec((TM, qw), row), pl.BlockSpec((TM, qw), row),
                     pl.BlockSpec((TM, vw), row),
                     pl.BlockSpec((SEQS_PER_TILE, None, SEQ, KV_RANK), slot),
                     pl.BlockSpec((SEQS_PER_TILE, None, MLA_ROPE, SEQ), slot)]
        out_shape = [jax.ShapeDtypeStruct((n_rows, qw), BF16),
                     jax.ShapeDtypeStruct((n_rows, qw), BF16),
                     jax.ShapeDtypeStruct((n_rows, vw), BF16),
                     jax.ShapeDtypeStruct((BATCH, N_MLA, SEQ, KV_RANK), F32),
                     jax.ShapeDtypeStruct((BATCH, N_MLA, MLA_ROPE, SEQ), F32)]
        if stacks is not None:
            aliases = {len(args): 3, len(args) + 1: 4}
            in_specs += [pl.BlockSpec(memory_space=pl.ANY)] * 2
            args += list(stacks)
    return pl.pallas_call(
        functools.partial(_mla_pre_kernel, lat=lat, n_alias=len(aliases)),
        grid=(n_rows // TM,),
        in_specs=in_specs,
        out_specs=out_specs,
        out_shape=out_shape,
        input_output_aliases=aliases,
        compiler_params=_params("parallel"),
        name="mla_pre_lat" if lat else "mla_pre_ctx",
    )(*args)


def _mla_kv_kernel(ckv_ref, kpe_ref, wk_ref, wv_ref, kg_ref, cos_ref, up_ref, dn_ref, k_ref, v_ref):
    _mla_keys(ckv_ref[...], kpe_ref[...], wk_ref, wv_ref, kg_ref[...],
              (cos_ref, up_ref, dn_ref), k_ref, v_ref)


def _mla_kv_lat(ckv_rows, kpe_rows, w_k, w_v, kg, tables):
    n_rows = ckv_rows.shape[0]
    tm = KVP_TM_LAT
    per = LAT_KV // tm
    full = lambda i: (0, 0)
    row = lambda i: (i, 0)
    pos = lambda i: (i % per, 0)
    kw = MLA_HEADS * LANES
    vw = MLA_HEADS * MLA_VD
    return pl.pallas_call(
        _mla_kv_kernel,
        grid=(n_rows // tm,),
        in_specs=[
            pl.BlockSpec((tm, KV_RANK), row),
            pl.BlockSpec((tm, LANES), row),
            pl.BlockSpec((KV_RANK, kw), full),
            pl.BlockSpec((KV_RANK, vw), full),
            pl.BlockSpec((1, LANES), full),
        ] + [pl.BlockSpec((tm, LANES), pos)] * 3,
        out_specs=[pl.BlockSpec((tm, kw), row), pl.BlockSpec((tm, vw), row)],
        out_shape=[jax.ShapeDtypeStruct((n_rows, kw), BF16),
                   jax.ShapeDtypeStruct((n_rows, vw), BF16)],
        compiler_params=_params("parallel"),
        name="mla_kv_lat",
    )(ckv_rows, kpe_rows, w_k, w_v, kg, *tables)


def _mla_attn_kernel(q_ref, k_ref, v_ref, o_ref, *, pairs):
    lane = lax.broadcasted_iota(jnp.int32, (1, LANES), 1)
    lo = lane < MLA_VD

    items = [(b, hd) for b in range(q_ref.shape[0]) for hd in range(2 * pairs)]

    def scores(item):
        b, hd = item
        sl = slice(hd * LANES, (hd + 1) * LANES)
        return _dot_nt(q_ref[b, :, sl], k_ref[b, :, sl])

    s_next = scores(items[0])
    outs = []
    for n, (b, hd) in enumerate(items):
        s = s_next
        if n + 1 < len(items):
            s_next = scores(items[n + 1])
        pl_ = slice((hd // 2) * LANES, (hd // 2 + 1) * LANES)
        vp = v_ref[b, :, pl_]
        outs.append(_softmax_pv(s, jnp.concatenate([vp, jnp.ones_like(vp)], axis=1)))
        if hd % 2 == 1:
            o_ref[b, :, pl_] = jnp.where(lo, outs[0], outs[1]).astype(o_ref.dtype)
            outs = []


def _mla_attn(q, k, v, pairs_per_step):
    b, lq, _ = q.shape
    lk = k.shape[1]
    nb = CTX_SEQS_PER_STEP if lq == SEQ else 1
    n_pg = MLA_HEADS // 2 // pairs_per_step
    tq = min(TQ, lq) if lq == SEQ else LAT_TQ
    qkw = pairs_per_step * 2 * LANES
    ow = pairs_per_step * LANES
    qmap = lambda bi, hi, qi: (bi, qi, hi)
    kmap = lambda bi, hi, qi: (bi, 0, hi)
    return pl.pallas_call(
        functools.partial(_mla_attn_kernel, pairs=pairs_per_step),
        grid=(b // nb, n_pg, lq // tq),
        in_specs=[
            pl.BlockSpec((nb, tq, qkw), qmap),
            pl.BlockSpec((nb, lk, qkw), kmap),
            pl.BlockSpec((nb, lk, ow), kmap),
        ],
        out_specs=pl.BlockSpec((nb, tq, ow), qmap),
        out_shape=jax.ShapeDtypeStruct((b, lq, D_MODEL), BF16),
        compiler_params=_params("parallel", "parallel", "parallel"),
        name="mla_attn_lat" if lq == DEC_SEQ else "mla_attn_ctx",
    )(q, k, v)


def _rope_tables(n_rot, lane0, identity_rows):
    rows = DEC_SEQ // GRID_W
    row = np.repeat(np.arange(rows), GRID_W)
    col = np.tile(np.arange(GRID_W), rows)
    n = n_rot // 2
    freqs = ROPE_THETA ** (-np.arange(0, n, 2, dtype=np.float64) / n)
    zeros = np.zeros((DEC_SEQ, n // 2))
    cos_parts, up_parts, dn_parts = [], [], []
    for pos in (row, col):
        ang = pos.astype(np.float64)[:, None] * freqs[None, :]
        c, s = np.cos(ang), np.sin(ang)
        cos_parts += [c, c]
        up_parts += [-s, zeros]
        dn_parts += [zeros, s]

    def place(parts, fill):
        reps = 2 if lane0 == 0 else 1
        body = np.concatenate(parts * reps, axis=1)
        full = np.full((identity_rows + DEC_SEQ, LANES), fill)
        full[identity_rows:, lane0:lane0 + body.shape[1]] = body
        return jnp.asarray(full, dtype=F32)

    return place(cos_parts, 1.0), place(up_parts, 0.0), place(dn_parts, 0.0)


def kernel(x_prompt, x_sample, c, cache_diff_k, cache_diff_v, cache_mla_ckv, cache_mla_kpe, c_ctx, w_mod, b_mod, norm_g, ffn_w_in, ffn_w_out, diff_w_qkv, diff_q_norm, diff_k_norm, diff_lambda, diff_subln, diff_w_o, mla_w_down, mla_q_a_norm, mla_kv_a_norm, mla_w_q_up, mla_w_kv_up, mla_q_norm, mla_k_norm, mla_w_o):
    xs = (x_prompt.reshape(N_CTX, D_MODEL), x_sample.reshape(N_LAT, D_MODEL))
    conds = jnp.concatenate(
        [c_ctx[None, :], c, jnp.zeros((COND_ROWS - 1 - DEC_BATCH, D_MODEL), F32)], axis=0)
    mod = _modulation(conds, w_mod, b_mod)

    w_in = ffn_w_in[0, 0].astype(BF16)
    w_out = ffn_w_out[0, 0].astype(BF16)
    diff_tabs = _rope_tables(DIFF_HD, 0, 0)
    mla_tabs = _rope_tables(MLA_ROPE, MLA_NOPE, 0)
    mla_kv_tabs = _rope_tables(MLA_ROPE, MLA_NOPE, PAST_LEN)
    cache_kt = jnp.transpose(cache_diff_k, (0, 1, 3, 4, 5, 2)).reshape(
        DEC_BATCH, N_DIFF, D_MODEL, PAST_LEN)
    cache_v = cache_diff_v.reshape(DEC_BATCH, N_DIFF, PAST_LEN, D_MODEL)

    diff_stacks, mla_stacks = None, None
    for l in range(DEPTH):
        j = l // 2
        mod_l = mod[l]
        x, w_in, w_out = _ffn(xs, mod_l, norm_g[l, 0], w_in, w_out, 0,
                              nxt=(ffn_w_in, ffn_w_out, l, 1))
        g1 = norm_g[l, 1]
        if l % 2 == 0:
            w_qkv = diff_w_qkv[j].astype(BF16)
            qg = jnp.tile(diff_q_norm[j] * DIFF_Q_SCALE, 2).reshape(1, LANES)
            kg = jnp.tile(diff_k_norm[j], 2).reshape(1, LANES)
            sub_g = diff_subln[j].reshape(1, LANES)
            lam_init = 0.8 - 0.6 * math.exp(-0.3 * l)
            q_c, kt_stack, v_stack = _diff_pre(x, mod_l, g1, w_qkv, qg, kg, None, lat=False,
                                               j=j, stacks=diff_stacks)
            diff_stacks = (kt_stack, v_stack)
            q_l, kt_l, v_l = _diff_pre(x, mod_l, g1, w_qkv, qg, kg, diff_tabs, lat=True)
            o_c = _diff_attn(q_c.reshape(BATCH, SEQ, D_MODEL), kt_stack, v_stack, diff_lambda[j],
                             sub_g, lam_init, heads_per_step=DIFF_HEADS, j=j)
            o_l = _diff_attn(q_l.reshape(DEC_BATCH, DEC_SEQ, D_MODEL), kt_l,
                             v_l.reshape(DEC_BATCH, DEC_SEQ, D_MODEL), diff_lambda[j], sub_g,
                             lam_init, heads_per_step=LAT_HEADS_PER_STEP, j=j,
                             caches=(cache_kt, cache_v))
            w_o = diff_w_o[j].astype(BF16)
        else:
            wd = mla_w_down[j]
            w_dq = wd[:, :Q_RANK].astype(BF16)
            w_dkv = jnp.pad(wd[:, Q_RANK:], ((0, 0), (0, LANES - MLA_ROPE))).astype(BF16)
            w_q = jnp.pad(mla_w_q_up[j].reshape(Q_RANK, MLA_HEADS, MLA_QK),
                          ((0, 0), (0, 0), (0, LANES - MLA_QK))).reshape(Q_RANK, MLA_HEADS * LANES).astype(BF16)
            wkv = mla_w_kv_up[j].reshape(KV_RANK, MLA_HEADS, MLA_NOPE + MLA_VD)
            w_k = jnp.pad(wkv[:, :, :MLA_NOPE], ((0, 0), (0, 0), (0, LANES - MLA_NOPE))
                          ).reshape(KV_RANK, MLA_HEADS * LANES).astype(BF16)
            w_v = wkv[:, :, MLA_NOPE:].reshape(KV_RANK, MLA_HEADS * MLA_VD).astype(BF16)
            qg = jnp.pad(mla_q_norm[j] * MLA_Q_SCALE, (0, LANES - MLA_QK)).reshape(1, LANES)
            kg = jnp.pad(mla_k_norm[j], (0, LANES - MLA_QK)).reshape(1, LANES)
            pre = functools.partial(_mla_pre, x, mod_l, g1, w_dq, w_dkv, mla_q_a_norm[j],
                                    mla_kv_a_norm[j], w_q, qg)
            q_c, k_c, v_c, ckv_stack, kpet_stack = pre(lat=False, kv_weights=(w_k, w_v, kg), j=j,
                                                      stacks=mla_stacks)
            mla_stacks = (ckv_stack, kpet_stack)
            q_l, ckv_l, kpe_l = pre(lat=True, tables=mla_tabs)
            cache_kpe = jnp.pad(cache_mla_kpe[:, j], ((0, 0), (0, 0), (0, LANES - MLA_ROPE)))
            ckv_rows = jnp.concatenate([cache_mla_ckv[:, j], ckv_l.reshape(DEC_BATCH, DEC_SEQ, KV_RANK)],
                                       axis=1).reshape(DEC_BATCH * LAT_KV, KV_RANK)
            kpe_rows = jnp.concatenate([cache_kpe, kpe_l.reshape(DEC_BATCH, DEC_SEQ, LANES)],
                                       axis=1).reshape(DEC_BATCH * LAT_KV, LANES)
            k_a, v_a = _mla_kv_lat(ckv_rows, kpe_rows, w_k, w_v, kg, mla_kv_tabs)
            kw = MLA_HEADS * LANES
            o_c = _mla_attn(q_c.reshape(BATCH, SEQ, kw), k_c.reshape(BATCH, SEQ, kw),
                            v_c.reshape(BATCH, SEQ, D_MODEL), pairs_per_step=MLA_HEADS // 2)
            o_l = _mla_attn(q_l.reshape(DEC_BATCH, DEC_SEQ, kw), k_a.reshape(DEC_BATCH, LAT_KV, kw),
                            v_a.reshape(DEC_BATCH, LAT_KV, D_MODEL),
                            pairs_per_step=LAT_HEADS_PER_STEP)
            w_o = mla_w_o[j].astype(BF16)
        os = (o_c.reshape(N_CTX, D_MODEL), o_l.reshape(N_LAT, D_MODEL))
        last = l == DEPTH - 1
        xs = _ffn((x,), mod_l, norm_g[l, 2], w_in, w_out, 1, os=os, w_o=w_o, split_out=last,
                  nxt=None if last else (ffn_w_in, ffn_w_out, l + 1, 0))
        if not last:
            xs, (w_in, w_out) = xs[:1], xs[1:]

    kt_stack, v_stack = diff_stacks
    ckv_stack, kpet_stack = mla_stacks
    new_diff_k = jnp.transpose(kt_stack.reshape(BATCH, N_DIFF, DIFF_HEADS, 2, DIFF_HD, SEQ),
                               (0, 1, 5, 2, 3, 4))
    new_diff_v = v_stack.reshape(BATCH, N_DIFF, SEQ, DIFF_HEADS, DIFF_VD)
    new_mla_kpe = jnp.transpose(kpet_stack, (0, 1, 3, 2))
    return (xs[0].reshape(BATCH, SEQ, D_MODEL), xs[1].reshape(DEC_BATCH, DEC_SEQ, D_MODEL),
            new_diff_k, new_diff_v, ckv_stack, new_mla_kpe)
```

```python
import functools
import math

import jax
import jax.numpy as jnp
import numpy as np
from jax import lax
from jax.experimental import pallas as pl
from jax.experimental.pallas import tpu as pltpu

D_MODEL = 1024
BATCH = 32
SEQ = 256
DEPTH = 4
DEC_BATCH = 2
DEC_SEQ = 1024
PAST_LEN = 256
GRID_W = 64
N_DIFF = (DEPTH + 1) // 2
N_MLA = DEPTH // 2
N_MOD = 9
D_FF = 2816
DIFF_HEADS = 8
DIFF_HD = 64
DIFF_VD = 128
MLA_HEADS = 16
MLA_NOPE = 64
MLA_ROPE = 32
MLA_QK = MLA_NOPE + MLA_ROPE
MLA_VD = 64
Q_RANK = 768
KV_RANK = 256
ROPE_THETA = 10000.0
EPS = 1e-6
LOG2E = math.log2(math.e)
DIFF_Q_SCALE = DIFF_HD ** -0.5 * LOG2E
MLA_Q_SCALE = MLA_QK ** -0.5 * LOG2E

N_CTX = BATCH * SEQ
N_LAT = DEC_BATCH * DEC_SEQ
N_TOK = N_CTX + N_LAT
LAT_KV = PAST_LEN + DEC_SEQ

LANES = 128
COND_ROWS = 8
TM = 512
SEQS_PER_TILE = TM // SEQ
FF_CHUNK = 256
TQ = 256
LAT_TQ = 512
LAT_HEADS_PER_STEP = 8
CTX_SEQS_PER_STEP = 4
KVP_TM_LAT = 640
MOD_TN = 2304
VMEM_LIMIT = 56 * 1024 * 1024

F32 = jnp.float32
BF16 = jnp.bfloat16


def _params(*sem):
    return pltpu.CompilerParams(dimension_semantics=sem, vmem_limit_bytes=VMEM_LIMIT)


def _dot(a, b):
    return jnp.dot(a, b, preferred_element_type=F32)


def _dot_nt(a, b):
    return lax.dot_general(a, b, (((1,), (1,)), ((), ())), preferred_element_type=F32)


def _modulate(x, g, shift, scale):
    ms = jnp.mean(x * x, axis=-1, keepdims=True)
    return (x * lax.rsqrt(ms + EPS) * g) * (1.0 + scale) + shift


def _rope(x, cos, s_up, s_dn, shift):
    return (x * cos + pltpu.roll(x, LANES - shift, 1) * s_up
            + pltpu.roll(x, shift, 1) * s_dn)


def _cond_of_tile(i, first_lat_tile, tiles_per_lat_batch):
    lat = jnp.maximum(i - first_lat_tile, 0) // tiles_per_lat_batch
    return jnp.where(i < first_lat_tile, 0, 1 + lat)


def _mod_kernel(c_ref, w_ref, b_ref, o_ref):
    c = c_ref[...]
    s = (c * jax.nn.sigmoid(c)).astype(BF16)
    o_ref[...] = _dot(s, w_ref[...].astype(BF16)) + b_ref[...]


def _modulation(conds, w_mod, b_mod):
    n_out = N_MOD * D_MODEL
    out = pl.pallas_call(
        _mod_kernel,
        grid=(DEPTH, n_out // MOD_TN),
        in_specs=[
            pl.BlockSpec((COND_ROWS, D_MODEL), lambda l, n: (0, 0)),
            pl.BlockSpec((None, D_MODEL, MOD_TN), lambda l, n: (l, 0, n)),
            pl.BlockSpec((None, 1, MOD_TN), lambda l, n: (l, 0, n)),
        ],
        out_specs=pl.BlockSpec((None, COND_ROWS, MOD_TN), lambda l, n: (l, 0, n)),
        out_shape=jax.ShapeDtypeStruct((DEPTH, COND_ROWS, n_out), F32),
        compiler_params=_params("parallel", "parallel"),
        name="modulation",
    )(conds, w_mod, b_mod.reshape(DEPTH, 1, n_out))
    return out.reshape(DEPTH, COND_ROWS, N_MOD, D_MODEL)


FIRST_LAT_TILE = N_CTX // TM
TILES_PER_LAT_BATCH = DEC_SEQ // TM


def _ctx_tile(i):
    return (jnp.minimum(i, FIRST_LAT_TILE - 1), 0)


def _lat_tile(i):
    return (jnp.maximum(i - FIRST_LAT_TILE, 0), 0)


def _pick_rows(i, ctx_ref, lat_ref):
    rows = i * TM + lax.broadcasted_iota(jnp.int32, (TM, 1), 0)
    return jnp.where(rows < N_CTX, ctx_ref[...], lat_ref[...])


def _ffn_kernel(*refs, n_x, proj, n_out, convert, i_shift, i_scale, i_gate, i_pgate):
    refs = list(refs)
    x_refs = [refs.pop(0) for _ in range(n_x)]
    if proj:
        oc_ref, ol_ref, wo_ref = refs[:3]
        refs = refs[3:]
    mod_ref, g_ref, win_ref, wout_ref = refs[:4]
    refs = refs[4:]
    if convert:
        nin_ref, nout_ref = refs[:2]
        refs = refs[2:]
    out_refs = refs[:n_out]
    refs = refs[n_out:]
    if convert:
        nin_bf_ref, nout_bf_ref = refs[:2]
        refs = refs[2:]
    a_ref = refs[0]
    i = pl.program_id(0)
    if convert:
        nin_bf_ref[...] = nin_ref[...].astype(BF16)
        nout_bf_ref[...] = nout_ref[...].astype(BF16)
    x = x_refs[0][...] if n_x == 1 else _pick_rows(i, *x_refs)
    if proj:
        o = _pick_rows(i, oc_ref, ol_ref)
        x = x + mod_ref[i_pgate:i_pgate + 1, :] * _dot(o, wo_ref[...])
    h = _modulate(x, g_ref[...], mod_ref[i_shift:i_shift + 1, :],
                  mod_ref[i_scale:i_scale + 1, :]).astype(BF16)
    for c in range(D_FF // FF_CHUNK):
        lo = c * FF_CHUNK
        g = _dot(h, win_ref[:, lo:lo + FF_CHUNK])
        u = _dot(h, win_ref[:, D_FF + lo:D_FF + lo + FF_CHUNK])
        a_ref[:, lo:lo + FF_CHUNK] = ((g * jax.nn.sigmoid(g)) * u).astype(BF16)
    ff = _dot(a_ref[...], wout_ref[...])
    y = x + mod_ref[i_gate:i_gate + 1, :] * (0.5 * ff)
    if n_out == 1:
        out_refs[0][...] = y
    else:
        @pl.when(i < FIRST_LAT_TILE)
        def _():
            out_refs[0][...] = y

        @pl.when(i >= FIRST_LAT_TILE)
        def _():
            out_refs[1][...] = y


def _ffn(xs, mod_l, g, w_in, w_out, which, os=None, w_o=None, split_out=False, nxt=None):
    proj = os is not None
    convert = nxt is not None
    base = 0 if which == 0 else 6
    row = lambda i: (i, 0)
    full = lambda i: (0, 0)
    tile = (TM, D_MODEL)
    if len(xs) == 1:
        in_specs = [pl.BlockSpec(tile, row)]
    else:
        in_specs = [pl.BlockSpec(tile, _ctx_tile), pl.BlockSpec(tile, _lat_tile)]
    args = list(xs)
    if proj:
        in_specs += [pl.BlockSpec(tile, _ctx_tile), pl.BlockSpec(tile, _lat_tile),
                     pl.BlockSpec((D_MODEL, D_MODEL), full)]
        args += [os[0], os[1], w_o]
    in_specs += [
        pl.BlockSpec((None, N_MOD, D_MODEL),
                     lambda i: (_cond_of_tile(i, FIRST_LAT_TILE, TILES_PER_LAT_BATCH), 0, 0)),
        pl.BlockSpec((1, D_MODEL), full),
        pl.BlockSpec((D_MODEL, 2 * D_FF), full),
        pl.BlockSpec((D_FF, D_MODEL), full),
    ]
    args += [mod_l, g.reshape(1, D_MODEL), w_in, w_out]
    in_slab = (D_MODEL // FIRST_LAT_TILE, 2 * D_FF)
    out_slab = (D_FF // FIRST_LAT_TILE, D_MODEL)
    if convert:
        nw_in, nw_out, nl, nw = nxt
        slab = lambda i: (nl, nw, jnp.minimum(i, FIRST_LAT_TILE - 1), 0)
        in_specs += [pl.BlockSpec((None, None) + in_slab, slab),
                     pl.BlockSpec((None, None) + out_slab, slab)]
        args += [nw_in, nw_out]
    if split_out:
        out_specs = [pl.BlockSpec(tile, _ctx_tile), pl.BlockSpec(tile, _lat_tile)]
        out_shape = [jax.ShapeDtypeStruct((N_CTX, D_MODEL), F32),
                     jax.ShapeDtypeStruct((N_LAT, D_MODEL), F32)]
    else:
        out_specs = [pl.BlockSpec(tile, row)]
        out_shape = [jax.ShapeDtypeStruct((N_TOK, D_MODEL), F32)]
    n_out = len(out_shape)
    if convert:
        out_specs += [pl.BlockSpec(in_slab, _ctx_tile), pl.BlockSpec(out_slab, _ctx_tile)]
        out_shape += [jax.ShapeDtypeStruct((D_MODEL, 2 * D_FF), BF16),
                      jax.ShapeDtypeStruct((D_FF, D_MODEL), BF16)]
    kern = functools.partial(_ffn_kernel, n_x=len(xs), proj=proj, n_out=n_out, convert=convert,
                             i_shift=base, i_scale=base + 1, i_gate=base + 2, i_pgate=5)
    return pl.pallas_call(
        kern,
        grid=(N_TOK // TM,),
        in_specs=in_specs,
        out_specs=out_specs,
        out_shape=out_shape,
        scratch_shapes=[pltpu.VMEM((TM, D_FF), BF16)],
        compiler_params=_params("arbitrary"),
        name="ffn_proj" if proj else "ffn",
    )(*args)


def _diff_pre_kernel(*refs, rope, n_alias):
    refs = list(refs)
    x_ref, mod_ref, g_ref, w_ref = refs[:4]
    refs = refs[4:]
    if rope:
        wrot_ref, qc_ref, qs_ref, kc_ref, ks_ref = refs[:5]
        refs = refs[5:]
        tabs = ((qc_ref, qs_ref), (kc_ref, ks_ref))
    else:
        gains = refs[:2]
        refs = refs[2:]
    q_ref, kt_ref, v_ref = refs[n_alias:]
    h = _modulate(x_ref[...], g_ref[...], mod_ref[3:4, :], mod_ref[4:5, :]).astype(BF16)
    hw = DIFF_HEADS * DIFF_VD
    lane = lax.broadcasted_iota(jnp.int32, (1, LANES), 1)
    lo = lane < DIFF_HD
    for part in (1, 0):
        y = _dot(h, w_ref[:, part * hw:(part + 1) * hw])
        if rope:
            y_pair = _dot(h, wrot_ref[:, part * hw:(part + 1) * hw])
        for hd in range(DIFF_HEADS):
            sl = slice(hd * LANES, (hd + 1) * LANES)
            yh = y[:, sl]
            sq = yh * yh
            s_lo = jnp.sum(jnp.where(lo, sq, 0.0), axis=-1, keepdims=True)
            s_hi = jnp.sum(jnp.where(lo, 0.0, sq), axis=-1, keepdims=True)
            ms = jnp.where(lo, s_lo, s_hi) * (1.0 / DIFF_HD)
            if rope:
                gc_ref, gs_ref = tabs[part]
                yn = (yh * gc_ref[...] + y_pair[:, sl] * gs_ref[...]) * lax.rsqrt(ms + EPS)
            else:
                yn = yh * lax.rsqrt(ms + EPS) * gains[part][...]
            if part == 0:
                q_ref[:, sl] = yn.astype(q_ref.dtype)
            elif len(kt_ref.shape) == 2:
                kt_ref[sl, :] = yn.T
            else:
                ynt = yn.T
                for b in range(SEQS_PER_TILE):
                    kt_ref[b, sl, :] = ynt[:, b * SEQ:(b + 1) * SEQ]
    v = _dot(h, w_ref[:, 2 * hw:3 * hw])
    if len(v_ref.shape) == 2:
        v_ref[...] = v
    else:
        for b in range(SEQS_PER_TILE):
            v_ref[b] = v[b * SEQ:(b + 1) * SEQ]


def _diff_pre(x, mod_l, g, w_qkv, lat, gains=None, rot=None, j=0, stacks=None):
    n_rows = N_LAT if lat else N_CTX
    tile0 = N_CTX // TM if lat else 0
    per_lat = DEC_SEQ // TM
    full = lambda i: (0, 0)
    xrow = lambda i: (i + tile0, 0)
    row = lambda i: (i, 0)
    if lat:
        cond = lambda i: (1 + i // per_lat, 0, 0)
    else:
        cond = lambda i: (0, 0, 0)
    in_specs = [
        pl.BlockSpec((TM, D_MODEL), xrow),
        pl.BlockSpec((None, N_MOD, D_MODEL), cond),
        pl.BlockSpec((1, D_MODEL), full),
        pl.BlockSpec((D_MODEL, 3 * D_MODEL), full),
    ]
    args = [x, mod_l, g.reshape(1, D_MODEL), w_qkv]
    aliases = {}
    if lat:
        pos = lambda i: (i % per_lat, 0)
        in_specs += [pl.BlockSpec((D_MODEL, 2 * D_MODEL), full)] + [pl.BlockSpec((TM, LANES), pos)] * 4
        args += list(rot)
        out_specs = [pl.BlockSpec((TM, D_MODEL), row),
                     pl.BlockSpec((None, D_MODEL, TM), lambda i: (i // per_lat, 0, i % per_lat)),
                     pl.BlockSpec((TM, D_MODEL), row)]
        out_shape = [jax.ShapeDtypeStruct((N_LAT, D_MODEL), BF16),
                     jax.ShapeDtypeStruct((DEC_BATCH, D_MODEL, DEC_SEQ), F32),
                     jax.ShapeDtypeStruct((N_LAT, D_MODEL), F32)]
    else:
        in_specs += [pl.BlockSpec((1, LANES), full)] * 2
        args += list(gains)
        slot = lambda i: (i, j, 0, 0)
        out_specs = [pl.BlockSpec((TM, D_MODEL), row),
                     pl.BlockSpec((SEQS_PER_TILE, None, D_MODEL, SEQ), slot),
                     pl.BlockSpec((SEQS_PER_TILE, None, SEQ, D_MODEL), slot)]
        out_shape = [jax.ShapeDtypeStruct((N_CTX, D_MODEL), BF16),
                     jax.ShapeDtypeStruct((BATCH, N_DIFF, D_MODEL, SEQ), F32),
                     jax.ShapeDtypeStruct((BATCH, N_DIFF, SEQ, D_MODEL), F32)]
        if stacks is not None:
            aliases = {len(args): 1, len(args) + 1: 2}
            in_specs += [pl.BlockSpec(memory_space=pl.ANY)] * 2
            args += list(stacks)
    return pl.pallas_call(
        functools.partial(_diff_pre_kernel, rope=lat, n_alias=len(aliases)),
        grid=(n_rows // TM,),
        in_specs=in_specs,
        out_specs=out_specs,
        out_shape=out_shape,
        input_output_aliases=aliases,
        compiler_params=_params("parallel"),
        name="diff_pre_lat" if lat else "diff_pre_ctx",
    )(*args)


def _softmax_pv(s, v_aug):
    e = jnp.exp2(s - jnp.max(s, axis=-1, keepdims=True)).astype(BF16)
    r = _dot(e, v_aug)
    return r[:, :LANES] * (1.0 / r[:, LANES:])


def _diff_attn_kernel(*refs, heads, lam_init, cached):
    if cached:
        q_ref, ktc_ref, kt_ref, vc_ref, v_ref, lam_ref, sub_ref, o_ref = refs
    else:
        q_ref, kt_ref, v_ref, lam_ref, sub_ref, o_ref = refs
    lp = lam_ref[...]
    lam = (jnp.exp(jnp.sum(lp[0:1, :] * lp[1:2, :], axis=-1, keepdims=True))
           - jnp.exp(jnp.sum(lp[2:3, :] * lp[3:4, :], axis=-1, keepdims=True)) + lam_init)
    n_b, tq = q_ref.shape[:2]
    lane = lax.broadcasted_iota(jnp.int32, (1, LANES), 1)
    lo = lane < DIFF_HD
    sub_g = sub_ref[...]
    items = [(b, hd) for b in range(n_b) for hd in range(heads)]

    def scores(item):
        b, hd = item
        sl = slice(hd * LANES, (hd + 1) * LANES)
        qh = q_ref[b, :, sl]
        kt = kt_ref[b, sl, :].astype(BF16)
        if cached:
            kt = jnp.concatenate([ktc_ref[b, sl, :].astype(BF16), kt], axis=1)
        zero = jnp.zeros_like(qh)
        qq = jnp.concatenate([jnp.where(lo, qh, zero), jnp.where(lo, zero, qh)], axis=0)
        return _dot(qq, kt)

    s_next = scores(items[0])
    for n, (b, hd) in enumerate(items):
        sl = slice(hd * LANES, (hd + 1) * LANES)
        s = s_next
        if n + 1 < len(items):
            s_next = scores(items[n + 1])
        vh = v_ref[b, :, sl].astype(BF16)
        if cached:
            vh = jnp.concatenate([vc_ref[b, :, sl].astype(BF16), vh], axis=0)
        v_aug = jnp.concatenate([vh, jnp.ones_like(vh)], axis=1)
        o12 = _softmax_pv(s, v_aug)
        o = o12[:tq] - lam * o12[tq:]
        ms = jnp.mean(o * o, axis=-1, keepdims=True)
        o = (o * lax.rsqrt(ms + EPS) * sub_g) * (1.0 - lam_init)
        o_ref[b, :, sl] = o.astype(o_ref.dtype)


def _diff_attn(q, kt, v, lam_p, sub_g, lam_init, heads_per_step, j, caches=None):
    b, lq, _ = q.shape
    cached = caches is not None
    nb = 1 if cached else CTX_SEQS_PER_STEP
    w = heads_per_step * LANES
    n_hg = DIFF_HEADS // heads_per_step
    tq = min(TQ, lq) if lq == SEQ else LAT_TQ
    qmap = lambda bi, hi, qi: (bi, qi, hi)
    full = lambda bi, hi, qi: (0, 0)
    q_spec = pl.BlockSpec((nb, tq, w), qmap)
    kt_slot = lambda bi, hi, qi: (bi, j, hi, 0)
    v_slot = lambda bi, hi, qi: (bi, j, 0, hi)
    if cached:
        kt_cache, v_cache = caches
        in_specs = [q_spec,
                    pl.BlockSpec((nb, None, w, PAST_LEN), kt_slot),
                    pl.BlockSpec((nb, w, lq), lambda bi, hi, qi: (bi, hi, 0)),
                    pl.BlockSpec((nb, None, PAST_LEN, w), v_slot),
                    pl.BlockSpec((nb, lq, w), lambda bi, hi, qi: (bi, 0, hi))]
        args = [q, kt_cache, kt, v_cache, v]
    else:
        in_specs = [q_spec,
                    pl.BlockSpec((nb, None, w, lq), kt_slot),
                    pl.BlockSpec((nb, None, lq, w), v_slot)]
        args = [q, kt, v]
    in_specs += [pl.BlockSpec((4, DIFF_HD), full), pl.BlockSpec((1, LANES), full)]
    args += [lam_p, sub_g]
    return pl.pallas_call(
        functools.partial(_diff_attn_kernel, heads=heads_per_step, lam_init=lam_init, cached=cached),
        grid=(b // nb, n_hg, lq // tq),
        in_specs=in_specs,
        out_specs=pl.BlockSpec((nb, tq, w), qmap),
        out_shape=jax.ShapeDtypeStruct((b, lq, D_MODEL), BF16),
        compiler_params=_params("parallel", "parallel", "parallel"),
        name="diff_attn_lat" if cached else "diff_attn_ctx",
    )(*args)


def _mla_keys(ckv_n, kpe, wk_ref, wv_ref, kg, rope_refs, k_ref, v_ref):
    c = ckv_n.astype(BF16)
    kk = _dot(c, wk_ref[...])
    kpe = pltpu.roll(kpe, MLA_NOPE, 1)
    pe_ss = jnp.sum(kpe * kpe, axis=-1, keepdims=True)
    kpe_g = kpe * kg
    if rope_refs is not None:
        cos_ref, up_ref, dn_ref = rope_refs
        kpe_g = _rope(kpe_g, cos_ref[...], up_ref[...], dn_ref[...], MLA_ROPE // 4)
    for hd in range(MLA_HEADS):
        sl = slice(hd * LANES, (hd + 1) * LANES)
        kh = kk[:, sl]
        ms = (jnp.sum(kh * kh, axis=-1, keepdims=True) + pe_ss) * (1.0 / MLA_QK)
        kn = (kh * kg + kpe_g) * lax.rsqrt(ms + EPS)
        k_ref[:, sl] = kn.astype(k_ref.dtype)
    v_ref[...] = _dot(c, wv_ref[...]).astype(v_ref.dtype)


def _mla_pre_kernel(*refs, lat, n_alias):
    refs = list(refs)
    x_ref, mod_ref, g_ref, wdq_ref, wdkv_ref, qag_ref, kvag_ref, wq_ref = refs[:8]
    refs = refs[8:]
    if lat:
        wqrot_ref, gc_ref, gs_ref = refs[:3]
        q_ref, ckv_ref, kpe_ref = refs[3:]
    else:
        qg_ref, wk_ref, wv_ref, kg_ref = refs[:4]
        q_ref, k_ref, v_ref, ckvs_ref, kpet_ref = refs[4 + n_alias:]
    h = _modulate(x_ref[...], g_ref[...], mod_ref[3:4, :], mod_ref[4:5, :]).astype(BF16)
    d2 = _dot(h, wdkv_ref[...])
    ckv = d2[:, :KV_RANK]
    ms = jnp.mean(ckv * ckv, axis=-1, keepdims=True)
    ckv_n = ckv * lax.rsqrt(ms + EPS) * kvag_ref[...]
    kpe = d2[:, KV_RANK:]
    if lat:
        ckv_ref[...] = ckv_n
        kpe_ref[...] = kpe
    cq = _dot(h, wdq_ref[...])
    ms = jnp.mean(cq * cq, axis=-1, keepdims=True)
    cqn = (cq * lax.rsqrt(ms + EPS) * qag_ref[...]).astype(BF16)
    q = _dot(cqn, wq_ref[...])
    if lat:
        q_pair = _dot(cqn, wqrot_ref[...])
    for hd in range(MLA_HEADS):
        sl = slice(hd * LANES, (hd + 1) * LANES)
        qh = q[:, sl]
        ms = jnp.sum(qh * qh, axis=-1, keepdims=True) * (1.0 / MLA_QK)
        if lat:
            qn = (qh * gc_ref[...] + q_pair[:, sl] * gs_ref[...]) * lax.rsqrt(ms + EPS)
        else:
            qn = qh * lax.rsqrt(ms + EPS) * qg_ref[...]
        q_ref[:, sl] = qn.astype(q_ref.dtype)
    if not lat:
        kpe_t = kpe.T
        for b in range(SEQS_PER_TILE):
            ckvs_ref[b] = ckv_n[b * SEQ:(b + 1) * SEQ]
            kpet_ref[b] = kpe_t[:MLA_ROPE, b * SEQ:(b + 1) * SEQ]
        _mla_keys(ckv_n, kpe, wk_ref, wv_ref, kg_ref[...], None, k_ref, v_ref)


def _mla_pre(x, mod_l, g, w_dq, w_dkv, qag, kvag, w_q, lat, rot=None, ctx_weights=None,
             j=0, stacks=None):
    n_rows = N_LAT if lat else N_CTX
    tile0 = N_CTX // TM if lat else 0
    per_lat = DEC_SEQ // TM
    full = lambda i: (0, 0)
    xrow = lambda i: (i + tile0, 0)
    row = lambda i: (i, 0)
    if lat:
        cond = lambda i: (1 + i // per_lat, 0, 0)
    else:
        cond = lambda i: (0, 0, 0)
    kvw = KV_RANK + LANES
    qw = MLA_HEADS * LANES
    vw = MLA_HEADS * MLA_VD
    in_specs = [
        pl.BlockSpec((TM, D_MODEL), xrow),
        pl.BlockSpec((None, N_MOD, D_MODEL), cond),
        pl.BlockSpec((1, D_MODEL), full),
        pl.BlockSpec((D_MODEL, Q_RANK), full),
        pl.BlockSpec((D_MODEL, kvw), full),
        pl.BlockSpec((1, Q_RANK), full),
        pl.BlockSpec((1, KV_RANK), full),
        pl.BlockSpec((Q_RANK, qw), full),
    ]
    args = [x, mod_l, g.reshape(1, D_MODEL), w_dq, w_dkv, qag.reshape(1, Q_RANK),
            kvag.reshape(1, KV_RANK), w_q]
    aliases = {}
    if lat:
        pos = lambda i: (i % per_lat, 0)
        in_specs += [pl.BlockSpec((Q_RANK, qw), full)] + [pl.BlockSpec((TM, LANES), pos)] * 2
        args += list(rot)
        out_specs = [pl.BlockSpec((TM, qw), row), pl.BlockSpec((TM, KV_RANK), row),
                     pl.BlockSpec((TM, LANES), row)]
        out_shape = [jax.ShapeDtypeStruct((n_rows, qw), BF16),
                     jax.ShapeDtypeStruct((n_rows, KV_RANK), F32),
                     jax.ShapeDtypeStruct((n_rows, LANES), F32)]
    else:
        qg, w_k, w_v, kg = ctx_weights
        in_specs += [pl.BlockSpec((1, LANES), full), pl.BlockSpec((KV_RANK, qw), full),
                     pl.BlockSpec((KV_RANK, vw), full), pl.BlockSpec((1, LANES), full)]
        args += [qg, w_k, w_v, kg]
        slot = lambda i: (i, j, 0, 0)
        out_specs = [pl.BlockSpec((TM, qw), row), pl.BlockSpec((TM, qw), row),
                     pl.BlockSpec((TM, vw), row),
                     pl.BlockSpec((SEQS_PER_TILE, None, SEQ, KV_RANK), slot),
                     pl.BlockSpec((SEQS_PER_TILE, None, MLA_ROPE, SEQ), slot)]
        out_shape = [jax.ShapeDtypeStruct((n_rows, qw), BF16),
                     jax.ShapeDtypeStruct((n_rows, qw), BF16),
                     jax.ShapeDtypeStruct((n_rows, vw), BF16),
                     jax.ShapeDtypeStruct((BATCH, N_MLA, SEQ, KV_RANK), F32),
                     jax.ShapeDtypeStruct((BATCH, N_MLA, MLA_ROPE, SEQ), F32)]
        if stacks is not None:
            aliases = {len(args): 3, len(args) + 1: 4}
            in_specs += [pl.BlockSpec(memory_space=pl.ANY)] * 2
            args += list(stacks)
    return pl.pallas_call(
        functools.partial(_mla_pre_kernel, lat=lat, n_alias=len(aliases)),
        grid=(n_rows // TM,),
        in_specs=in_specs,
        out_specs=out_specs,
        out_shape=out_shape,
        input_output_aliases=aliases,
        compiler_params=_params("parallel"),
        name="mla_pre_lat" if lat else "mla_pre_ctx",
    )(*args)


def _mla_kv_kernel(ckv_ref, kpe_ref, wk_ref, wv_ref, kg_ref, cos_ref, up_ref, dn_ref, k_ref, v_ref):
    _mla_keys(ckv_ref[...], kpe_ref[...], wk_ref, wv_ref, kg_ref[...],
              (cos_ref, up_ref, dn_ref), k_ref, v_ref)


def _mla_kv_lat(ckv_rows, kpe_rows, w_k, w_v, kg, tables):
    n_rows = ckv_rows.shape[0]
    tm = KVP_TM_LAT
    per = LAT_KV // tm
    full = lambda i: (0, 0)
    row = lambda i: (i, 0)
    pos = lambda i: (i % per, 0)
    kw = MLA_HEADS * LANES
    vw = MLA_HEADS * MLA_VD
    return pl.pallas_call(
        _mla_kv_kernel,
        grid=(n_rows // tm,),
        in_specs=[
            pl.BlockSpec((tm, KV_RANK), row),
            pl.BlockSpec((tm, LANES), row),
            pl.BlockSpec((KV_RANK, kw), full),
            pl.BlockSpec((KV_RANK, vw), full),
            pl.BlockSpec((1, LANES), full),
        ] + [pl.BlockSpec((tm, LANES), pos)] * 3,
        out_specs=[pl.BlockSpec((tm, kw), row), pl.BlockSpec((tm, vw), row)],
        out_shape=[jax.ShapeDtypeStruct((n_rows, kw), BF16),
                   jax.ShapeDtypeStruct((n_rows, vw), BF16)],
        compiler_params=_params("parallel"),
        name="mla_kv_lat",
    )(ckv_rows, kpe_rows, w_k, w_v, kg, *tables)


def _mla_attn_kernel(q_ref, k_ref, v_ref, o_ref, *, pairs):
    lane = lax.broadcasted_iota(jnp.int32, (1, LANES), 1)
    lo = lane < MLA_VD

    items = [(b, hd) for b in range(q_ref.shape[0]) for hd in range(2 * pairs)]

    def scores(item):
        b, hd = item
        sl = slice(hd * LANES, (hd + 1) * LANES)
        return _dot_nt(q_ref[b, :, sl], k_ref[b, :, sl])

    s_next = scores(items[0])
    outs = []
    for n, (b, hd) in enumerate(items):
        s = s_next
        if n + 1 < len(items):
            s_next = scores(items[n + 1])
        pl_ = slice((hd // 2) * LANES, (hd // 2 + 1) * LANES)
        vp = v_ref[b, :, pl_]
        outs.append(_softmax_pv(s, jnp.concatenate([vp, jnp.ones_like(vp)], axis=1)))
        if hd % 2 == 1:
            o_ref[b, :, pl_] = jnp.where(lo, outs[0], outs[1]).astype(o_ref.dtype)
            outs = []


def _mla_attn(q, k, v, pairs_per_step):
    b, lq, _ = q.shape
    lk = k.shape[1]
    nb = CTX_SEQS_PER_STEP if lq == SEQ else 1
    n_pg = MLA_HEADS // 2 // pairs_per_step
    tq = min(TQ, lq) if lq == SEQ else LAT_TQ
    qkw = pairs_per_step * 2 * LANES
    ow = pairs_per_step * LANES
    qmap = lambda bi, hi, qi: (bi, qi, hi)
    kmap = lambda bi, hi, qi: (bi, 0, hi)
    return pl.pallas_call(
        functools.partial(_mla_attn_kernel, pairs=pairs_per_step),
        grid=(b // nb, n_pg, lq // tq),
        in_specs=[
            pl.BlockSpec((nb, tq, qkw), qmap),
            pl.BlockSpec((nb, lk, qkw), kmap),
            pl.BlockSpec((nb, lk, ow), kmap),
        ],
        out_specs=pl.BlockSpec((nb, tq, ow), qmap),
        out_shape=jax.ShapeDtypeStruct((b, lq, D_MODEL), BF16),
        compiler_params=_params("parallel", "parallel", "parallel"),
        name="mla_attn_lat" if lq == DEC_SEQ else "mla_attn_ctx",
    )(q, k, v)


def _rope_tables(n_rot, lane0, identity_rows):
    rows = DEC_SEQ // GRID_W
    row = np.repeat(np.arange(rows), GRID_W)
    col = np.tile(np.arange(GRID_W), rows)
    n = n_rot // 2
    freqs = ROPE_THETA ** (-np.arange(0, n, 2, dtype=np.float64) / n)
    zeros = np.zeros((DEC_SEQ, n // 2))
    cos_parts, up_parts, dn_parts = [], [], []
    for pos in (row, col):
        ang = pos.astype(np.float64)[:, None] * freqs[None, :]
        c, s = np.cos(ang), np.sin(ang)
        cos_parts += [c, c]
        up_parts += [-s, zeros]
        dn_parts += [zeros, s]

    def place(parts, fill):
        reps = 2 if lane0 == 0 else 1
        body = np.concatenate(parts * reps, axis=1)
        full = np.full((identity_rows + DEC_SEQ, LANES), fill)
        full[identity_rows:, lane0:lane0 + body.shape[1]] = body
        return jnp.asarray(full, dtype=F32)

    return place(cos_parts, 1.0), place(up_parts, 0.0), place(dn_parts, 0.0)


def _swap_pairs(w, half):
    k, n = w.shape
    return w.reshape(k, n // (2 * half), 2, half)[:, :, ::-1, :].reshape(k, n)


def _pair_tables(tables, gain, half):
    cos, up, dn = tables
    partner = np.arange(LANES) ^ half
    return cos * gain, (up + dn) * gain[:, partner]


def kernel(x_prompt, x_sample, c, cache_diff_k, cache_diff_v, cache_mla_ckv, cache_mla_kpe, c_ctx, w_mod, b_mod, norm_g, ffn_w_in, ffn_w_out, diff_w_qkv, diff_q_norm, diff_k_norm, diff_lambda, diff_subln, diff_w_o, mla_w_down, mla_q_a_norm, mla_kv_a_norm, mla_w_q_up, mla_w_kv_up, mla_q_norm, mla_k_norm, mla_w_o):
    xs = (x_prompt.reshape(N_CTX, D_MODEL), x_sample.reshape(N_LAT, D_MODEL))
    conds = jnp.concatenate(
        [c_ctx[None, :], c, jnp.zeros((COND_ROWS - 1 - DEC_BATCH, D_MODEL), F32)], axis=0)
    mod = _modulation(conds, w_mod, b_mod)

    w_in = ffn_w_in[0, 0].astype(BF16)
    w_out = ffn_w_out[0, 0].astype(BF16)
    diff_tabs = _rope_tables(DIFF_HD, 0, 0)
    mla_tabs = _rope_tables(MLA_ROPE, MLA_NOPE, 0)
    mla_kv_tabs = _rope_tables(MLA_ROPE, MLA_NOPE, PAST_LEN)
    cache_kt = jnp.transpose(cache_diff_k, (0, 1, 3, 4, 5, 2)).reshape(
        DEC_BATCH, N_DIFF, D_MODEL, PAST_LEN)
    cache_v = cache_diff_v.reshape(DEC_BATCH, N_DIFF, PAST_LEN, D_MODEL)

    diff_stacks, mla_stacks = None, None
    for l in range(DEPTH):
        j = l // 2
        mod_l = mod[l]
        x, w_in, w_out = _ffn(xs, mod_l, norm_g[l, 0], w_in, w_out, 0,
                              nxt=(ffn_w_in, ffn_w_out, l, 1))
        g1 = norm_g[l, 1]
        if l % 2 == 0:
            w_qkv = diff_w_qkv[j].astype(BF16)
            qg = jnp.tile(diff_q_norm[j] * DIFF_Q_SCALE, 2).reshape(1, LANES)
            kg = jnp.tile(diff_k_norm[j], 2).reshape(1, LANES)
            sub_g = diff_subln[j].reshape(1, LANES)
            lam_init = 0.8 - 0.6 * math.exp(-0.3 * l)
            q_c, kt_stack, v_stack = _diff_pre(x, mod_l, g1, w_qkv, lat=False, gains=(qg, kg),
                                               j=j, stacks=diff_stacks)
            diff_stacks = (kt_stack, v_stack)
            w_rot = _swap_pairs(w_qkv[:, :2 * D_MODEL], DIFF_HD // 4)
            rot = (w_rot,) + _pair_tables(diff_tabs, qg, DIFF_HD // 4) + _pair_tables(
                diff_tabs, kg, DIFF_HD // 4)
            q_l, kt_l, v_l = _diff_pre(x, mod_l, g1, w_qkv, lat=True, rot=rot)
            o_c = _diff_attn(q_c.reshape(BATCH, SEQ, D_MODEL), kt_stack, v_stack, diff_lambda[j],
                             sub_g, lam_init, heads_per_step=DIFF_HEADS, j=j)
            o_l = _diff_attn(q_l.reshape(DEC_BATCH, DEC_SEQ, D_MODEL), kt_l,
                             v_l.reshape(DEC_BATCH, DEC_SEQ, D_MODEL), diff_lambda[j], sub_g,
                             lam_init, heads_per_step=LAT_HEADS_PER_STEP, j=j,
                             caches=(cache_kt, cache_v))
            w_o = diff_w_o[j].astype(BF16)
        else:
            wd = mla_w_down[j]
            w_dq = wd[:, :Q_RANK].astype(BF16)
            w_dkv = jnp.pad(wd[:, Q_RANK:], ((0, 0), (0, LANES - MLA_ROPE))).astype(BF16)
            w_q = jnp.pad(mla_w_q_up[j].reshape(Q_RANK, MLA_HEADS, MLA_QK),
                          ((0, 0), (0, 0), (0, LANES - MLA_QK))).reshape(Q_RANK, MLA_HEADS * LANES).astype(BF16)
            wkv = mla_w_kv_up[j].reshape(KV_RANK, MLA_HEADS, MLA_NOPE + MLA_VD)
            w_k = jnp.pad(wkv[:, :, :MLA_NOPE], ((0, 0), (0, 0), (0, LANES - MLA_NOPE))
                          ).reshape(KV_RANK, MLA_HEADS * LANES).astype(BF16)
            w_v = wkv[:, :, MLA_NOPE:].reshape(KV_RANK, MLA_HEADS * MLA_VD).astype(BF16)
            qg = jnp.pad(mla_q_norm[j] * MLA_Q_SCALE, (0, LANES - MLA_QK)).reshape(1, LANES)
            kg = jnp.pad(mla_k_norm[j], (0, LANES - MLA_QK)).reshape(1, LANES)
            pre = functools.partial(_mla_pre, x, mod_l, g1, w_dq, w_dkv, mla_q_a_norm[j],
                                    mla_kv_a_norm[j], w_q)
            q_c, k_c, v_c, ckv_stack, kpet_stack = pre(lat=False, ctx_weights=(qg, w_k, w_v, kg),
                                                      j=j, stacks=mla_stacks)
            mla_stacks = (ckv_stack, kpet_stack)
            rot = (_swap_pairs(w_q, MLA_ROPE // 4),) + _pair_tables(mla_tabs, qg, MLA_ROPE // 4)
            q_l, ckv_l, kpe_l = pre(lat=True, rot=rot)
            cache_kpe = jnp.pad(cache_mla_kpe[:, j], ((0, 0), (0, 0), (0, LANES - MLA_ROPE)))
            ckv_rows = jnp.concatenate([cache_mla_ckv[:, j], ckv_l.reshape(DEC_BATCH, DEC_SEQ, KV_RANK)],
                                       axis=1).reshape(DEC_BATCH * LAT_KV, KV_RANK)
            kpe_rows = jnp.concatenate([cache_kpe, kpe_l.reshape(DEC_BATCH, DEC_SEQ, LANES)],
                                       axis=1).reshape(DEC_BATCH * LAT_KV, LANES)
            k_a, v_a = _mla_kv_lat(ckv_rows, kpe_rows, w_k, w_v, kg, mla_kv_tabs)
            kw = MLA_HEADS * LANES
            o_c = _mla_attn(q_c.reshape(BATCH, SEQ, kw), k_c.reshape(BATCH, SEQ, kw),
                            v_c.reshape(BATCH, SEQ, D_MODEL), pairs_per_step=MLA_HEADS // 2)
            o_l = _mla_attn(q_l.reshape(DEC_BATCH, DEC_SEQ, kw), k_a.reshape(DEC_BATCH, LAT_KV, kw),
                            v_a.reshape(DEC_BATCH, LAT_KV, D_MODEL),
                            pairs_per_step=LAT_HEADS_PER_STEP)
            w_o = mla_w_o[j].astype(BF16)
        os = (o_c.reshape(N_CTX, D_MODEL), o_l.reshape(N_LAT, D_MODEL))
        last = l == DEPTH - 1
        xs = _ffn((x,), mod_l, norm_g[l, 2], w_in, w_out, 1, os=os, w_o=w_o, split_out=last,
                  nxt=None if last else (ffn_w_in, ffn_w_out, l + 1, 0))
        if not last:
            xs, (w_in, w_out) = xs[:1], xs[1:]

    kt_stack, v_stack = diff_stacks
    ckv_stack, kpet_stack = mla_stacks
    new_diff_k = jnp.transpose(kt_stack.reshape(BATCH, N_DIFF, DIFF_HEADS, 2, DIFF_HD, SEQ),
                               (0, 1, 5, 2, 3, 4))
    new_diff_v = v_stack.reshape(BATCH, N_DIFF, SEQ, DIFF_HEADS, DIFF_VD)
    new_mla_kpe = jnp.transpose(kpet_stack, (0, 1, 3, 2))
    return (xs[0].reshape(BATCH, SEQ, D_MODEL), xs[1].reshape(DEC_BATCH, DEC_SEQ, D_MODEL),
            new_diff_k, new_diff_v, ckv_stack, new_mla_kpe)
```

```python
import functools
import math

import jax
import jax.numpy as jnp
import numpy as np
from jax import lax
from jax.experimental import pallas as pl
from jax.experimental.pallas import tpu as pltpu

D_MODEL = 1024
BATCH = 32
SEQ = 256
DEPTH = 4
DEC_BATCH = 2
DEC_SEQ = 1024
PAST_LEN = 256
GRID_W = 64
N_DIFF = (DEPTH + 1) // 2
N_MLA = DEPTH // 2
N_MOD = 9
D_FF = 2816
DIFF_HEADS = 8
DIFF_HD = 64
DIFF_VD = 128
MLA_HEADS = 16
MLA_NOPE = 64
MLA_ROPE = 32
MLA_QK = MLA_NOPE + MLA_ROPE
MLA_VD = 64
Q_RANK = 768
KV_RANK = 256
ROPE_THETA = 10000.0
EPS = 1e-6
LOG2E = math.log2(math.e)
DIFF_Q_SCALE = DIFF_HD ** -0.5 * LOG2E
MLA_Q_SCALE = MLA_QK ** -0.5 * LOG2E

N_CTX = BATCH * SEQ
N_LAT = DEC_BATCH * DEC_SEQ
N_TOK = N_CTX + N_LAT
LAT_KV = PAST_LEN + DEC_SEQ

LANES = 128
COND_ROWS = 8
TM = 512
SEQS_PER_TILE = TM // SEQ
FF_CHUNK = 256
TQ = 256
LAT_TQ = 512
LAT_HEADS_PER_STEP = 8
CTX_SEQS_PER_STEP = 4
KVP_TM_LAT = 640
MOD_TN = 2304
VMEM_LIMIT = 56 * 1024 * 1024

F32 = jnp.float32
BF16 = jnp.bfloat16


def _params(*sem):
    return pltpu.CompilerParams(dimension_semantics=sem, vmem_limit_bytes=VMEM_LIMIT)


def _dot(a, b):
    return jnp.dot(a, b, preferred_element_type=F32)


def _dot_nt(a, b):
    return lax.dot_general(a, b, (((1,), (1,)), ((), ())), preferred_element_type=F32)


def _modulate(x, g, shift, scale):
    ms = jnp.mean(x * x, axis=-1, keepdims=True)
    return (x * lax.rsqrt(ms + EPS) * g) * (1.0 + scale) + shift


def _rope(x, cos, s_up, s_dn, shift):
    return (x * cos + pltpu.roll(x, LANES - shift, 1) * s_up
            + pltpu.roll(x, shift, 1) * s_dn)


def _cond_of_tile(i, first_lat_tile, tiles_per_lat_batch):
    lat = jnp.maximum(i - first_lat_tile, 0) // tiles_per_lat_batch
    return jnp.where(i < first_lat_tile, 0, 1 + lat)


def _mod_kernel(c_ref, w_ref, b_ref, o_ref):
    c = c_ref[...]
    s = (c * jax.nn.sigmoid(c)).astype(BF16)
    o_ref[...] = _dot(s, w_ref[...].astype(BF16)) + b_ref[...]


def _modulation(conds, w_mod, b_mod):
    n_out = N_MOD * D_MODEL
    out = pl.pallas_call(
        _mod_kernel,
        grid=(DEPTH, n_out // MOD_TN),
        in_specs=[
            pl.BlockSpec((COND_ROWS, D_MODEL), lambda l, n: (0, 0)),
            pl.BlockSpec((None, D_MODEL, MOD_TN), lambda l, n: (l, 0, n)),
            pl.BlockSpec((None, 1, MOD_TN), lambda l, n: (l, 0, n)),
        ],
        out_specs=pl.BlockSpec((None, COND_ROWS, MOD_TN), lambda l, n: (l, 0, n)),
        out_shape=jax.ShapeDtypeStruct((DEPTH, COND_ROWS, n_out), F32),
        compiler_params=_params("parallel", "parallel"),
        name="modulation",
    )(conds, w_mod, b_mod.reshape(DEPTH, 1, n_out))
    return out.reshape(DEPTH, COND_ROWS, N_MOD, D_MODEL)


FIRST_LAT_TILE = N_CTX // TM
TILES_PER_LAT_BATCH = DEC_SEQ // TM


def _ctx_tile(i):
    return (jnp.minimum(i, FIRST_LAT_TILE - 1), 0)


def _lat_tile(i):
    return (jnp.maximum(i - FIRST_LAT_TILE, 0), 0)


def _pick_rows(i, ctx_ref, lat_ref):
    rows = i * TM + lax.broadcasted_iota(jnp.int32, (TM, 1), 0)
    return jnp.where(rows < N_CTX, ctx_ref[...], lat_ref[...])


def _ffn_kernel(*refs, n_x, proj, n_out, convert, i_shift, i_scale, i_gate, i_pgate):
    refs = list(refs)
    x_refs = [refs.pop(0) for _ in range(n_x)]
    if proj:
        oc_ref, ol_ref, wo_ref = refs[:3]
        refs = refs[3:]
    mod_ref, g_ref, win_ref, wout_ref = refs[:4]
    refs = refs[4:]
    if convert:
        nin_ref, nout_ref = refs[:2]
        refs = refs[2:]
    out_refs = refs[:n_out]
    refs = refs[n_out:]
    if convert:
        nin_bf_ref, nout_bf_ref = refs[:2]
        refs = refs[2:]
    a_ref = refs[0]
    i = pl.program_id(0)
    if convert:
        nin_bf_ref[...] = nin_ref[...].astype(BF16)
        nout_bf_ref[...] = nout_ref[...].astype(BF16)
    x = x_refs[0][...] if n_x == 1 else _pick_rows(i, *x_refs)
    if proj:
        o = _pick_rows(i, oc_ref, ol_ref)
        x = x + mod_ref[i_pgate:i_pgate + 1, :] * _dot(o, wo_ref[...])
    h = _modulate(x, g_ref[...], mod_ref[i_shift:i_shift + 1, :],
                  mod_ref[i_scale:i_scale + 1, :]).astype(BF16)
    for c in range(D_FF // FF_CHUNK):
        lo = c * FF_CHUNK
        g = _dot(h, win_ref[:, lo:lo + FF_CHUNK])
        u = _dot(h, win_ref[:, D_FF + lo:D_FF + lo + FF_CHUNK])
        a_ref[:, lo:lo + FF_CHUNK] = ((g * jax.nn.sigmoid(g)) * u).astype(BF16)
    ff = _dot(a_ref[...], wout_ref[...])
    y = x + mod_ref[i_gate:i_gate + 1, :] * (0.5 * ff)
    if n_out == 1:
        out_refs[0][...] = y
    else:
        @pl.when(i < FIRST_LAT_TILE)
        def _():
            out_refs[0][...] = y

        @pl.when(i >= FIRST_LAT_TILE)
        def _():
            out_refs[1][...] = y


def _ffn(xs, mod_l, g, w_in, w_out, which, os=None, w_o=None, split_out=False, nxt=None):
    proj = os is not None
    convert = nxt is not None
    base = 0 if which == 0 else 6
    row = lambda i: (i, 0)
    full = lambda i: (0, 0)
    tile = (TM, D_MODEL)
    if len(xs) == 1:
        in_specs = [pl.BlockSpec(tile, row)]
    else:
        in_specs = [pl.BlockSpec(tile, _ctx_tile), pl.BlockSpec(tile, _lat_tile)]
    args = list(xs)
    if proj:
        in_specs += [pl.BlockSpec(tile, _ctx_tile), pl.BlockSpec(tile, _lat_tile),
                     pl.BlockSpec((D_MODEL, D_MODEL), full)]
        args += [os[0], os[1], w_o]
    in_specs += [
        pl.BlockSpec((None, N_MOD, D_MODEL),
                     lambda i: (_cond_of_tile(i, FIRST_LAT_TILE, TILES_PER_LAT_BATCH), 0, 0)),
        pl.BlockSpec((1, D_MODEL), full),
        pl.BlockSpec((D_MODEL, 2 * D_FF), full),
        pl.BlockSpec((D_FF, D_MODEL), full),
    ]
    args += [mod_l, g.reshape(1, D_MODEL), w_in, w_out]
    in_slab = (D_MODEL // FIRST_LAT_TILE, 2 * D_FF)
    out_slab = (D_FF // FIRST_LAT_TILE, D_MODEL)
    if convert:
        nw_in, nw_out, nl, nw = nxt
        slab = lambda i: (nl, nw, jnp.minimum(i, FIRST_LAT_TILE - 1), 0)
        in_specs += [pl.BlockSpec((None, None) + in_slab, slab),
                     pl.BlockSpec((None, None) + out_slab, slab)]
        args += [nw_in, nw_out]
    if split_out:
        out_specs = [pl.BlockSpec(tile, _ctx_tile), pl.BlockSpec(tile, _lat_tile)]
        out_shape = [jax.ShapeDtypeStruct((N_CTX, D_MODEL), F32),
                     jax.ShapeDtypeStruct((N_LAT, D_MODEL), F32)]
    else:
        out_specs = [pl.BlockSpec(tile, row)]
        out_shape = [jax.ShapeDtypeStruct((N_TOK, D_MODEL), F32)]
    n_out = len(out_shape)
    if convert:
        out_specs += [pl.BlockSpec(in_slab, _ctx_tile), pl.BlockSpec(out_slab, _ctx_tile)]
        out_shape += [jax.ShapeDtypeStruct((D_MODEL, 2 * D_FF), BF16),
                      jax.ShapeDtypeStruct((D_FF, D_MODEL), BF16)]
    kern = functools.partial(_ffn_kernel, n_x=len(xs), proj=proj, n_out=n_out, convert=convert,
                             i_shift=base, i_scale=base + 1, i_gate=base + 2, i_pgate=5)
    return pl.pallas_call(
        kern,
        grid=(N_TOK // TM,),
        in_specs=in_specs,
        out_specs=out_specs,
        out_shape=out_shape,
        scratch_shapes=[pltpu.VMEM((TM, D_FF), BF16)],
        compiler_params=_params("arbitrary"),
        name="ffn_proj" if proj else "ffn",
    )(*args)


def _diff_pre_kernel(*refs, rope, n_alias):
    refs = list(refs)
    x_ref, mod_ref, g_ref, w_ref = refs[:4]
    refs = refs[4:]
    if rope:
        wrot_ref, qc_ref, qs_ref, kc_ref, ks_ref = refs[:5]
        refs = refs[5:]
        tabs = ((qc_ref, qs_ref), (kc_ref, ks_ref))
    else:
        gains = refs[:2]
        refs = refs[2:]
    q_ref, kt_ref, v_ref = refs[n_alias:]
    h = _modulate(x_ref[...], g_ref[...], mod_ref[3:4, :], mod_ref[4:5, :]).astype(BF16)
    hw = DIFF_HEADS * DIFF_VD
    lane = lax.broadcasted_iota(jnp.int32, (1, LANES), 1)
    lo = lane < DIFF_HD
    for part in (1, 0):
        y = _dot(h, w_ref[:, part * hw:(part + 1) * hw])
        if rope:
            y_pair = _dot(h, wrot_ref[:, part * hw:(part + 1) * hw])
        for hd in range(DIFF_HEADS):
            sl = slice(hd * LANES, (hd + 1) * LANES)
            yh = y[:, sl]
            sq = yh * yh
            s_lo = jnp.sum(jnp.where(lo, sq, 0.0), axis=-1, keepdims=True)
            s_hi = jnp.sum(jnp.where(lo, 0.0, sq), axis=-1, keepdims=True)
            ms = jnp.where(lo, s_lo, s_hi) * (1.0 / DIFF_HD)
            if rope:
                gc_ref, gs_ref = tabs[part]
                yn = (yh * gc_ref[...] + y_pair[:, sl] * gs_ref[...]) * lax.rsqrt(ms + EPS)
            else:
                yn = yh * lax.rsqrt(ms + EPS) * gains[part][...]
            if part == 0:
                q_ref[:, sl] = yn.astype(q_ref.dtype)
            elif len(kt_ref.shape) == 2:
                kt_ref[sl, :] = yn.T
            else:
                ynt = yn.T
                for b in range(SEQS_PER_TILE):
                    kt_ref[b, sl, :] = ynt[:, b * SEQ:(b + 1) * SEQ]
    v = _dot(h, w_ref[:, 2 * hw:3 * hw])
    if len(v_ref.shape) == 2:
        v_ref[...] = v
    else:
        for b in range(SEQS_PER_TILE):
            v_ref[b] = v[b * SEQ:(b + 1) * SEQ]


def _diff_pre(x, mod_l, g, w_qkv, lat, gains=None, rot=None, j=0, stacks=None):
    n_rows = N_LAT if lat else N_CTX
    tile0 = N_CTX // TM if lat else 0
    per_lat = DEC_SEQ // TM
    full = lambda i: (0, 0)
    xrow = lambda i: (i + tile0, 0)
    row = lambda i: (i, 0)
    if lat:
        cond = lambda i: (1 + i // per_lat, 0, 0)
    else:
        cond = lambda i: (0, 0, 0)
    in_specs = [
        pl.BlockSpec((TM, D_MODEL), xrow),
        pl.BlockSpec((None, N_MOD, D_MODEL), cond),
        pl.BlockSpec((1, D_MODEL), full),
        pl.BlockSpec((D_MODEL, 3 * D_MODEL), full),
    ]
    args = [x, mod_l, g.reshape(1, D_MODEL), w_qkv]
    aliases = {}
    if lat:
        pos = lambda i: (i % per_lat, 0)
        in_specs += [pl.BlockSpec((D_MODEL, 2 * D_MODEL), full)] + [pl.BlockSpec((TM, LANES), pos)] * 4
        args += list(rot)
        out_specs = [pl.BlockSpec((TM, D_MODEL), row),
                     pl.BlockSpec((None, D_MODEL, TM), lambda i: (i // per_lat, 0, i % per_lat)),
                     pl.BlockSpec((TM, D_MODEL), row)]
        out_shape = [jax.ShapeDtypeStruct((N_LAT, D_MODEL), BF16),
                     jax.ShapeDtypeStruct((DEC_BATCH, D_MODEL, DEC_SEQ), F32),
                     jax.ShapeDtypeStruct((N_LAT, D_MODEL), F32)]
    else:
        in_specs += [pl.BlockSpec((1, LANES), full)] * 2
        args += list(gains)
        slot = lambda i: (i, j, 0, 0)
        out_specs = [pl.BlockSpec((TM, D_MODEL), row),
                     pl.BlockSpec((SEQS_PER_TILE, None, D_MODEL, SEQ), slot),
                     pl.BlockSpec((SEQS_PER_TILE, None, SEQ, D_MODEL), slot)]
        out_shape = [jax.ShapeDtypeStruct((N_CTX, D_MODEL), BF16),
                     jax.ShapeDtypeStruct((BATCH, N_DIFF, D_MODEL, SEQ), F32),
                     jax.ShapeDtypeStruct((BATCH, N_DIFF, SEQ, D_MODEL), F32)]
        if stacks is not None:
            aliases = {len(args): 1, len(args) + 1: 2}
            in_specs += [pl.BlockSpec(memory_space=pl.ANY)] * 2
            args += list(stacks)
    return pl.pallas_call(
        functools.partial(_diff_pre_kernel, rope=lat, n_alias=len(aliases)),
        grid=(n_rows // TM,),
        in_specs=in_specs,
        out_specs=out_specs,
        out_shape=out_shape,
        input_output_aliases=aliases,
        compiler_params=_params("parallel"),
        name="diff_pre_lat" if lat else "diff_pre_ctx",
    )(*args)


def _softmax_pv(s, v_aug):
    e = jnp.exp2(s - jnp.max(s, axis=-1, keepdims=True)).astype(BF16)
    r = _dot(e, v_aug)
    return r[:, :LANES] * (1.0 / r[:, LANES:])


def _diff_attn_kernel(*refs, heads, lam_init, cached):
    if cached:
        q_ref, ktc_ref, kt_ref, vc_ref, v_ref, lam_ref, sub_ref, o_ref = refs
    else:
        q_ref, kt_ref, v_ref, lam_ref, sub_ref, o_ref = refs
    lp = lam_ref[...]
    lam = (jnp.exp(jnp.sum(lp[0:1, :] * lp[1:2, :], axis=-1, keepdims=True))
           - jnp.exp(jnp.sum(lp[2:3, :] * lp[3:4, :], axis=-1, keepdims=True)) + lam_init)
    n_b, tq = q_ref.shape[:2]
    lane = lax.broadcasted_iota(jnp.int32, (1, LANES), 1)
    lo = lane < DIFF_HD
    sub_g = sub_ref[...]
    items = [(b, hd) for b in range(n_b) for hd in range(heads)]

    def scores(item):
        b, hd = item
        sl = slice(hd * LANES, (hd + 1) * LANES)
        qh = q_ref[b, :, sl]
        kt = kt_ref[b, sl, :].astype(BF16)
        if cached:
            kt = jnp.concatenate([ktc_ref[b, sl, :].astype(BF16), kt], axis=1)
        zero = jnp.zeros_like(qh)
        qq = jnp.concatenate([jnp.where(lo, qh, zero), jnp.where(lo, zero, qh)], axis=0)
        return _dot(qq, kt)

    s_next = scores(items[0])
    for n, (b, hd) in enumerate(items):
        sl = slice(hd * LANES, (hd + 1) * LANES)
        s = s_next
        if n + 1 < len(items):
            s_next = scores(items[n + 1])
        vh = v_ref[b, :, sl].astype(BF16)
        if cached:
            vh = jnp.concatenate([vc_ref[b, :, sl].astype(BF16), vh], axis=0)
        v_aug = jnp.concatenate([vh, jnp.ones_like(vh)], axis=1)
        o12 = _softmax_pv(s, v_aug)
        o = o12[:tq] - lam * o12[tq:]
        ms = jnp.mean(o * o, axis=-1, keepdims=True)
        o = (o * lax.rsqrt(ms + EPS) * sub_g) * (1.0 - lam_init)
        o_ref[b, :, sl] = o.astype(o_ref.dtype)


def _diff_attn(q, kt, v, lam_p, sub_g, lam_init, heads_per_step, j, caches=None):
    b, lq, _ = q.shape
    cached = caches is not None
    nb = 1 if cached else CTX_SEQS_PER_STEP
    w = heads_per_step * LANES
    n_hg = DIFF_HEADS // heads_per_step
    tq = min(TQ, lq) if lq == SEQ else LAT_TQ
    qmap = lambda bi, hi, qi: (bi, qi, hi)
    full = lambda bi, hi, qi: (0, 0)
    q_spec = pl.BlockSpec((nb, tq, w), qmap)
    kt_slot = lambda bi, hi, qi: (bi, j, hi, 0)
    v_slot = lambda bi, hi, qi: (bi, j, 0, hi)
    if cached:
        kt_cache, v_cache = caches
        in_specs = [q_spec,
                    pl.BlockSpec((nb, None, w, PAST_LEN), kt_slot),
                    pl.BlockSpec((nb, w, lq), lambda bi, hi, qi: (bi, hi, 0)),
                    pl.BlockSpec((nb, None, PAST_LEN, w), v_slot),
                    pl.BlockSpec((nb, lq, w), lambda bi, hi, qi: (bi, 0, hi))]
        args = [q, kt_cache, kt, v_cache, v]
    else:
        in_specs = [q_spec,
                    pl.BlockSpec((nb, None, w, lq), kt_slot),
                    pl.BlockSpec((nb, None, lq, w), v_slot)]
        args = [q, kt, v]
    in_specs += [pl.BlockSpec((4, DIFF_HD), full), pl.BlockSpec((1, LANES), full)]
    args += [lam_p, sub_g]
    return pl.pallas_call(
        functools.partial(_diff_attn_kernel, heads=heads_per_step, lam_init=lam_init, cached=cached),
        grid=(b // nb, n_hg, lq // tq),
        in_specs=in_specs,
        out_specs=pl.BlockSpec((nb, tq, w), qmap),
        out_shape=jax.ShapeDtypeStruct((b, lq, D_MODEL), BF16),
        compiler_params=_params("parallel", "parallel", "parallel"),
        name="diff_attn_lat" if cached else "diff_attn_ctx",
    )(*args)


def _mla_keys(ckv_n, kpe, wk_ref, wv_ref, kg, rope_refs, k_ref, v_ref):
    c = ckv_n.astype(BF16)
    kk = _dot(c, wk_ref[...])
    kpe = pltpu.roll(kpe, MLA_NOPE, 1)
    pe_ss = jnp.sum(kpe * kpe, axis=-1, keepdims=True)
    kpe_g = kpe * kg
    if rope_refs is not None:
        cos_ref, up_ref, dn_ref = rope_refs
        kpe_g = _rope(kpe_g, cos_ref[...], up_ref[...], dn_ref[...], MLA_ROPE // 4)
    for hd in range(MLA_HEADS):
        sl = slice(hd * LANES, (hd + 1) * LANES)
        kh = kk[:, sl]
        ms = (jnp.sum(kh * kh, axis=-1, keepdims=True) + pe_ss) * (1.0 / MLA_QK)
        kn = (kh * kg + kpe_g) * lax.rsqrt(ms + EPS)
        k_ref[:, sl] = kn.astype(k_ref.dtype)
    v_ref[...] = _dot(c, wv_ref[...]).astype(v_ref.dtype)


def _mla_pre_kernel(*refs, lat, n_alias):
    refs = list(refs)
    x_ref, mod_ref, g_ref, wdq_ref, wdkv_ref, qag_ref, kvag_ref, wq_ref = refs[:8]
    refs = refs[8:]
    if lat:
        wqrot_ref, gc_ref, gs_ref = refs[:3]
        q_ref, ckv_ref, kpe_ref = refs[3:]
    else:
        qg_ref, wk_ref, wv_ref, kg_ref = refs[:4]
        q_ref, k_ref, v_ref, ckvs_ref, kpet_ref = refs[4 + n_alias:]
    h = _modulate(x_ref[...], g_ref[...], mod_ref[3:4, :], mod_ref[4:5, :]).astype(BF16)
    d2 = _dot(h, wdkv_ref[...])
    ckv = d2[:, :KV_RANK]
    ms = jnp.mean(ckv * ckv, axis=-1, keepdims=True)
    ckv_n = ckv * lax.rsqrt(ms + EPS) * kvag_ref[...]
    kpe = d2[:, KV_RANK:]
    if lat:
        ckv_ref[...] = ckv_n
        kpe_ref[...] = kpe
    cq = _dot(h, wdq_ref[...])
    ms = jnp.mean(cq * cq, axis=-1, keepdims=True)
    cqn = (cq * lax.rsqrt(ms + EPS) * qag_ref[...]).astype(BF16)
    q = _dot(cqn, wq_ref[...])
    if lat:
        q_pair = _dot(cqn, wqrot_ref[...])
    for hd in range(MLA_HEADS):
        sl = slice(hd * LANES, (hd + 1) * LANES)
        qh = q[:, sl]
        ms = jnp.sum(qh * qh, axis=-1, keepdims=True) * (1.0 / MLA_QK)
        if lat:
            qn = (qh * gc_ref[...] + q_pair[:, sl] * gs_ref[...]) * lax.rsqrt(ms + EPS)
        else:
            qn = qh * lax.rsqrt(ms + EPS) * qg_ref[...]
        q_ref[:, sl] = qn.astype(q_ref.dtype)
    if not lat:
        kpe_t = kpe.T
        for b in range(SEQS_PER_TILE):
            ckvs_ref[b] = ckv_n[b * SEQ:(b + 1) * SEQ]
            kpet_ref[b] = kpe_t[:MLA_ROPE, b * SEQ:(b + 1) * SEQ]
        _mla_keys(ckv_n, kpe, wk_ref, wv_ref, kg_ref[...], None, k_ref, v_ref)


def _mla_pre(x, mod_l, g, w_dq, w_dkv, qag, kvag, w_q, lat, rot=None, ctx_weights=None,
             j=0, stacks=None):
    n_rows = N_LAT if lat else N_CTX
    tile0 = N_CTX // TM if lat else 0
    per_lat = DEC_SEQ // TM
    full = lambda i: (0, 0)
    xrow = lambda i: (i + tile0, 0)
    row = lambda i: (i, 0)
    if lat:
        cond = lambda i: (1 + i // per_lat, 0, 0)
    else:
        cond = lambda i: (0, 0, 0)
    kvw = KV_RANK + LANES
    qw = MLA_HEADS * LANES
    vw = MLA_HEADS * MLA_VD
    in_specs = [
        pl.BlockSpec((TM, D_MODEL), xrow),
        pl.BlockSpec((None, N_MOD, D_MODEL), cond),
        pl.BlockSpec((1, D_MODEL), full),
        pl.BlockSpec((D_MODEL, Q_RANK), full),
        pl.BlockSpec((D_MODEL, kvw), full),
        pl.BlockSpec((1, Q_RANK), full),
        pl.BlockSpec((1, KV_RANK), full),
        pl.BlockSpec((Q_RANK, qw), full),
    ]
    args = [x, mod_l, g.reshape(1, D_MODEL), w_dq, w_dkv, qag.reshape(1, Q_RANK),
            kvag.reshape(1, KV_RANK), w_q]
    aliases = {}
    if lat:
        pos = lambda i: (i % per_lat, 0)
        in_specs += [pl.BlockSpec((Q_RANK, qw), full)] + [pl.BlockSpec((TM, LANES), pos)] * 2
        args += list(rot)
        out_specs = [pl.BlockSpec((TM, qw), row), pl.BlockSpec((TM, KV_RANK), row),
                     pl.BlockSpec((TM, LANES), row)]
        out_shape = [jax.ShapeDtypeStruct((n_rows, qw), BF16),
                     jax.ShapeDtypeStruct((n_rows, KV_RANK), F32),
                     jax.ShapeDtypeStruct((n_rows, LANES), F32)]
    else:
        qg, w_k, w_v, kg = ctx_weights
        in_specs += [pl.BlockSpec((1, LANES), full), pl.BlockSpec((KV_RANK, qw), full),
                     pl.BlockSpec((KV_RANK, vw), full), pl.BlockSpec((1, LANES), full)]
        args += [qg, w_k, w_v, kg]
        slot = lambda i: (i, j, 0, 0)
        out_specs = [pl.BlockSpec((TM, qw), row), pl.BlockSpec((TM, qw), row),
                     pl.BlockSpec((TM, vw), row),
                     pl.BlockSpec((SEQS_PER_TILE, None, SEQ, KV_RANK), slot),
                     pl.BlockSpec((SEQS_PER_TILE, None, MLA_ROPE, SEQ), slot)]
        out_shape = [jax.ShapeDtypeStruct((n_rows, qw), BF16),
                     jax.ShapeDtypeStruct((n_rows, qw), BF16),
                     jax.ShapeDtypeStruct((n_rows, vw), BF16),
                     jax.ShapeDtypeStruct((BATCH, N_MLA, SEQ, KV_RANK), F32),
                     jax.ShapeDtypeStruct((BATCH, N_MLA, MLA_ROPE, SEQ), F32)]
        if stacks is not None:
            aliases = {len(args): 3, len(args) + 1: 4}
            in_specs += [pl.BlockSpec(memory_space=pl.ANY)] * 2
            args += list(stacks)
    return pl.pallas_call(
        functools.partial(_mla_pre_kernel, lat=lat, n_alias=len(aliases)),
        grid=(n_rows // TM,),
        in_specs=in_specs,
        out_specs=out_specs,
        out_shape=out_shape,
        input_output_aliases=aliases,
        compiler_params=_params("parallel"),
        name="mla_pre_lat" if lat else "mla_pre_ctx",
    )(*args)


def _mla_kv_kernel(ckv_ref, kpe_ref, wk_ref, wv_ref, kg_ref, cos_ref, up_ref, dn_ref, k_ref, v_ref):
    _mla_keys(ckv_ref[...], kpe_ref[...], wk_ref, wv_ref, kg_ref[...],
              (cos_ref, up_ref, dn_ref), k_ref, v_ref)


def _mla_kv_lat(ckv_rows, kpe_rows, w_k, w_v, kg, tables):
    n_rows = ckv_rows.shape[0]
    tm = KVP_TM_LAT
    per = LAT_KV // tm
    full = lambda i: (0, 0)
    row = lambda i: (i, 0)
    pos = lambda i: (i % per, 0)
    kw = MLA_HEADS * LANES
    vw = MLA_HEADS * MLA_VD
    return pl.pallas_call(
        _mla_kv_kernel,
        grid=(n_rows // tm,),
        in_specs=[
            pl.BlockSpec((tm, KV_RANK), row),
            pl.BlockSpec((tm, LANES), row),
            pl.BlockSpec((KV_RANK, kw), full),
            pl.BlockSpec((KV_RANK, vw), full),
            pl.BlockSpec((1, LANES), full),
        ] + [pl.BlockSpec((tm, LANES), pos)] * 3,
        out_specs=[pl.BlockSpec((tm, kw), row), pl.BlockSpec((tm, vw), row)],
        out_shape=[jax.ShapeDtypeStruct((n_rows, kw), BF16),
                   jax.ShapeDtypeStruct((n_rows, vw), BF16)],
        compiler_params=_params("parallel"),
        name="mla_kv_lat",
    )(ckv_rows, kpe_rows, w_k, w_v, kg, *tables)


def _mla_attn_kernel(q_ref, k_ref, v_ref, o_ref, *, pairs):
    lane = lax.broadcasted_iota(jnp.int32, (1, LANES), 1)
    lo = lane < MLA_VD

    items = [(b, hd) for b in range(q_ref.shape[0]) for hd in range(2 * pairs)]

    def scores(item):
        b, hd = item
        sl = slice(hd * LANES, (hd + 1) * LANES)
        return _dot_nt(q_ref[b, :, sl], k_ref[b, :, sl])

    s_next = scores(items[0])
    outs = []
    for n, (b, hd) in enumerate(items):
        s = s_next
        if n + 1 < len(items):
            s_next = scores(items[n + 1])
        pl_ = slice((hd // 2) * LANES, (hd // 2 + 1) * LANES)
        vp = v_ref[b, :, pl_]
        outs.append(_softmax_pv(s, jnp.concatenate([vp, jnp.ones_like(vp)], axis=1)))
        if hd % 2 == 1:
            o_ref[b, :, pl_] = jnp.where(lo, outs[0], outs[1]).astype(o_ref.dtype)
            outs = []


def _mla_attn(q, k, v, pairs_per_step):
    b, lq, _ = q.shape
    lk = k.shape[1]
    nb = CTX_SEQS_PER_STEP if lq == SEQ else 1
    n_pg = MLA_HEADS // 2 // pairs_per_step
    tq = min(TQ, lq) if lq == SEQ else LAT_TQ
    qkw = pairs_per_step * 2 * LANES
    ow = pairs_per_step * LANES
    qmap = lambda bi, hi, qi: (bi, qi, hi)
    kmap = lambda bi, hi, qi: (bi, 0, hi)
    return pl.pallas_call(
        functools.partial(_mla_attn_kernel, pairs=pairs_per_step),
        grid=(b // nb, n_pg, lq // tq),
        in_specs=[
            pl.BlockSpec((nb, tq, qkw), qmap),
            pl.BlockSpec((nb, lk, qkw), kmap),
            pl.BlockSpec((nb, lk, ow), kmap),
        ],
        out_specs=pl.BlockSpec((nb, tq, ow), qmap),
        out_shape=jax.ShapeDtypeStruct((b, lq, D_MODEL), BF16),
        compiler_params=_params("parallel", "parallel", "parallel"),
        name="mla_attn_lat" if lq == DEC_SEQ else "mla_attn_ctx",
    )(q, k, v)


def _rope_tables(n_rot, lane0, identity_rows):
    rows = DEC_SEQ // GRID_W
    row = np.repeat(np.arange(rows), GRID_W)
    col = np.tile(np.arange(GRID_W), rows)
    n = n_rot // 2
    freqs = ROPE_THETA ** (-np.arange(0, n, 2, dtype=np.float64) / n)
    zeros = np.zeros((DEC_SEQ, n // 2))
    cos_parts, up_parts, dn_parts = [], [], []
    for pos in (row, col):
        ang = pos.astype(np.float64)[:, None] * freqs[None, :]
        c, s = np.cos(ang), np.sin(ang)
        cos_parts += [c, c]
        up_parts += [-s, zeros]
        dn_parts += [zeros, s]

    def place(parts, fill):
        reps = 2 if lane0 == 0 else 1
        body = np.concatenate(parts * reps, axis=1)
        full = np.full((identity_rows + DEC_SEQ, LANES), fill)
        full[identity_rows:, lane0:lane0 + body.shape[1]] = body
        return jnp.asarray(full, dtype=F32)

    return place(cos_parts, 1.0), place(up_parts, 0.0), place(dn_parts, 0.0)


def _swap_pairs(w, half):
    first = (np.arange(w.shape[1]) % (2 * half) < half)[None, :]
    return jnp.where(first, jnp.roll(w, -half, axis=1), jnp.roll(w, half, axis=1))


def _pair_tables(tables, gain, half):
    cos, up, dn = tables
    return cos * gain, (up + dn) * _swap_pairs(gain, half)


def kernel(x_prompt, x_sample, c, cache_diff_k, cache_diff_v, cache_mla_ckv, cache_mla_kpe, c_ctx, w_mod, b_mod, norm_g, ffn_w_in, ffn_w_out, diff_w_qkv, diff_q_norm, diff_k_norm, diff_lambda, diff_subln, diff_w_o, mla_w_down, mla_q_a_norm, mla_kv_a_norm, mla_w_q_up, mla_w_kv_up, mla_q_norm, mla_k_norm, mla_w_o):
    xs = (x_prompt.reshape(N_CTX, D_MODEL), x_sample.reshape(N_LAT, D_MODEL))
    conds = jnp.concatenate(
        [c_ctx[None, :], c, jnp.zeros((COND_ROWS - 1 - DEC_BATCH, D_MODEL), F32)], axis=0)
    mod = _modulation(conds, w_mod, b_mod)

    w_in = ffn_w_in[0, 0].astype(BF16)
    w_out = ffn_w_out[0, 0].astype(BF16)
    diff_tabs = _rope_tables(DIFF_HD, 0, 0)
    mla_tabs = _rope_tables(MLA_ROPE, MLA_NOPE, 0)
    mla_kv_tabs = _rope_tables(MLA_ROPE, MLA_NOPE, PAST_LEN)
    cache_kt = jnp.transpose(cache_diff_k, (0, 1, 3, 4, 5, 2)).reshape(
        DEC_BATCH, N_DIFF, D_MODEL, PAST_LEN)
    cache_v = cache_diff_v.reshape(DEC_BATCH, N_DIFF, PAST_LEN, D_MODEL)

    diff_stacks, mla_stacks = None, None
    for l in range(DEPTH):
        j = l // 2
        mod_l = mod[l]
        x, w_in, w_out = _ffn(xs, mod_l, norm_g[l, 0], w_in, w_out, 0,
                              nxt=(ffn_w_in, ffn_w_out, l, 1))
        g1 = norm_g[l, 1]
        if l % 2 == 0:
            w_qkv = diff_w_qkv[j].astype(BF16)
            qg = jnp.tile(diff_q_norm[j] * DIFF_Q_SCALE, 2).reshape(1, LANES)
            kg = jnp.tile(diff_k_norm[j], 2).reshape(1, LANES)
            sub_g = diff_subln[j].reshape(1, LANES)
            lam_init = 0.8 - 0.6 * math.exp(-0.3 * l)
            q_c, kt_stack, v_stack = _diff_pre(x, mod_l, g1, w_qkv, lat=False, gains=(qg, kg),
                                               j=j, stacks=diff_stacks)
            diff_stacks = (kt_stack, v_stack)
            w_rot = _swap_pairs(w_qkv[:, :2 * D_MODEL], DIFF_HD // 4)
            rot = (w_rot,) + _pair_tables(diff_tabs, qg, DIFF_HD // 4) + _pair_tables(
                diff_tabs, kg, DIFF_HD // 4)
            q_l, kt_l, v_l = _diff_pre(x, mod_l, g1, w_qkv, lat=True, rot=rot)
            o_c = _diff_attn(q_c.reshape(BATCH, SEQ, D_MODEL), kt_stack, v_stack, diff_lambda[j],
                             sub_g, lam_init, heads_per_step=DIFF_HEADS, j=j)
            o_l = _diff_attn(q_l.reshape(DEC_BATCH, DEC_SEQ, D_MODEL), kt_l,
                             v_l.reshape(DEC_BATCH, DEC_SEQ, D_MODEL), diff_lambda[j], sub_g,
                             lam_init, heads_per_step=LAT_HEADS_PER_STEP, j=j,
                             caches=(cache_kt, cache_v))
            w_o = diff_w_o[j].astype(BF16)
        else:
            wd = mla_w_down[j]
            w_dq = wd[:, :Q_RANK].astype(BF16)
            w_dkv = jnp.pad(wd[:, Q_RANK:], ((0, 0), (0, LANES - MLA_ROPE))).astype(BF16)
            w_q = jnp.pad(mla_w_q_up[j].reshape(Q_RANK, MLA_HEADS, MLA_QK),
                          ((0, 0), (0, 0), (0, LANES - MLA_QK))).reshape(Q_RANK, MLA_HEADS * LANES).astype(BF16)
            wkv = mla_w_kv_up[j].reshape(KV_RANK, MLA_HEADS, MLA_NOPE + MLA_VD)
            w_k = jnp.pad(wkv[:, :, :MLA_NOPE], ((0, 0), (0, 0), (0, LANES - MLA_NOPE))
                          ).reshape(KV_RANK, MLA_HEADS * LANES).astype(BF16)
            w_v = wkv[:, :, MLA_NOPE:].reshape(KV_RANK, MLA_HEADS * MLA_VD).astype(BF16)
            qg = jnp.pad(mla_q_norm[j] * MLA_Q_SCALE, (0, LANES - MLA_QK)).reshape(1, LANES)
            kg = jnp.pad(mla_k_norm[j], (0, LANES - MLA_QK)).reshape(1, LANES)
            pre = functools.partial(_mla_pre, x, mod_l, g1, w_dq, w_dkv, mla_q_a_norm[j],
                                    mla_kv_a_norm[j], w_q)
            q_c, k_c, v_c, ckv_stack, kpet_stack = pre(lat=False, ctx_weights=(qg, w_k, w_v, kg),
                                                      j=j, stacks=mla_stacks)
            mla_stacks = (ckv_stack, kpet_stack)
            rot = (_swap_pairs(w_q, MLA_ROPE // 4),) + _pair_tables(mla_tabs, qg, MLA_ROPE // 4)
            q_l, ckv_l, kpe_l = pre(lat=True, rot=rot)
            cache_kpe = jnp.pad(cache_mla_kpe[:, j], ((0, 0), (0, 0), (0, LANES - MLA_ROPE)))
            ckv_rows = jnp.concatenate([cache_mla_ckv[:, j], ckv_l.reshape(DEC_BATCH, DEC_SEQ, KV_RANK)],
                                       axis=1).reshape(DEC_BATCH * LAT_KV, KV_RANK)
            kpe_rows = jnp.concatenate([cache_kpe, kpe_l.reshape(DEC_BATCH, DEC_SEQ, LANES)],
                                       axis=1).reshape(DEC_BATCH * LAT_KV, LANES)
            k_a, v_a = _mla_kv_lat(ckv_rows, kpe_rows, w_k, w_v, kg, mla_kv_tabs)
            kw = MLA_HEADS * LANES
            o_c = _mla_attn(q_c.reshape(BATCH, SEQ, kw), k_c.reshape(BATCH, SEQ, kw),
                            v_c.reshape(BATCH, SEQ, D_MODEL), pairs_per_step=MLA_HEADS // 2)
            o_l = _mla_attn(q_l.reshape(DEC_BATCH, DEC_SEQ, kw), k_a.reshape(DEC_BATCH, LAT_KV, kw),
                            v_a.reshape(DEC_BATCH, LAT_KV, D_MODEL),
                            pairs_per_step=LAT_HEADS_PER_STEP)
            w_o = mla_w_o[j].astype(BF16)
        os = (o_c.reshape(N_CTX, D_MODEL), o_l.reshape(N_LAT, D_MODEL))
        last = l == DEPTH - 1
        xs = _ffn((x,), mod_l, norm_g[l, 2], w_in, w_out, 1, os=os, w_o=w_o, split_out=last,
                  nxt=None if last else (ffn_w_in, ffn_w_out, l + 1, 0))
        if not last:
            xs, (w_in, w_out) = xs[:1], xs[1:]

    kt_stack, v_stack = diff_stacks
    ckv_stack, kpet_stack = mla_stacks
    new_diff_k = jnp.transpose(kt_stack.reshape(BATCH, N_DIFF, DIFF_HEADS, 2, DIFF_HD, SEQ),
                               (0, 1, 5, 2, 3, 4))
    new_diff_v = v_stack.reshape(BATCH, N_DIFF, SEQ, DIFF_HEADS, DIFF_VD)
    new_mla_kpe = jnp.transpose(kpet_stack, (0, 1, 3, 2))
    return (xs[0].reshape(BATCH, SEQ, D_MODEL), xs[1].reshape(DEC_BATCH, DEC_SEQ, D_MODEL),
            new_diff_k, new_diff_v, ckv_stack, new_mla_kpe)
```

```python
import functools
import math

import jax
import jax.numpy as jnp
import numpy as np
from jax import lax
from jax.experimental import pallas as pl
from jax.experimental.pallas import tpu as pltpu

D_MODEL = 1024
BATCH = 32
SEQ = 256
DEPTH = 4
DEC_BATCH = 2
DEC_SEQ = 1024
PAST_LEN = 256
GRID_W = 64
N_DIFF = (DEPTH + 1) // 2
N_MLA = DEPTH // 2
N_MOD = 9
D_FF = 2816
DIFF_HEADS = 8
DIFF_HD = 64
DIFF_VD = 128
MLA_HEADS = 16
MLA_NOPE = 64
MLA_ROPE = 32
MLA_QK = MLA_NOPE + MLA_ROPE
MLA_VD = 64
Q_RANK = 768
KV_RANK = 256
ROPE_THETA = 10000.0
EPS = 1e-6
LOG2E = math.log2(math.e)
DIFF_Q_SCALE = DIFF_HD ** -0.5 * LOG2E
MLA_Q_SCALE = MLA_QK ** -0.5 * LOG2E

N_CTX = BATCH * SEQ
N_LAT = DEC_BATCH * DEC_SEQ
N_TOK = N_CTX + N_LAT
LAT_KV = PAST_LEN + DEC_SEQ

LANES = 128
COND_ROWS = 8
TM = 512
SEQS_PER_TILE = TM // SEQ
FF_CHUNK = 256
TQ = 256
LAT_TQ = 512
LAT_HEADS_PER_STEP = 8
CTX_SEQS_PER_STEP = 4
KVP_TM_LAT = 640
MOD_TN = 2304
VMEM_LIMIT = 56 * 1024 * 1024

F32 = jnp.float32
BF16 = jnp.bfloat16


def _params(*sem):
    return pltpu.CompilerParams(dimension_semantics=sem, vmem_limit_bytes=VMEM_LIMIT)


def _dot(a, b):
    return jnp.dot(a, b, preferred_element_type=F32)


def _dot_nt(a, b):
    return lax.dot_general(a, b, (((1,), (1,)), ((), ())), preferred_element_type=F32)


def _modulate(x, g, shift, scale):
    ms = jnp.mean(x * x, axis=-1, keepdims=True)
    return (x * lax.rsqrt(ms + EPS) * g) * (1.0 + scale) + shift


def _rope(x, cos, s_up, s_dn, shift):
    return (x * cos + pltpu.roll(x, LANES - shift, 1) * s_up
            + pltpu.roll(x, shift, 1) * s_dn)


def _swap_pairs(x, half):
    lane = lax.broadcasted_iota(jnp.int32, (1, LANES), 1)
    first = (lane & (2 * half - 1)) < half
    return jnp.where(first, pltpu.roll(x, LANES - half, 1), pltpu.roll(x, half, 1))


def _swap_pair_columns(dst_ref, src_ref, n_cols, half):
    for c in range(0, n_cols, LANES):
        dst_ref[:, c:c + LANES] = _swap_pairs(src_ref[:, c:c + LANES], half)


def _cond_of_tile(i, first_lat_tile, tiles_per_lat_batch):
    lat = jnp.maximum(i - first_lat_tile, 0) // tiles_per_lat_batch
    return jnp.where(i < first_lat_tile, 0, 1 + lat)


def _mod_kernel(c_ref, w_ref, b_ref, o_ref):
    c = c_ref[...]
    s = (c * jax.nn.sigmoid(c)).astype(BF16)
    o_ref[...] = _dot(s, w_ref[...].astype(BF16)) + b_ref[...]


def _modulation(conds, w_mod, b_mod):
    n_out = N_MOD * D_MODEL
    out = pl.pallas_call(
        _mod_kernel,
        grid=(DEPTH, n_out // MOD_TN),
        in_specs=[
            pl.BlockSpec((COND_ROWS, D_MODEL), lambda l, n: (0, 0)),
            pl.BlockSpec((None, D_MODEL, MOD_TN), lambda l, n: (l, 0, n)),
            pl.BlockSpec((None, 1, MOD_TN), lambda l, n: (l, 0, n)),
        ],
        out_specs=pl.BlockSpec((None, COND_ROWS, MOD_TN), lambda l, n: (l, 0, n)),
        out_shape=jax.ShapeDtypeStruct((DEPTH, COND_ROWS, n_out), F32),
        compiler_params=_params("parallel", "parallel"),
        name="modulation",
    )(conds, w_mod, b_mod.reshape(DEPTH, 1, n_out))
    return out.reshape(DEPTH, COND_ROWS, N_MOD, D_MODEL)


FIRST_LAT_TILE = N_CTX // TM
TILES_PER_LAT_BATCH = DEC_SEQ // TM


def _ctx_tile(i):
    return (jnp.minimum(i, FIRST_LAT_TILE - 1), 0)


def _lat_tile(i):
    return (jnp.maximum(i - FIRST_LAT_TILE, 0), 0)


def _pick_rows(i, ctx_ref, lat_ref):
    rows = i * TM + lax.broadcasted_iota(jnp.int32, (TM, 1), 0)
    return jnp.where(rows < N_CTX, ctx_ref[...], lat_ref[...])


def _ffn_kernel(*refs, n_x, proj, n_out, convert, i_shift, i_scale, i_gate, i_pgate):
    refs = list(refs)
    x_refs = [refs.pop(0) for _ in range(n_x)]
    if proj:
        oc_ref, ol_ref, wo_ref = refs[:3]
        refs = refs[3:]
    mod_ref, g_ref, win_ref, wout_ref = refs[:4]
    refs = refs[4:]
    if convert:
        nin_ref, nout_ref = refs[:2]
        refs = refs[2:]
    out_refs = refs[:n_out]
    refs = refs[n_out:]
    if convert:
        nin_bf_ref, nout_bf_ref = refs[:2]
        refs = refs[2:]
    a_ref = refs[0]
    i = pl.program_id(0)
    if convert:
        nin_bf_ref[...] = nin_ref[...].astype(BF16)
        nout_bf_ref[...] = nout_ref[...].astype(BF16)
    x = x_refs[0][...] if n_x == 1 else _pick_rows(i, *x_refs)
    if proj:
        o = _pick_rows(i, oc_ref, ol_ref)
        x = x + mod_ref[i_pgate:i_pgate + 1, :] * _dot(o, wo_ref[...])
    h = _modulate(x, g_ref[...], mod_ref[i_shift:i_shift + 1, :],
                  mod_ref[i_scale:i_scale + 1, :]).astype(BF16)
    for c in range(D_FF // FF_CHUNK):
        lo = c * FF_CHUNK
        g = _dot(h, win_ref[:, lo:lo + FF_CHUNK])
        u = _dot(h, win_ref[:, D_FF + lo:D_FF + lo + FF_CHUNK])
        a_ref[:, lo:lo + FF_CHUNK] = ((g * jax.nn.sigmoid(g)) * u).astype(BF16)
    ff = _dot(a_ref[...], wout_ref[...])
    y = x + mod_ref[i_gate:i_gate + 1, :] * (0.5 * ff)
    if n_out == 1:
        out_refs[0][...] = y
    else:
        @pl.when(i < FIRST_LAT_TILE)
        def _():
            out_refs[0][...] = y

        @pl.when(i >= FIRST_LAT_TILE)
        def _():
            out_refs[1][...] = y


def _ffn(xs, mod_l, g, w_in, w_out, which, os=None, w_o=None, split_out=False, nxt=None):
    proj = os is not None
    convert = nxt is not None
    base = 0 if which == 0 else 6
    row = lambda i: (i, 0)
    full = lambda i: (0, 0)
    tile = (TM, D_MODEL)
    if len(xs) == 1:
        in_specs = [pl.BlockSpec(tile, row)]
    else:
        in_specs = [pl.BlockSpec(tile, _ctx_tile), pl.BlockSpec(tile, _lat_tile)]
    args = list(xs)
    if proj:
        in_specs += [pl.BlockSpec(tile, _ctx_tile), pl.BlockSpec(tile, _lat_tile),
                     pl.BlockSpec((D_MODEL, D_MODEL), full)]
        args += [os[0], os[1], w_o]
    in_specs += [
        pl.BlockSpec((None, N_MOD, D_MODEL),
                     lambda i: (_cond_of_tile(i, FIRST_LAT_TILE, TILES_PER_LAT_BATCH), 0, 0)),
        pl.BlockSpec((1, D_MODEL), full),
        pl.BlockSpec((D_MODEL, 2 * D_FF), full),
        pl.BlockSpec((D_FF, D_MODEL), full),
    ]
    args += [mod_l, g.reshape(1, D_MODEL), w_in, w_out]
    in_slab = (D_MODEL // FIRST_LAT_TILE, 2 * D_FF)
    out_slab = (D_FF // FIRST_LAT_TILE, D_MODEL)
    if convert:
        nw_in, nw_out, nl, nw = nxt
        slab = lambda i: (nl, nw, jnp.minimum(i, FIRST_LAT_TILE - 1), 0)
        in_specs += [pl.BlockSpec((None, None) + in_slab, slab),
                     pl.BlockSpec((None, None) + out_slab, slab)]
        args += [nw_in, nw_out]
    if split_out:
        out_specs = [pl.BlockSpec(tile, _ctx_tile), pl.BlockSpec(tile, _lat_tile)]
        out_shape = [jax.ShapeDtypeStruct((N_CTX, D_MODEL), F32),
                     jax.ShapeDtypeStruct((N_LAT, D_MODEL), F32)]
    else:
        out_specs = [pl.BlockSpec(tile, row)]
        out_shape = [jax.ShapeDtypeStruct((N_TOK, D_MODEL), F32)]
    n_out = len(out_shape)
    if convert:
        out_specs += [pl.BlockSpec(in_slab, _ctx_tile), pl.BlockSpec(out_slab, _ctx_tile)]
        out_shape += [jax.ShapeDtypeStruct((D_MODEL, 2 * D_FF), BF16),
                      jax.ShapeDtypeStruct((D_FF, D_MODEL), BF16)]
    kern = functools.partial(_ffn_kernel, n_x=len(xs), proj=proj, n_out=n_out, convert=convert,
                             i_shift=base, i_scale=base + 1, i_gate=base + 2, i_pgate=5)
    return pl.pallas_call(
        kern,
        grid=(N_TOK // TM,),
        in_specs=in_specs,
        out_specs=out_specs,
        out_shape=out_shape,
        scratch_shapes=[pltpu.VMEM((TM, D_FF), BF16)],
        compiler_params=_params("arbitrary"),
        name="ffn_proj" if proj else "ffn",
    )(*args)


def _diff_pre_kernel(*refs, rope, n_alias):
    refs = list(refs)
    x_ref, mod_ref, g_ref, w_ref = refs[:4]
    refs = refs[4:]
    hw = DIFF_HEADS * DIFF_VD
    half = DIFF_HD // 4
    if rope:
        cos_ref, sgn_ref = refs[:2]
        refs = refs[2:]
    gains = refs[:2]
    refs = refs[2:]
    if rope:
        q_ref, kt_ref, v_ref, wrot_ref = refs

        @pl.when(pl.program_id(0) == 0)
        def _():
            _swap_pair_columns(wrot_ref, w_ref, 2 * hw, half)
    else:
        q_ref, kt_ref, v_ref = refs[n_alias:]
    h = _modulate(x_ref[...], g_ref[...], mod_ref[3:4, :], mod_ref[4:5, :]).astype(BF16)
    lane = lax.broadcasted_iota(jnp.int32, (1, LANES), 1)
    lo = lane < DIFF_HD
    for part in (1, 0):
        gain = gains[part][...]
        y = _dot(h, w_ref[:, part * hw:(part + 1) * hw])
        if rope:
            y_pair = _dot(h, wrot_ref[:, part * hw:(part + 1) * hw])
            g_cos = cos_ref[...] * gain
            g_sin = sgn_ref[...] * _swap_pairs(jnp.broadcast_to(gain, (8, LANES)), half)[0:1]
        for hd in range(DIFF_HEADS):
            sl = slice(hd * LANES, (hd + 1) * LANES)
            yh = y[:, sl]
            sq = yh * yh
            s_lo = jnp.sum(jnp.where(lo, sq, 0.0), axis=-1, keepdims=True)
            s_hi = jnp.sum(jnp.where(lo, 0.0, sq), axis=-1, keepdims=True)
            ms = jnp.where(lo, s_lo, s_hi) * (1.0 / DIFF_HD)
            if rope:
                yn = (yh * g_cos + y_pair[:, sl] * g_sin) * lax.rsqrt(ms + EPS)
            else:
                yn = yh * lax.rsqrt(ms + EPS) * gain
            if part == 0:
                q_ref[:, sl] = yn.astype(q_ref.dtype)
            elif len(kt_ref.shape) == 2:
                kt_ref[sl, :] = yn.T
            else:
                ynt = yn.T
                for b in range(SEQS_PER_TILE):
                    kt_ref[b, sl, :] = ynt[:, b * SEQ:(b + 1) * SEQ]
    v = _dot(h, w_ref[:, 2 * hw:3 * hw])
    if len(v_ref.shape) == 2:
        v_ref[...] = v
    else:
        for b in range(SEQS_PER_TILE):
            v_ref[b] = v[b * SEQ:(b + 1) * SEQ]


def _diff_pre(x, mod_l, g, w_qkv, gains, lat, rot=None, j=0, stacks=None):
    n_rows = N_LAT if lat else N_CTX
    tile0 = N_CTX // TM if lat else 0
    per_lat = DEC_SEQ // TM
    full = lambda i: (0, 0)
    xrow = lambda i: (i + tile0, 0)
    row = lambda i: (i, 0)
    if lat:
        cond = lambda i: (1 + i // per_lat, 0, 0)
    else:
        cond = lambda i: (0, 0, 0)
    in_specs = [
        pl.BlockSpec((TM, D_MODEL), xrow),
        pl.BlockSpec((None, N_MOD, D_MODEL), cond),
        pl.BlockSpec((1, D_MODEL), full),
        pl.BlockSpec((D_MODEL, 3 * D_MODEL), full),
    ]
    args = [x, mod_l, g.reshape(1, D_MODEL), w_qkv]
    aliases = {}
    scratch = []
    if lat:
        pos = lambda i: (i % per_lat, 0)
        in_specs += [pl.BlockSpec((TM, LANES), pos)] * 2 + [pl.BlockSpec((1, LANES), full)] * 2
        args += list(rot) + list(gains)
        scratch = [pltpu.VMEM((D_MODEL, 2 * D_MODEL), BF16)]
        out_specs = [pl.BlockSpec((TM, D_MODEL), row),
                     pl.BlockSpec((None, D_MODEL, TM), lambda i: (i // per_lat, 0, i % per_lat)),
                     pl.BlockSpec((TM, D_MODEL), row)]
        out_shape = [jax.ShapeDtypeStruct((N_LAT, D_MODEL), BF16),
                     jax.ShapeDtypeStruct((DEC_BATCH, D_MODEL, DEC_SEQ), F32),
                     jax.ShapeDtypeStruct((N_LAT, D_MODEL), F32)]
    else:
        in_specs += [pl.BlockSpec((1, LANES), full)] * 2
        args += list(gains)
        slot = lambda i: (i, j, 0, 0)
        out_specs = [pl.BlockSpec((TM, D_MODEL), row),
                     pl.BlockSpec((SEQS_PER_TILE, None, D_MODEL, SEQ), slot),
                     pl.BlockSpec((SEQS_PER_TILE, None, SEQ, D_MODEL), slot)]
        out_shape = [jax.ShapeDtypeStruct((N_CTX, D_MODEL), BF16),
                     jax.ShapeDtypeStruct((BATCH, N_DIFF, D_MODEL, SEQ), F32),
                     jax.ShapeDtypeStruct((BATCH, N_DIFF, SEQ, D_MODEL), F32)]
        if stacks is not None:
            aliases = {len(args): 1, len(args) + 1: 2}
            in_specs += [pl.BlockSpec(memory_space=pl.ANY)] * 2
            args += list(stacks)
    return pl.pallas_call(
        functools.partial(_diff_pre_kernel, rope=lat, n_alias=len(aliases)),
        grid=(n_rows // TM,),
        in_specs=in_specs,
        out_specs=out_specs,
        out_shape=out_shape,
        input_output_aliases=aliases,
        scratch_shapes=scratch,
        compiler_params=_params("arbitrary" if lat else "parallel"),
        name="diff_pre_lat" if lat else "diff_pre_ctx",
    )(*args)


def _softmax_pv(s, v_aug):
    e = jnp.exp2(s - jnp.max(s, axis=-1, keepdims=True)).astype(BF16)
    r = _dot(e, v_aug)
    return r[:, :LANES] * (1.0 / r[:, LANES:])


def _diff_attn_kernel(*refs, heads, lam_init, cached):
    if cached:
        q_ref, ktc_ref, kt_ref, vc_ref, v_ref, lam_ref, sub_ref, o_ref = refs
    else:
        q_ref, kt_ref, v_ref, lam_ref, sub_ref, o_ref = refs
    lp = lam_ref[...]
    lam = (jnp.exp(jnp.sum(lp[0:1, :] * lp[1:2, :], axis=-1, keepdims=True))
           - jnp.exp(jnp.sum(lp[2:3, :] * lp[3:4, :], axis=-1, keepdims=True)) + lam_init)
    n_b, tq = q_ref.shape[:2]
    lane = lax.broadcasted_iota(jnp.int32, (1, LANES), 1)
    lo = lane < DIFF_HD
    sub_g = sub_ref[...]
    items = [(b, hd) for b in range(n_b) for hd in range(heads)]

    def scores(item):
        b, hd = item
        sl = slice(hd * LANES, (hd + 1) * LANES)
        qh = q_ref[b, :, sl]
        kt = kt_ref[b, sl, :].astype(BF16)
        if cached:
            kt = jnp.concatenate([ktc_ref[b, sl, :].astype(BF16), kt], axis=1)
        zero = jnp.zeros_like(qh)
        qq = jnp.concatenate([jnp.where(lo, qh, zero), jnp.where(lo, zero, qh)], axis=0)
        return _dot(qq, kt)

    s_next = scores(items[0])
    for n, (b, hd) in enumerate(items):
        sl = slice(hd * LANES, (hd + 1) * LANES)
        s = s_next
        if n + 1 < len(items):
            s_next = scores(items[n + 1])
        vh = v_ref[b, :, sl].astype(BF16)
        if cached:
            vh = jnp.concatenate([vc_ref[b, :, sl].astype(BF16), vh], axis=0)
        v_aug = jnp.concatenate([vh, jnp.ones_like(vh)], axis=1)
        o12 = _softmax_pv(s, v_aug)
        o = o12[:tq] - lam * o12[tq:]
        ms = jnp.mean(o * o, axis=-1, keepdims=True)
        o = (o * lax.rsqrt(ms + EPS) * sub_g) * (1.0 - lam_init)
        o_ref[b, :, sl] = o.astype(o_ref.dtype)


def _diff_attn(q, kt, v, lam_p, sub_g, lam_init, heads_per_step, j, caches=None):
    b, lq, _ = q.shape
    cached = caches is not None
    nb = 1 if cached else CTX_SEQS_PER_STEP
    w = heads_per_step * LANES
    n_hg = DIFF_HEADS // heads_per_step
    tq = min(TQ, lq) if lq == SEQ else LAT_TQ
    qmap = lambda bi, hi, qi: (bi, qi, hi)
    full = lambda bi, hi, qi: (0, 0)
    q_spec = pl.BlockSpec((nb, tq, w), qmap)
    kt_slot = lambda bi, hi, qi: (bi, j, hi, 0)
    v_slot = lambda bi, hi, qi: (bi, j, 0, hi)
    if cached:
        kt_cache, v_cache = caches
        in_specs = [q_spec,
                    pl.BlockSpec((nb, None, w, PAST_LEN), kt_slot),
                    pl.BlockSpec((nb, w, lq), lambda bi, hi, qi: (bi, hi, 0)),
                    pl.BlockSpec((nb, None, PAST_LEN, w), v_slot),
                    pl.BlockSpec((nb, lq, w), lambda bi, hi, qi: (bi, 0, hi))]
        args = [q, kt_cache, kt, v_cache, v]
    else:
        in_specs = [q_spec,
                    pl.BlockSpec((nb, None, w, lq), kt_slot),
                    pl.BlockSpec((nb, None, lq, w), v_slot)]
        args = [q, kt, v]
    in_specs += [pl.BlockSpec((4, DIFF_HD), full), pl.BlockSpec((1, LANES), full)]
    args += [lam_p, sub_g]
    return pl.pallas_call(
        functools.partial(_diff_attn_kernel, heads=heads_per_step, lam_init=lam_init, cached=cached),
        grid=(b // nb, n_hg, lq // tq),
        in_specs=in_specs,
        out_specs=pl.BlockSpec((nb, tq, w), qmap),
        out_shape=jax.ShapeDtypeStruct((b, lq, D_MODEL), BF16),
        compiler_params=_params("parallel", "parallel", "parallel"),
        name="diff_attn_lat" if cached else "diff_attn_ctx",
    )(*args)


def _mla_keys(ckv_n, kpe, wk_ref, wv_ref, kg, rope_refs, k_ref, v_ref):
    c = ckv_n.astype(BF16)
    kk = _dot(c, wk_ref[...])
    kpe = pltpu.roll(kpe, MLA_NOPE, 1)
    pe_ss = jnp.sum(kpe * kpe, axis=-1, keepdims=True)
    kpe_g = kpe * kg
    if rope_refs is not None:
        cos_ref, up_ref, dn_ref = rope_refs
        kpe_g = _rope(kpe_g, cos_ref[...], up_ref[...], dn_ref[...], MLA_ROPE // 4)
    for hd in range(MLA_HEADS):
        sl = slice(hd * LANES, (hd + 1) * LANES)
        kh = kk[:, sl]
        ms = (jnp.sum(kh * kh, axis=-1, keepdims=True) + pe_ss) * (1.0 / MLA_QK)
        kn = (kh * kg + kpe_g) * lax.rsqrt(ms + EPS)
        k_ref[:, sl] = kn.astype(k_ref.dtype)
    v_ref[...] = _dot(c, wv_ref[...]).astype(v_ref.dtype)


def _mla_pre_kernel(*refs, lat, n_alias):
    refs = list(refs)
    x_ref, mod_ref, g_ref, wdq_ref, wdkv_ref, qag_ref, kvag_ref, wq_ref = refs[:8]
    refs = refs[8:]
    half = MLA_ROPE // 4
    if lat:
        cos_ref, sgn_ref, qg_ref = refs[:3]
        q_ref, ckv_ref, kpe_ref, wqrot_ref = refs[3:]

        @pl.when(pl.program_id(0) == 0)
        def _():
            _swap_pair_columns(wqrot_ref, wq_ref, MLA_HEADS * LANES, half)
    else:
        qg_ref, wk_ref, wv_ref, kg_ref = refs[:4]
        q_ref, k_ref, v_ref, ckvs_ref, kpet_ref = refs[4 + n_alias:]
    h = _modulate(x_ref[...], g_ref[...], mod_ref[3:4, :], mod_ref[4:5, :]).astype(BF16)
    d2 = _dot(h, wdkv_ref[...])
    ckv = d2[:, :KV_RANK]
    ms = jnp.mean(ckv * ckv, axis=-1, keepdims=True)
    ckv_n = ckv * lax.rsqrt(ms + EPS) * kvag_ref[...]
    kpe = d2[:, KV_RANK:]
    if lat:
        ckv_ref[...] = ckv_n
        kpe_ref[...] = kpe
    cq = _dot(h, wdq_ref[...])
    ms = jnp.mean(cq * cq, axis=-1, keepdims=True)
    cqn = (cq * lax.rsqrt(ms + EPS) * qag_ref[...]).astype(BF16)
    q = _dot(cqn, wq_ref[...])
    qg = qg_ref[...]
    if lat:
        q_pair = _dot(cqn, wqrot_ref[...])
        g_cos = cos_ref[...] * qg
        g_sin = sgn_ref[...] * _swap_pairs(jnp.broadcast_to(qg, (8, LANES)), half)[0:1]
    for hd in range(MLA_HEADS):
        sl = slice(hd * LANES, (hd + 1) * LANES)
        qh = q[:, sl]
        ms = jnp.sum(qh * qh, axis=-1, keepdims=True) * (1.0 / MLA_QK)
        if lat:
            qn = (qh * g_cos + q_pair[:, sl] * g_sin) * lax.rsqrt(ms + EPS)
        else:
            qn = qh * lax.rsqrt(ms + EPS) * qg
        q_ref[:, sl] = qn.astype(q_ref.dtype)
    if not lat:
        kpe_t = kpe.T
        for b in range(SEQS_PER_TILE):
            ckvs_ref[b] = ckv_n[b * SEQ:(b + 1) * SEQ]
            kpet_ref[b] = kpe_t[:MLA_ROPE, b * SEQ:(b + 1) * SEQ]
        _mla_keys(ckv_n, kpe, wk_ref, wv_ref, kg_ref[...], None, k_ref, v_ref)


def _mla_pre(x, mod_l, g, w_dq, w_dkv, qag, kvag, w_q, lat, rot=None, ctx_weights=None,
             j=0, stacks=None):
    n_rows = N_LAT if lat else N_CTX
    tile0 = N_CTX // TM if lat else 0
    per_lat = DEC_SEQ // TM
    full = lambda i: (0, 0)
    xrow = lambda i: (i + tile0, 0)
    row = lambda i: (i, 0)
    if lat:
        cond = lambda i: (1 + i // per_lat, 0, 0)
    else:
        cond = lambda i: (0, 0, 0)
    kvw = KV_RANK + LANES
    qw = MLA_HEADS * LANES
    vw = MLA_HEADS * MLA_VD
    in_specs = [
        pl.BlockSpec((TM, D_MODEL), xrow),
        pl.BlockSpec((None, N_MOD, D_MODEL), cond),
        pl.BlockSpec((1, D_MODEL), full),
        pl.BlockSpec((D_MODEL, Q_RANK), full),
        pl.BlockSpec((D_MODEL, kvw), full),
        pl.BlockSpec((1, Q_RANK), full),
        pl.BlockSpec((1, KV_RANK), full),
        pl.BlockSpec((Q_RANK, qw), full),
    ]
    args = [x, mod_l, g.reshape(1, D_MODEL), w_dq, w_dkv, qag.reshape(1, Q_RANK),
            kvag.reshape(1, KV_RANK), w_q]
    aliases = {}
    scratch = []
    if lat:
        pos = lambda i: (i % per_lat, 0)
        in_specs += [pl.BlockSpec((TM, LANES), pos)] * 2 + [pl.BlockSpec((1, LANES), full)]
        args += list(rot)
        scratch = [pltpu.VMEM((Q_RANK, qw), BF16)]
        out_specs = [pl.BlockSpec((TM, qw), row), pl.BlockSpec((TM, KV_RANK), row),
                     pl.BlockSpec((TM, LANES), row)]
        out_shape = [jax.ShapeDtypeStruct((n_rows, qw), BF16),
                     jax.ShapeDtypeStruct((n_rows, KV_RANK), F32),
                     jax.ShapeDtypeStruct((n_rows, LANES), F32)]
    else:
        qg, w_k, w_v, kg = ctx_weights
        in_specs += [pl.BlockSpec((1, LANES), full), pl.BlockSpec((KV_RANK, qw), full),
                     pl.BlockSpec((KV_RANK, vw), full), pl.BlockSpec((1, LANES), full)]
        args += [qg, w_k, w_v, kg]
        slot = lambda i: (i, j, 0, 0)
        out_specs = [pl.BlockSpec((TM, qw), row), pl.BlockSpec((TM, qw), row),
                     pl.BlockSpec((TM, vw), row),
                     pl.BlockSpec((SEQS_PER_TILE, None, SEQ, KV_RANK), slot),
                     pl.BlockSpec((SEQS_PER_TILE, None, MLA_ROPE, SEQ), slot)]
        out_shape = [jax.ShapeDtypeStruct((n_rows, qw), BF16),
                     jax.ShapeDtypeStruct((n_rows, qw), BF16),
                     jax.ShapeDtypeStruct((n_rows, vw), BF16),
                     jax.ShapeDtypeStruct((BATCH, N_MLA, SEQ, KV_RANK), F32),
                     jax.ShapeDtypeStruct((BATCH, N_MLA, MLA_ROPE, SEQ), F32)]
        if stacks is not None:
            aliases = {len(args): 3, len(args) + 1: 4}
            in_specs += [pl.BlockSpec(memory_space=pl.ANY)] * 2
            args += list(stacks)
    return pl.pallas_call(
        functools.partial(_mla_pre_kernel, lat=lat, n_alias=len(aliases)),
        grid=(n_rows // TM,),
        in_specs=in_specs,
        out_specs=out_specs,
        out_shape=out_shape,
        input_output_aliases=aliases,
        scratch_shapes=scratch,
        compiler_params=_params("arbitrary" if lat else "parallel"),
        name="mla_pre_lat" if lat else "mla_pre_ctx",
    )(*args)


def _mla_kv_kernel(ckv_ref, kpe_ref, wk_ref, wv_ref, kg_ref, cos_ref, up_ref, dn_ref, k_ref, v_ref):
    _mla_keys(ckv_ref[...], kpe_ref[...], wk_ref, wv_ref, kg_ref[...],
              (cos_ref, up_ref, dn_ref), k_ref, v_ref)


def _mla_kv_lat(ckv_rows, kpe_rows, w_k, w_v, kg, tables):
    n_rows = ckv_rows.shape[0]
    tm = KVP_TM_LAT
    per = LAT_KV // tm
    full = lambda i: (0, 0)
    row = lambda i: (i, 0)
    pos = lambda i: (i % per, 0)
    kw = MLA_HEADS * LANES
    vw = MLA_HEADS * MLA_VD
    return pl.pallas_call(
        _mla_kv_kernel,
        grid=(n_rows // tm,),
        in_specs=[
            pl.BlockSpec((tm, KV_RANK), row),
            pl.BlockSpec((tm, LANES), row),
            pl.BlockSpec((KV_RANK, kw), full),
            pl.BlockSpec((KV_RANK, vw), full),
            pl.BlockSpec((1, LANES), full),
        ] + [pl.BlockSpec((tm, LANES), pos)] * 3,
        out_specs=[pl.BlockSpec((tm, kw), row), pl.BlockSpec((tm, vw), row)],
        out_shape=[jax.ShapeDtypeStruct((n_rows, kw), BF16),
                   jax.ShapeDtypeStruct((n_rows, vw), BF16)],
        compiler_params=_params("parallel"),
        name="mla_kv_lat",
    )(ckv_rows, kpe_rows, w_k, w_v, kg, *tables)


def _mla_attn_kernel(q_ref, k_ref, v_ref, o_ref, *, pairs):
    lane = lax.broadcasted_iota(jnp.int32, (1, LANES), 1)
    lo = lane < MLA_VD

    items = [(b, hd) for b in range(q_ref.shape[0]) for hd in range(2 * pairs)]

    def scores(item):
        b, hd = item
        sl = slice(hd * LANES, (hd + 1) * LANES)
        return _dot_nt(q_ref[b, :, sl], k_ref[b, :, sl])

    s_next = scores(items[0])
    outs = []
    for n, (b, hd) in enumerate(items):
        s = s_next
        if n + 1 < len(items):
            s_next = scores(items[n + 1])
        pl_ = slice((hd // 2) * LANES, (hd // 2 + 1) * LANES)
        vp = v_ref[b, :, pl_]
        outs.append(_softmax_pv(s, jnp.concatenate([vp, jnp.ones_like(vp)], axis=1)))
        if hd % 2 == 1:
            o_ref[b, :, pl_] = jnp.where(lo, outs[0], outs[1]).astype(o_ref.dtype)
            outs = []


def _mla_attn(q, k, v, pairs_per_step):
    b, lq, _ = q.shape
    lk = k.shape[1]
    nb = CTX_SEQS_PER_STEP if lq == SEQ else 1
    n_pg = MLA_HEADS // 2 // pairs_per_step
    tq = min(TQ, lq) if lq == SEQ else LAT_TQ
    qkw = pairs_per_step * 2 * LANES
    ow = pairs_per_step * LANES
    qmap = lambda bi, hi, qi: (bi, qi, hi)
    kmap = lambda bi, hi, qi: (bi, 0, hi)
    return pl.pallas_call(
        functools.partial(_mla_attn_kernel, pairs=pairs_per_step),
        grid=(b // nb, n_pg, lq // tq),
        in_specs=[
            pl.BlockSpec((nb, tq, qkw), qmap),
            pl.BlockSpec((nb, lk, qkw), kmap),
            pl.BlockSpec((nb, lk, ow), kmap),
        ],
        out_specs=pl.BlockSpec((nb, tq, ow), qmap),
        out_shape=jax.ShapeDtypeStruct((b, lq, D_MODEL), BF16),
        compiler_params=_params("parallel", "parallel", "parallel"),
        name="mla_attn_lat" if lq == DEC_SEQ else "mla_attn_ctx",
    )(q, k, v)


def _rope_tables(n_rot, lane0, identity_rows):
    rows = DEC_SEQ // GRID_W
    row = np.repeat(np.arange(rows), GRID_W)
    col = np.tile(np.arange(GRID_W), rows)
    n = n_rot // 2
    freqs = ROPE_THETA ** (-np.arange(0, n, 2, dtype=np.float64) / n)
    zeros = np.zeros((DEC_SEQ, n // 2))
    cos_parts, up_parts, dn_parts = [], [], []
    for pos in (row, col):
        ang = pos.astype(np.float64)[:, None] * freqs[None, :]
        c, s = np.cos(ang), np.sin(ang)
        cos_parts += [c, c]
        up_parts += [-s, zeros]
        dn_parts += [zeros, s]

    def place(parts, fill):
        reps = 2 if lane0 == 0 else 1
        body = np.concatenate(parts * reps, axis=1)
        full = np.full((identity_rows + DEC_SEQ, LANES), fill)
        full[identity_rows:, lane0:lane0 + body.shape[1]] = body
        return jnp.asarray(full, dtype=F32)

    return place(cos_parts, 1.0), place(up_parts, 0.0), place(dn_parts, 0.0)


def _signed_tables(tables):
    cos, up, dn = tables
    return cos, up + dn


def kernel(x_prompt, x_sample, c, cache_diff_k, cache_diff_v, cache_mla_ckv, cache_mla_kpe, c_ctx, w_mod, b_mod, norm_g, ffn_w_in, ffn_w_out, diff_w_qkv, diff_q_norm, diff_k_norm, diff_lambda, diff_subln, diff_w_o, mla_w_down, mla_q_a_norm, mla_kv_a_norm, mla_w_q_up, mla_w_kv_up, mla_q_norm, mla_k_norm, mla_w_o):
    xs = (x_prompt.reshape(N_CTX, D_MODEL), x_sample.reshape(N_LAT, D_MODEL))
    conds = jnp.concatenate(
        [c_ctx[None, :], c, jnp.zeros((COND_ROWS - 1 - DEC_BATCH, D_MODEL), F32)], axis=0)
    mod = _modulation(conds, w_mod, b_mod)

    w_in = ffn_w_in[0, 0].astype(BF16)
    w_out = ffn_w_out[0, 0].astype(BF16)
    diff_rot = _signed_tables(_rope_tables(DIFF_HD, 0, 0))
    mla_rot = _signed_tables(_rope_tables(MLA_ROPE, MLA_NOPE, 0))
    mla_kv_tabs = _rope_tables(MLA_ROPE, MLA_NOPE, PAST_LEN)
    cache_kt = jnp.transpose(cache_diff_k, (0, 1, 3, 4, 5, 2)).reshape(
        DEC_BATCH, N_DIFF, D_MODEL, PAST_LEN)
    cache_v = cache_diff_v.reshape(DEC_BATCH, N_DIFF, PAST_LEN, D_MODEL)

    diff_stacks, mla_stacks = None, None
    for l in range(DEPTH):
        j = l // 2
        mod_l = mod[l]
        x, w_in, w_out = _ffn(xs, mod_l, norm_g[l, 0], w_in, w_out, 0,
                              nxt=(ffn_w_in, ffn_w_out, l, 1))
        g1 = norm_g[l, 1]
        if l % 2 == 0:
            w_qkv = diff_w_qkv[j].astype(BF16)
            qg = jnp.tile(diff_q_norm[j] * DIFF_Q_SCALE, 2).reshape(1, LANES)
            kg = jnp.tile(diff_k_norm[j], 2).reshape(1, LANES)
            sub_g = diff_subln[j].reshape(1, LANES)
            lam_init = 0.8 - 0.6 * math.exp(-0.3 * l)
            q_c, kt_stack, v_stack = _diff_pre(x, mod_l, g1, w_qkv, (qg, kg), lat=False,
                                               j=j, stacks=diff_stacks)
            diff_stacks = (kt_stack, v_stack)
            q_l, kt_l, v_l = _diff_pre(x, mod_l, g1, w_qkv, (qg, kg), lat=True, rot=diff_rot)
            o_c = _diff_attn(q_c.reshape(BATCH, SEQ, D_MODEL), kt_stack, v_stack, diff_lambda[j],
                             sub_g, lam_init, heads_per_step=DIFF_HEADS, j=j)
            o_l = _diff_attn(q_l.reshape(DEC_BATCH, DEC_SEQ, D_MODEL), kt_l,
                             v_l.reshape(DEC_BATCH, DEC_SEQ, D_MODEL), diff_lambda[j], sub_g,
                             lam_init, heads_per_step=LAT_HEADS_PER_STEP, j=j,
                             caches=(cache_kt, cache_v))
            w_o = diff_w_o[j].astype(BF16)
        else:
            wd = mla_w_down[j]
            w_dq = wd[:, :Q_RANK].astype(BF16)
            w_dkv = jnp.pad(wd[:, Q_RANK:], ((0, 0), (0, LANES - MLA_ROPE))).astype(BF16)
            w_q = jnp.pad(mla_w_q_up[j].reshape(Q_RANK, MLA_HEADS, MLA_QK),
                          ((0, 0), (0, 0), (0, LANES - MLA_QK))).reshape(Q_RANK, MLA_HEADS * LANES).astype(BF16)
            wkv = mla_w_kv_up[j].reshape(KV_RANK, MLA_HEADS, MLA_NOPE + MLA_VD)
            w_k = jnp.pad(wkv[:, :, :MLA_NOPE], ((0, 0), (0, 0), (0, LANES - MLA_NOPE))
                          ).reshape(KV_RANK, MLA_HEADS * LANES).astype(BF16)
            w_v = wkv[:, :, MLA_NOPE:].reshape(KV_RANK, MLA_HEADS * MLA_VD).astype(BF16)
            qg = jnp.pad(mla_q_norm[j] * MLA_Q_SCALE, (0, LANES - MLA_QK)).reshape(1, LANES)
            kg = jnp.pad(mla_k_norm[j], (0, LANES - MLA_QK)).reshape(1, LANES)
            pre = functools.partial(_mla_pre, x, mod_l, g1, w_dq, w_dkv, mla_q_a_norm[j],
                                    mla_kv_a_norm[j], w_q)
            q_c, k_c, v_c, ckv_stack, kpet_stack = pre(lat=False, ctx_weights=(qg, w_k, w_v, kg),
                                                      j=j, stacks=mla_stacks)
            mla_stacks = (ckv_stack, kpet_stack)
            q_l, ckv_l, kpe_l = pre(lat=True, rot=mla_rot + (qg,))
            cache_kpe = jnp.pad(cache_mla_kpe[:, j], ((0, 0), (0, 0), (0, LANES - MLA_ROPE)))
            ckv_rows = jnp.concatenate([cache_mla_ckv[:, j], ckv_l.reshape(DEC_BATCH, DEC_SEQ, KV_RANK)],
                                       axis=1).reshape(DEC_BATCH * LAT_KV, KV_RANK)
            kpe_rows = jnp.concatenate([cache_kpe, kpe_l.reshape(DEC_BATCH, DEC_SEQ, LANES)],
                                       axis=1).reshape(DEC_BATCH * LAT_KV, LANES)
            k_a, v_a = _mla_kv_lat(ckv_rows, kpe_rows, w_k, w_v, kg, mla_kv_tabs)
            kw = MLA_HEADS * LANES
            o_c = _mla_attn(q_c.reshape(BATCH, SEQ, kw), k_c.reshape(BATCH, SEQ, kw),
                            v_c.reshape(BATCH, SEQ, D_MODEL), pairs_per_step=MLA_HEADS // 2)
            o_l = _mla_attn(q_l.reshape(DEC_BATCH, DEC_SEQ, kw), k_a.reshape(DEC_BATCH, LAT_KV, kw),
                            v_a.reshape(DEC_BATCH, LAT_KV, D_MODEL),
                            pairs_per_step=LAT_HEADS_PER_STEP)
            w_o = mla_w_o[j].astype(BF16)
        os = (o_c.reshape(N_CTX, D_MODEL), o_l.reshape(N_LAT, D_MODEL))
        last = l == DEPTH - 1
        xs = _ffn((x,), mod_l, norm_g[l, 2], w_in, w_out, 1, os=os, w_o=w_o, split_out=last,
                  nxt=None if last else (ffn_w_in, ffn_w_out, l + 1, 0))
        if not last:
            xs, (w_in, w_out) = xs[:1], xs[1:]

    kt_stack, v_stack = diff_stacks
    ckv_stack, kpet_stack = mla_stacks
    new_diff_k = jnp.transpose(kt_stack.reshape(BATCH, N_DIFF, DIFF_HEADS, 2, DIFF_HD, SEQ),
                               (0, 1, 5, 2, 3, 4))
    new_diff_v = v_stack.reshape(BATCH, N_DIFF, SEQ, DIFF_HEADS, DIFF_VD)
    new_mla_kpe = jnp.transpose(kpet_stack, (0, 1, 3, 2))
    return (xs[0].reshape(BATCH, SEQ, D_MODEL), xs[1].reshape(DEC_BATCH, DEC_SEQ, D_MODEL),
            new_diff_k, new_diff_v, ckv_stack, new_mla_kpe)
```

```python
import functools
import math

import jax
import jax.numpy as jnp
import numpy as np
from jax import lax
from jax.experimental import pallas as pl
from jax.experimental.pallas import tpu as pltpu

D_MODEL = 1024
BATCH = 32
SEQ = 256
DEPTH = 4
DEC_BATCH = 2
DEC_SEQ = 1024
PAST_LEN = 256
GRID_W = 64
N_DIFF = (DEPTH + 1) // 2
N_MLA = DEPTH // 2
N_MOD = 9
D_FF = 2816
DIFF_HEADS = 8
DIFF_HD = 64
DIFF_VD = 128
MLA_HEADS = 16
MLA_NOPE = 64
MLA_ROPE = 32
MLA_QK = MLA_NOPE + MLA_ROPE
MLA_VD = 64
Q_RANK = 768
KV_RANK = 256
ROPE_THETA = 10000.0
EPS = 1e-6
LOG2E = math.log2(math.e)
DIFF_Q_SCALE = DIFF_HD ** -0.5 * LOG2E
MLA_Q_SCALE = MLA_QK ** -0.5 * LOG2E

N_CTX = BATCH * SEQ
N_LAT = DEC_BATCH * DEC_SEQ
N_TOK = N_CTX + N_LAT
LAT_KV = PAST_LEN + DEC_SEQ

LANES = 128
SUBLANES = 8
COND_ROWS = SUBLANES
TM = 512
SEQS_PER_TILE = TM // SEQ
FF_CHUNK = 256
TQ = 256
LAT_TQ = 512
LAT_HEADS_PER_STEP = 8
CTX_SEQS_PER_STEP = 4
MLA_CTX_SEQS_PER_STEP = 8
KVP_TM_LAT = 640
MOD_TN = 2304
VMEM_LIMIT = 56 * 1024 * 1024

F32 = jnp.float32
BF16 = jnp.bfloat16


def _params(*sem):
    return pltpu.CompilerParams(dimension_semantics=sem, vmem_limit_bytes=VMEM_LIMIT)


def _dot(a, b):
    return jnp.dot(a, b, preferred_element_type=F32)


def _dot_nt(a, b):
    return lax.dot_general(a, b, (((1,), (1,)), ((), ())), preferred_element_type=F32)


def _modulate(x, g, shift, scale):
    ms = jnp.mean(x * x, axis=-1, keepdims=True)
    return (x * lax.rsqrt(ms + EPS) * g) * (1.0 + scale) + shift


def _rope(x, cos, s_up, s_dn, shift):
    return (x * cos + pltpu.roll(x, LANES - shift, 1) * s_up
            + pltpu.roll(x, shift, 1) * s_dn)


def _swap_pairs(x, half):
    lane = lax.broadcasted_iota(jnp.int32, (1, LANES), 1)
    first = (lane & (2 * half - 1)) < half
    return jnp.where(first, pltpu.roll(x, LANES - half, 1), pltpu.roll(x, half, 1))


def _swap_pair_columns(dst_ref, src_ref, n_cols, half):
    for c in range(0, n_cols, LANES):
        dst_ref[:, c:c + LANES] = _swap_pairs(src_ref[:, c:c + LANES], half)


def _cond_of_tile(i, first_lat_tile, tiles_per_lat_batch):
    lat = jnp.maximum(i - first_lat_tile, 0) // tiles_per_lat_batch
    return jnp.where(i < first_lat_tile, 0, 1 + lat)


def _mod_kernel(c_ref, w_ref, b_ref, o_ref):
    c = c_ref[...]
    s = (c * jax.nn.sigmoid(c)).astype(BF16)
    o_ref[...] = _dot(s, w_ref[...].astype(BF16)) + b_ref[...]


def _modulation(conds, w_mod, b_mod):
    n_out = N_MOD * D_MODEL
    out = pl.pallas_call(
        _mod_kernel,
        grid=(DEPTH, n_out // MOD_TN),
        in_specs=[
            pl.BlockSpec((COND_ROWS, D_MODEL), lambda l, n: (0, 0)),
            pl.BlockSpec((None, D_MODEL, MOD_TN), lambda l, n: (l, 0, n)),
            pl.BlockSpec((None, 1, MOD_TN), lambda l, n: (l, 0, n)),
        ],
        out_specs=pl.BlockSpec((None, COND_ROWS, MOD_TN), lambda l, n: (l, 0, n)),
        out_shape=jax.ShapeDtypeStruct((DEPTH, COND_ROWS, n_out), F32),
        compiler_params=_params("parallel", "parallel"),
        name="modulation",
    )(conds, w_mod, b_mod.reshape(DEPTH, 1, n_out))
    return out.reshape(DEPTH, COND_ROWS, N_MOD, D_MODEL)


FIRST_LAT_TILE = N_CTX // TM
TILES_PER_LAT_BATCH = DEC_SEQ // TM


def _ctx_tile(i):
    return (jnp.minimum(i, FIRST_LAT_TILE - 1), 0)


def _lat_tile(i):
    return (jnp.maximum(i - FIRST_LAT_TILE, 0), 0)


def _pick_rows(i, ctx_ref, lat_ref):
    rows = i * TM + lax.broadcasted_iota(jnp.int32, (TM, 1), 0)
    return jnp.where(rows < N_CTX, ctx_ref[...], lat_ref[...])


def _ffn_kernel(*refs, n_x, proj, n_out, n_jobs, i_shift, i_scale, i_gate, i_pgate):
    refs = list(refs)
    x_refs = [refs.pop(0) for _ in range(n_x)]
    if proj:
        oc_ref, ol_ref, wo_ref = refs[:3]
        refs = refs[3:]
    mod_ref, g_ref, win_ref, wout_ref = refs[:4]
    refs = refs[4:]
    job_refs = refs[:n_jobs]
    refs = refs[n_jobs:]
    out_refs = refs[:n_out]
    refs = refs[n_out:]
    job_out_refs = refs[:n_jobs]
    a_ref = refs[n_jobs]
    i = pl.program_id(0)
    for src_ref, dst_ref in zip(job_refs, job_out_refs):
        dst_ref[...] = src_ref[...].astype(BF16)
    x = x_refs[0][...] if n_x == 1 else _pick_rows(i, *x_refs)
    if proj:
        o = _pick_rows(i, oc_ref, ol_ref)
        x = x + mod_ref[i_pgate:i_pgate + 1, :] * _dot(o, wo_ref[...])
    h = _modulate(x, g_ref[...], mod_ref[i_shift:i_shift + 1, :],
                  mod_ref[i_scale:i_scale + 1, :]).astype(BF16)
    for c in range(D_FF // FF_CHUNK):
        lo = c * FF_CHUNK
        g = _dot(h, win_ref[:, lo:lo + FF_CHUNK])
        u = _dot(h, win_ref[:, D_FF + lo:D_FF + lo + FF_CHUNK])
        a_ref[:, lo:lo + FF_CHUNK] = ((g * jax.nn.sigmoid(g)) * u).astype(BF16)
    ff = _dot(a_ref[...], wout_ref[...])
    y = x + mod_ref[i_gate:i_gate + 1, :] * (0.5 * ff)
    if n_out == 1:
        out_refs[0][...] = y
    else:
        @pl.when(i < FIRST_LAT_TILE)
        def _():
            out_refs[0][...] = y

        @pl.when(i >= FIRST_LAT_TILE)
        def _():
            out_refs[1][...] = y


def _ffn(xs, mod_l, g, w_in, w_out, which, os=None, w_o=None, split_out=False, jobs=()):
    proj = os is not None
    base = 0 if which == 0 else 6
    row = lambda i: (i, 0)
    full = lambda i: (0, 0)
    tile = (TM, D_MODEL)
    if len(xs) == 1:
        in_specs = [pl.BlockSpec(tile, row)]
    else:
        in_specs = [pl.BlockSpec(tile, _ctx_tile), pl.BlockSpec(tile, _lat_tile)]
    args = list(xs)
    if proj:
        in_specs += [pl.BlockSpec(tile, _ctx_tile), pl.BlockSpec(tile, _lat_tile),
                     pl.BlockSpec((D_MODEL, D_MODEL), full)]
        args += [os[0], os[1], w_o]
    in_specs += [
        pl.BlockSpec((None, N_MOD, D_MODEL),
                     lambda i: (_cond_of_tile(i, FIRST_LAT_TILE, TILES_PER_LAT_BATCH), 0, 0)),
        pl.BlockSpec((1, D_MODEL), full),
        pl.BlockSpec((D_MODEL, 2 * D_FF), full),
        pl.BlockSpec((D_FF, D_MODEL), full),
    ]
    args += [mod_l, g.reshape(1, D_MODEL), w_in, w_out]
    job_out_specs, job_out_shape = [], []
    for arr, lead in jobs:
        rows, cols = arr.shape[-2:]
        slab = (rows // FIRST_LAT_TILE, cols)
        in_specs += [pl.BlockSpec((None,) * len(lead) + slab,
                                  lambda i, lead=lead: tuple(lead) + _ctx_tile(i))]
        args += [arr]
        job_out_specs += [pl.BlockSpec(slab, _ctx_tile)]
        job_out_shape += [jax.ShapeDtypeStruct((rows, cols), BF16)]
    if split_out:
        out_specs = [pl.BlockSpec(tile, _ctx_tile), pl.BlockSpec(tile, _lat_tile)]
        out_shape = [jax.ShapeDtypeStruct((N_CTX, D_MODEL), F32),
                     jax.ShapeDtypeStruct((N_LAT, D_MODEL), F32)]
    else:
        out_specs = [pl.BlockSpec(tile, row)]
        out_shape = [jax.ShapeDtypeStruct((N_TOK, D_MODEL), F32)]
    n_out = len(out_shape)
    kern = functools.partial(_ffn_kernel, n_x=len(xs), proj=proj, n_out=n_out, n_jobs=len(jobs),
                             i_shift=base, i_scale=base + 1, i_gate=base + 2, i_pgate=5)
    out_specs += job_out_specs
    out_shape += job_out_shape
    return pl.pallas_call(
        kern,
        grid=(N_TOK // TM,),
        in_specs=in_specs,
        out_specs=out_specs,
        out_shape=out_shape,
        scratch_shapes=[pltpu.VMEM((TM, D_FF), BF16)],
        compiler_params=_params("arbitrary"),
        name="ffn_proj" if proj else "ffn",
    )(*args)


def _diff_pre_kernel(*refs, rope, n_alias):
    refs = list(refs)
    x_ref, mod_ref, g_ref, w_ref = refs[:4]
    refs = refs[4:]
    hw = DIFF_HEADS * DIFF_VD
    half = DIFF_HD // 4
    if rope:
        cos_ref, sgn_ref = refs[:2]
        refs = refs[2:]
    gains = refs[:2]
    refs = refs[2:]
    if rope:
        q_ref, kt_ref, v_ref, wrot_ref = refs

        @pl.when(pl.program_id(0) == 0)
        def _():
            _swap_pair_columns(wrot_ref, w_ref, 2 * hw, half)
    else:
        q_ref, kt_ref, v_ref = refs[n_alias:]
    h = _modulate(x_ref[...], g_ref[...], mod_ref[3:4, :], mod_ref[4:5, :]).astype(BF16)
    lane = lax.broadcasted_iota(jnp.int32, (1, LANES), 1)
    lo = lane < DIFF_HD
    for part in (1, 0):
        gain = gains[part][...]
        y = _dot(h, w_ref[:, part * hw:(part + 1) * hw])
        if rope:
            y_pair = _dot(h, wrot_ref[:, part * hw:(part + 1) * hw])
            g_cos = cos_ref[...] * gain
            g_sin = sgn_ref[...] * _swap_pairs(jnp.broadcast_to(gain, (SUBLANES, LANES)), half)[0:1]
        for hd in range(DIFF_HEADS):
            sl = slice(hd * LANES, (hd + 1) * LANES)
            yh = y[:, sl]
            sq = yh * yh
            s_lo = jnp.sum(jnp.where(lo, sq, 0.0), axis=-1, keepdims=True)
            s_hi = jnp.sum(jnp.where(lo, 0.0, sq), axis=-1, keepdims=True)
            ms = jnp.where(lo, s_lo, s_hi) * (1.0 / DIFF_HD)
            if rope:
                yn = (yh * g_cos + y_pair[:, sl] * g_sin) * lax.rsqrt(ms + EPS)
            else:
                yn = yh * lax.rsqrt(ms + EPS) * gain
            if part == 0:
                q_ref[:, sl] = yn.astype(q_ref.dtype)
            elif len(kt_ref.shape) == 2:
                kt_ref[sl, :] = yn.T
            else:
                ynt = yn.T
                for b in range(SEQS_PER_TILE):
                    kt_ref[b, sl, :] = ynt[:, b * SEQ:(b + 1) * SEQ]
    v = _dot(h, w_ref[:, 2 * hw:3 * hw])
    if len(v_ref.shape) == 2:
        v_ref[...] = v
    else:
        for b in range(SEQS_PER_TILE):
            v_ref[b] = v[b * SEQ:(b + 1) * SEQ]


def _diff_pre(x, mod_l, g, w_qkv, gains, lat, rot=None, j=0, stacks=None):
    n_rows = N_LAT if lat else N_CTX
    tile0 = N_CTX // TM if lat else 0
    per_lat = DEC_SEQ // TM
    full = lambda i: (0, 0)
    xrow = lambda i: (i + tile0, 0)
    row = lambda i: (i, 0)
    if lat:
        cond = lambda i: (1 + i // per_lat, 0, 0)
    else:
        cond = lambda i: (0, 0, 0)
    in_specs = [
        pl.BlockSpec((TM, D_MODEL), xrow),
        pl.BlockSpec((None, N_MOD, D_MODEL), cond),
        pl.BlockSpec((1, D_MODEL), full),
        pl.BlockSpec((D_MODEL, 3 * D_MODEL), full),
    ]
    args = [x, mod_l, g.reshape(1, D_MODEL), w_qkv]
    aliases = {}
    scratch = []
    if lat:
        pos = lambda i: (i % per_lat, 0)
        in_specs += [pl.BlockSpec((TM, LANES), pos)] * 2 + [pl.BlockSpec((1, LANES), full)] * 2
        args += list(rot) + list(gains)
        scratch = [pltpu.VMEM((D_MODEL, 2 * D_MODEL), BF16)]
        out_specs = [pl.BlockSpec((TM, D_MODEL), row),
                     pl.BlockSpec((None, D_MODEL, TM), lambda i: (i // per_lat, 0, i % per_lat)),
                     pl.BlockSpec((TM, D_MODEL), row)]
        out_shape = [jax.ShapeDtypeStruct((N_LAT, D_MODEL), BF16),
                     jax.ShapeDtypeStruct((DEC_BATCH, D_MODEL, DEC_SEQ), F32),
                     jax.ShapeDtypeStruct((N_LAT, D_MODEL), F32)]
    else:
        in_specs += [pl.BlockSpec((1, LANES), full)] * 2
        args += list(gains)
        slot = lambda i: (i, j, 0, 0)
        out_specs = [pl.BlockSpec((TM, D_MODEL), row),
                     pl.BlockSpec((SEQS_PER_TILE, None, D_MODEL, SEQ), slot),
                     pl.BlockSpec((SEQS_PER_TILE, None, SEQ, D_MODEL), slot)]
        out_shape = [jax.ShapeDtypeStruct((N_CTX, D_MODEL), BF16),
                     jax.ShapeDtypeStruct((BATCH, N_DIFF, D_MODEL, SEQ), F32),
                     jax.ShapeDtypeStruct((BATCH, N_DIFF, SEQ, D_MODEL), F32)]
        if stacks is not None:
            aliases = {len(args): 1, len(args) + 1: 2}
            in_specs += [pl.BlockSpec(memory_space=pl.ANY)] * 2
            args += list(stacks)
    return pl.pallas_call(
        functools.partial(_diff_pre_kernel, rope=lat, n_alias=len(aliases)),
        grid=(n_rows // TM,),
        in_specs=in_specs,
        out_specs=out_specs,
        out_shape=out_shape,
        input_output_aliases=aliases,
        scratch_shapes=scratch,
        compiler_params=_params("arbitrary" if lat else "parallel"),
        name="diff_pre_lat" if lat else "diff_pre_ctx",
    )(*args)


def _softmax_pv(s, v_aug):
    e = jnp.exp2(s - jnp.max(s, axis=-1, keepdims=True)).astype(BF16)
    r = _dot(e, v_aug)
    return r[:, :LANES] * (1.0 / r[:, LANES:])


def _diff_attn_kernel(*refs, heads, lam_init, cached):
    if cached:
        q_ref, ktc_ref, kt_ref, vc_ref, v_ref, lam_ref, sub_ref, o_ref = refs
    else:
        q_ref, kt_ref, v_ref, lam_ref, sub_ref, o_ref = refs
    lp = lam_ref[...]
    lam = (jnp.exp(jnp.sum(lp[0:1, :] * lp[1:2, :], axis=-1, keepdims=True))
           - jnp.exp(jnp.sum(lp[2:3, :] * lp[3:4, :], axis=-1, keepdims=True)) + lam_init)
    n_b, tq = q_ref.shape[:2]
    lane = lax.broadcasted_iota(jnp.int32, (1, LANES), 1)
    lo = lane < DIFF_HD
    sub_g = sub_ref[...]
    items = [(b, hd) for b in range(n_b) for hd in range(heads)]

    def scores(item):
        b, hd = item
        sl = slice(hd * LANES, (hd + 1) * LANES)
        qh = q_ref[b, :, sl]
        kt = kt_ref[b, sl, :].astype(BF16)
        if cached:
            kt = jnp.concatenate([ktc_ref[b, sl, :].astype(BF16), kt], axis=1)
        zero = jnp.zeros_like(qh)
        qq = jnp.concatenate([jnp.where(lo, qh, zero), jnp.where(lo, zero, qh)], axis=0)
        return _dot(qq, kt)

    s_next = scores(items[0])
    for n, (b, hd) in enumerate(items):
        sl = slice(hd * LANES, (hd + 1) * LANES)
        s = s_next
        if n + 1 < len(items):
            s_next = scores(items[n + 1])
        vh = v_ref[b, :, sl].astype(BF16)
        if cached:
            vh = jnp.concatenate([vc_ref[b, :, sl].astype(BF16), vh], axis=0)
        v_aug = jnp.concatenate([vh, jnp.ones_like(vh)], axis=1)
        o12 = _softmax_pv(s, v_aug)
        o = o12[:tq] - lam * o12[tq:]
        ms = jnp.mean(o * o, axis=-1, keepdims=True)
        o = (o * lax.rsqrt(ms + EPS) * sub_g) * (1.0 - lam_init)
        o_ref[b, :, sl] = o.astype(o_ref.dtype)


def _diff_attn(q, kt, v, lam_p, sub_g, lam_init, heads_per_step, j, caches=None):
    b, lq, _ = q.shape
    cached = caches is not None
    nb = 1 if cached else CTX_SEQS_PER_STEP
    w = heads_per_step * LANES
    n_hg = DIFF_HEADS // heads_per_step
    tq = min(TQ, lq) if lq == SEQ else LAT_TQ
    qmap = lambda bi, hi, qi: (bi, qi, hi)
    full = lambda bi, hi, qi: (0, 0)
    q_spec = pl.BlockSpec((nb, tq, w), qmap)
    kt_slot = lambda bi, hi, qi: (bi, j, hi, 0)
    v_slot = lambda bi, hi, qi: (bi, j, 0, hi)
    if cached:
        kt_cache, v_cache = caches
        in_specs = [q_spec,
                    pl.BlockSpec((nb, None, w, PAST_LEN), kt_slot),
                    pl.BlockSpec((nb, w, lq), lambda bi, hi, qi: (bi, hi, 0)),
                    pl.BlockSpec((nb, None, PAST_LEN, w), v_slot),
                    pl.BlockSpec((nb, lq, w), lambda bi, hi, qi: (bi, 0, hi))]
        args = [q, kt_cache, kt, v_cache, v]
    else:
        in_specs = [q_spec,
                    pl.BlockSpec((nb, None, w, lq), kt_slot),
                    pl.BlockSpec((nb, None, lq, w), v_slot)]
        args = [q, kt, v]
    in_specs += [pl.BlockSpec((4, DIFF_HD), full), pl.BlockSpec((1, LANES), full)]
    args += [lam_p, sub_g]
    return pl.pallas_call(
        functools.partial(_diff_attn_kernel, heads=heads_per_step, lam_init=lam_init, cached=cached),
        grid=(b // nb, n_hg, lq // tq),
        in_specs=in_specs,
        out_specs=pl.BlockSpec((nb, tq, w), qmap),
        out_shape=jax.ShapeDtypeStruct((b, lq, D_MODEL), BF16),
        compiler_params=_params("parallel", "parallel", "parallel"),
        name="diff_attn_lat" if cached else "diff_attn_ctx",
    )(*args)


def _mla_keys(ckv_n, kpe, wk_ref, wv_ref, kg, rope_refs, k_ref, v_ref):
    c = ckv_n.astype(BF16)
    kk = _dot(c, wk_ref[...])
    kpe = pltpu.roll(kpe, MLA_NOPE, 1)
    pe_ss = jnp.sum(kpe * kpe, axis=-1, keepdims=True)
    kpe_g = kpe * kg
    if rope_refs is not None:
        cos_ref, up_ref, dn_ref = rope_refs
        kpe_g = _rope(kpe_g, cos_ref[...], up_ref[...], dn_ref[...], MLA_ROPE // 4)
    for hd in range(MLA_HEADS):
        sl = slice(hd * LANES, (hd + 1) * LANES)
        kh = kk[:, sl]
        ms = (jnp.sum(kh * kh, axis=-1, keepdims=True) + pe_ss) * (1.0 / MLA_QK)
        kn = (kh * kg + kpe_g) * lax.rsqrt(ms + EPS)
        k_ref[:, sl] = kn.astype(k_ref.dtype)
    v_ref[...] = _dot(c, wv_ref[...]).astype(v_ref.dtype)


def _mla_pre_kernel(*refs, lat, n_alias):
    refs = list(refs)
    x_ref, mod_ref, g_ref, wdq_ref, wdkv_ref, qag_ref, kvag_ref, wq_ref = refs[:8]
    refs = refs[8:]
    half = MLA_ROPE // 4
    if lat:
        cos_ref, sgn_ref, qg_ref = refs[:3]
        q_ref, ckv_ref, kpe_ref, wqrot_ref = refs[3:]

        @pl.when(pl.program_id(0) == 0)
        def _():
            _swap_pair_columns(wqrot_ref, wq_ref, MLA_HEADS * LANES, half)
    else:
        qg_ref, wk_ref, wv_ref, kg_ref = refs[:4]
        q_ref, k_ref, v_ref, ckvs_ref, kpet_ref = refs[4 + n_alias:]
    h = _modulate(x_ref[...], g_ref[...], mod_ref[3:4, :], mod_ref[4:5, :]).astype(BF16)
    d2 = _dot(h, wdkv_ref[...])
    ckv = d2[:, :KV_RANK]
    ms = jnp.mean(ckv * ckv, axis=-1, keepdims=True)
    ckv_n = ckv * lax.rsqrt(ms + EPS) * kvag_ref[...]
    kpe = d2[:, KV_RANK:]
    if lat:
        ckv_ref[...] = ckv_n
        kpe_ref[...] = kpe
    cq = _dot(h, wdq_ref[...])
    ms = jnp.mean(cq * cq, axis=-1, keepdims=True)
    cqn = (cq * lax.rsqrt(ms + EPS) * qag_ref[...]).astype(BF16)
    q = _dot(cqn, wq_ref[...])
    qg = qg_ref[...]
    if lat:
        q_pair = _dot(cqn, wqrot_ref[...])
        g_cos = cos_ref[...] * qg
        g_sin = sgn_ref[...] * _swap_pairs(jnp.broadcast_to(qg, (SUBLANES, LANES)), half)[0:1]
    for hd in range(MLA_HEADS):
        sl = slice(hd * LANES, (hd + 1) * LANES)
        qh = q[:, sl]
        ms = jnp.sum(qh * qh, axis=-1, keepdims=True) * (1.0 / MLA_QK)
        if lat:
            qn = (qh * g_cos + q_pair[:, sl] * g_sin) * lax.rsqrt(ms + EPS)
        else:
            qn = qh * lax.rsqrt(ms + EPS) * qg
        q_ref[:, sl] = qn.astype(q_ref.dtype)
    if not lat:
        kpe_t = kpe.T
        for b in range(SEQS_PER_TILE):
            ckvs_ref[b] = ckv_n[b * SEQ:(b + 1) * SEQ]
            kpet_ref[b] = kpe_t[:MLA_ROPE, b * SEQ:(b + 1) * SEQ]
        _mla_keys(ckv_n, kpe, wk_ref, wv_ref, kg_ref[...], None, k_ref, v_ref)


def _mla_pre(x, mod_l, g, w_dq, w_dkv, qag, kvag, w_q, lat, rot=None, ctx_weights=None,
             j=0, stacks=None):
    n_rows = N_LAT if lat else N_CTX
    tile0 = N_CTX // TM if lat else 0
    per_lat = DEC_SEQ // TM
    full = lambda i: (0, 0)
    xrow = lambda i: (i + tile0, 0)
    row = lambda i: (i, 0)
    if lat:
        cond = lambda i: (1 + i // per_lat, 0, 0)
    else:
        cond = lambda i: (0, 0, 0)
    kvw = KV_RANK + LANES
    qw = MLA_HEADS * LANES
    vw = MLA_HEADS * MLA_VD
    in_specs = [
        pl.BlockSpec((TM, D_MODEL), xrow),
        pl.BlockSpec((None, N_MOD, D_MODEL), cond),
        pl.BlockSpec((1, D_MODEL), full),
        pl.BlockSpec((D_MODEL, Q_RANK), full),
        pl.BlockSpec((D_MODEL, kvw), full),
        pl.BlockSpec((1, Q_RANK), full),
        pl.BlockSpec((1, KV_RANK), full),
        pl.BlockSpec((Q_RANK, qw), full),
    ]
    args = [x, mod_l, g.reshape(1, D_MODEL), w_dq, w_dkv, qag.reshape(1, Q_RANK),
            kvag.reshape(1, KV_RANK), w_q]
    aliases = {}
    scratch = []
    if lat:
        pos = lambda i: (i % per_lat, 0)
        in_specs += [pl.BlockSpec((TM, LANES), pos)] * 2 + [pl.BlockSpec((1, LANES), full)]
        args += list(rot)
        scratch = [pltpu.VMEM((Q_RANK, qw), BF16)]
        out_specs = [pl.BlockSpec((TM, qw), row), pl.BlockSpec((TM, KV_RANK), row),
                     pl.BlockSpec((TM, LANES), row)]
        out_shape = [jax.ShapeDtypeStruct((n_rows, qw), BF16),
                     jax.ShapeDtypeStruct((n_rows, KV_RANK), F32),
                     jax.ShapeDtypeStruct((n_rows, LANES), F32)]
    else:
        qg, w_k, w_v, kg = ctx_weights
        in_specs += [pl.BlockSpec((1, LANES), full), pl.BlockSpec((KV_RANK, qw), full),
                     pl.BlockSpec((KV_RANK, vw), full), pl.BlockSpec((1, LANES), full)]
        args += [qg, w_k, w_v, kg]
        slot = lambda i: (i, j, 0, 0)
        out_specs = [pl.BlockSpec((TM, qw), row), pl.BlockSpec((TM, qw), row),
                     pl.BlockSpec((TM, vw), row),
                     pl.BlockSpec((SEQS_PER_TILE, None, SEQ, KV_RANK), slot),
                     pl.BlockSpec((SEQS_PER_TILE, None, MLA_ROPE, SEQ), slot)]
        out_shape = [jax.ShapeDtypeStruct((n_rows, qw), BF16),
                     jax.ShapeDtypeStruct((n_rows, qw), BF16),
                     jax.ShapeDtypeStruct((n_rows, vw), BF16),
                     jax.ShapeDtypeStruct((BATCH, N_MLA, SEQ, KV_RANK), F32),
                     jax.ShapeDtypeStruct((BATCH, N_MLA, MLA_ROPE, SEQ), F32)]
        if stacks is not None:
            aliases = {len(args): 3, len(args) + 1: 4}
            in_specs += [pl.BlockSpec(memory_space=pl.ANY)] * 2
            args += list(stacks)
    return pl.pallas_call(
        functools.partial(_mla_pre_kernel, lat=lat, n_alias=len(aliases)),
        grid=(n_rows // TM,),
        in_specs=in_specs,
        out_specs=out_specs,
        out_shape=out_shape,
        input_output_aliases=aliases,
        scratch_shapes=scratch,
        compiler_params=_params("arbitrary" if lat else "parallel"),
        name="mla_pre_lat" if lat else "mla_pre_ctx",
    )(*args)


def _mla_kv_kernel(ckv_ref, kpe_ref, wk_ref, wv_ref, kg_ref, cos_ref, up_ref, dn_ref, k_ref, v_ref):
    _mla_keys(ckv_ref[...], kpe_ref[...], wk_ref, wv_ref, kg_ref[...],
              (cos_ref, up_ref, dn_ref), k_ref, v_ref)


def _mla_kv_lat(ckv_rows, kpe_rows, w_k, w_v, kg, tables):
    n_rows = ckv_rows.shape[0]
    tm = KVP_TM_LAT
    per = LAT_KV // tm
    full = lambda i: (0, 0)
    row = lambda i: (i, 0)
    pos = lambda i: (i % per, 0)
    kw = MLA_HEADS * LANES
    vw = MLA_HEADS * MLA_VD
    return pl.pallas_call(
        _mla_kv_kernel,
        grid=(n_rows // tm,),
        in_specs=[
            pl.BlockSpec((tm, KV_RANK), row),
            pl.BlockSpec((tm, LANES), row),
            pl.BlockSpec((KV_RANK, kw), full),
            pl.BlockSpec((KV_RANK, vw), full),
            pl.BlockSpec((1, LANES), full),
        ] + [pl.BlockSpec((tm, LANES), pos)] * 3,
        out_specs=[pl.BlockSpec((tm, kw), row), pl.BlockSpec((tm, vw), row)],
        out_shape=[jax.ShapeDtypeStruct((n_rows, kw), BF16),
                   jax.ShapeDtypeStruct((n_rows, vw), BF16)],
        compiler_params=_params("parallel"),
        name="mla_kv_lat",
    )(ckv_rows, kpe_rows, w_k, w_v, kg, *tables)


def _mla_attn_kernel(q_ref, k_ref, v_ref, o_ref, *, pairs):
    lane = lax.broadcasted_iota(jnp.int32, (1, LANES), 1)
    lo = lane < MLA_VD

    items = [(b, hd) for b in range(q_ref.shape[0]) for hd in range(2 * pairs)]

    def scores(item):
        b, hd = item
        sl = slice(hd * LANES, (hd + 1) * LANES)
        return _dot_nt(q_ref[b, :, sl], k_ref[b, :, sl])

    s_next = scores(items[0])
    outs = []
    for n, (b, hd) in enumerate(items):
        s = s_next
        if n + 1 < len(items):
            s_next = scores(items[n + 1])
        pl_ = slice((hd // 2) * LANES, (hd // 2 + 1) * LANES)
        vp = v_ref[b, :, pl_]
        outs.append(_softmax_pv(s, jnp.concatenate([vp, jnp.ones_like(vp)], axis=1)))
        if hd % 2 == 1:
            o_ref[b, :, pl_] = jnp.where(lo, outs[0], outs[1]).astype(o_ref.dtype)
            outs = []


def _mla_attn(q, k, v, pairs_per_step):
    b, lq, _ = q.shape
    lk = k.shape[1]
    nb = MLA_CTX_SEQS_PER_STEP if lq == SEQ else 1
    n_pg = MLA_HEADS // 2 // pairs_per_step
    tq = min(TQ, lq) if lq == SEQ else LAT_TQ
    qkw = pairs_per_step * 2 * LANES
    ow = pairs_per_step * LANES
    qmap = lambda bi, hi, qi: (bi, qi, hi)
    kmap = lambda bi, hi, qi: (bi, 0, hi)
    return pl.pallas_call(
        functools.partial(_mla_attn_kernel, pairs=pairs_per_step),
        grid=(b // nb, n_pg, lq // tq),
        in_specs=[
            pl.BlockSpec((nb, tq, qkw), qmap),
            pl.BlockSpec((nb, lk, qkw), kmap),
            pl.BlockSpec((nb, lk, ow), kmap),
        ],
        out_specs=pl.BlockSpec((nb, tq, ow), qmap),
        out_shape=jax.ShapeDtypeStruct((b, lq, D_MODEL), BF16),
        compiler_params=_params("parallel", "parallel", "parallel"),
        name="mla_attn_lat" if lq == DEC_SEQ else "mla_attn_ctx",
    )(q, k, v)


def _rope_tables(n_rot, lane0, identity_rows):
    rows = DEC_SEQ // GRID_W
    row = np.repeat(np.arange(rows), GRID_W)
    col = np.tile(np.arange(GRID_W), rows)
    n = n_rot // 2
    freqs = ROPE_THETA ** (-np.arange(0, n, 2, dtype=np.float64) / n)
    zeros = np.zeros((DEC_SEQ, n // 2))
    cos_parts, up_parts, dn_parts = [], [], []
    for pos in (row, col):
        ang = pos.astype(np.float64)[:, None] * freqs[None, :]
        c, s = np.cos(ang), np.sin(ang)
        cos_parts += [c, c]
        up_parts += [-s, zeros]
        dn_parts += [zeros, s]

    def place(parts, fill):
        reps = 2 if lane0 == 0 else 1
        body = np.concatenate(parts * reps, axis=1)
        full = np.full((identity_rows + DEC_SEQ, LANES), fill)
        full[identity_rows:, lane0:lane0 + body.shape[1]] = body
        return jnp.asarray(full, dtype=F32)

    return place(cos_parts, 1.0), place(up_parts, 0.0), place(dn_parts, 0.0)


def _signed_tables(tables):
    cos, up, dn = tables
    return cos, up + dn


def kernel(x_prompt, x_sample, c, cache_diff_k, cache_diff_v, cache_mla_ckv, cache_mla_kpe, c_ctx, w_mod, b_mod, norm_g, ffn_w_in, ffn_w_out, diff_w_qkv, diff_q_norm, diff_k_norm, diff_lambda, diff_subln, diff_w_o, mla_w_down, mla_q_a_norm, mla_kv_a_norm, mla_w_q_up, mla_w_kv_up, mla_q_norm, mla_k_norm, mla_w_o):
    xs = (x_prompt.reshape(N_CTX, D_MODEL), x_sample.reshape(N_LAT, D_MODEL))
    conds = jnp.concatenate(
        [c_ctx[None, :], c, jnp.zeros((COND_ROWS - 1 - DEC_BATCH, D_MODEL), F32)], axis=0)
    mod = _modulation(conds, w_mod, b_mod)

    w_in = ffn_w_in[0, 0].astype(BF16)
    w_out = ffn_w_out[0, 0].astype(BF16)
    diff_rot = _signed_tables(_rope_tables(DIFF_HD, 0, 0))
    mla_rot = _signed_tables(_rope_tables(MLA_ROPE, MLA_NOPE, 0))
    mla_kv_tabs = _rope_tables(MLA_ROPE, MLA_NOPE, PAST_LEN)
    cache_kt = jnp.transpose(cache_diff_k, (0, 1, 3, 4, 5, 2)).reshape(
        DEC_BATCH, N_DIFF, D_MODEL, PAST_LEN)
    cache_v = cache_diff_v.reshape(DEC_BATCH, N_DIFF, PAST_LEN, D_MODEL)

    diff_stacks, mla_stacks = None, None
    for l in range(DEPTH):
        j = l // 2
        mod_l = mod[l]
        mixer_jobs = ([(diff_w_qkv, (j,)), (diff_w_o, (j,))] if l % 2 == 0 else [(mla_w_o, (j,))])
        x, w_in, w_out, *mixer_w = _ffn(
            xs, mod_l, norm_g[l, 0], w_in, w_out, 0,
            jobs=[(ffn_w_in, (l, 1)), (ffn_w_out, (l, 1))] + mixer_jobs)
        w_o = mixer_w[-1]
        g1 = norm_g[l, 1]
        if l % 2 == 0:
            w_qkv = mixer_w[0]
            qg = jnp.tile(diff_q_norm[j] * DIFF_Q_SCALE, 2).reshape(1, LANES)
            kg = jnp.tile(diff_k_norm[j], 2).reshape(1, LANES)
            sub_g = diff_subln[j].reshape(1, LANES)
            lam_init = 0.8 - 0.6 * math.exp(-0.3 * l)
            q_c, kt_stack, v_stack = _diff_pre(x, mod_l, g1, w_qkv, (qg, kg), lat=False,
                                               j=j, stacks=diff_stacks)
            diff_stacks = (kt_stack, v_stack)
            q_l, kt_l, v_l = _diff_pre(x, mod_l, g1, w_qkv, (qg, kg), lat=True, rot=diff_rot)
            o_c = _diff_attn(q_c.reshape(BATCH, SEQ, D_MODEL), kt_stack, v_stack, diff_lambda[j],
                             sub_g, lam_init, heads_per_step=DIFF_HEADS, j=j)
            o_l = _diff_attn(q_l.reshape(DEC_BATCH, DEC_SEQ, D_MODEL), kt_l,
                             v_l.reshape(DEC_BATCH, DEC_SEQ, D_MODEL), diff_lambda[j], sub_g,
                             lam_init, heads_per_step=LAT_HEADS_PER_STEP, j=j,
                             caches=(cache_kt, cache_v))
        else:
            wd = mla_w_down[j]
            w_dq = wd[:, :Q_RANK].astype(BF16)
            w_dkv = jnp.pad(wd[:, Q_RANK:], ((0, 0), (0, LANES - MLA_ROPE))).astype(BF16)
            w_q = jnp.pad(mla_w_q_up[j].reshape(Q_RANK, MLA_HEADS, MLA_QK),
                          ((0, 0), (0, 0), (0, LANES - MLA_QK))).reshape(Q_RANK, MLA_HEADS * LANES).astype(BF16)
            wkv = mla_w_kv_up[j].reshape(KV_RANK, MLA_HEADS, MLA_NOPE + MLA_VD)
            w_k = jnp.pad(wkv[:, :, :MLA_NOPE], ((0, 0), (0, 0), (0, LANES - MLA_NOPE))
                          ).reshape(KV_RANK, MLA_HEADS * LANES).astype(BF16)
            w_v = wkv[:, :, MLA_NOPE:].reshape(KV_RANK, MLA_HEADS * MLA_VD).astype(BF16)
            qg = jnp.pad(mla_q_norm[j] * MLA_Q_SCALE, (0, LANES - MLA_QK)).reshape(1, LANES)
            kg = jnp.pad(mla_k_norm[j], (0, LANES - MLA_QK)).reshape(1, LANES)
            pre = functools.partial(_mla_pre, x, mod_l, g1, w_dq, w_dkv, mla_q_a_norm[j],
                                    mla_kv_a_norm[j], w_q)
            q_c, k_c, v_c, ckv_stack, kpet_stack = pre(lat=False, ctx_weights=(qg, w_k, w_v, kg),
                                                      j=j, stacks=mla_stacks)
            mla_stacks = (ckv_stack, kpet_stack)
            q_l, ckv_l, kpe_l = pre(lat=True, rot=mla_rot + (qg,))
            cache_kpe = jnp.pad(cache_mla_kpe[:, j], ((0, 0), (0, 0), (0, LANES - MLA_ROPE)))
            ckv_rows = jnp.concatenate([cache_mla_ckv[:, j], ckv_l.reshape(DEC_BATCH, DEC_SEQ, KV_RANK)],
                                       axis=1).reshape(DEC_BATCH * LAT_KV, KV_RANK)
            kpe_rows = jnp.concatenate([cache_kpe, kpe_l.reshape(DEC_BATCH, DEC_SEQ, LANES)],
                                       axis=1).reshape(DEC_BATCH * LAT_KV, LANES)
            k_a, v_a = _mla_kv_lat(ckv_rows, kpe_rows, w_k, w_v, kg, mla_kv_tabs)
            kw = MLA_HEADS * LANES
            o_c = _mla_attn(q_c.reshape(BATCH, SEQ, kw), k_c.reshape(BATCH, SEQ, kw),
                            v_c.reshape(BATCH, SEQ, D_MODEL), pairs_per_step=MLA_HEADS // 2)
            o_l = _mla_attn(q_l.reshape(DEC_BATCH, DEC_SEQ, kw), k_a.reshape(DEC_BATCH, LAT_KV, kw),
                            v_a.reshape(DEC_BATCH, LAT_KV, D_MODEL),
                            pairs_per_step=LAT_HEADS_PER_STEP)
        os = (o_c.reshape(N_CTX, D_MODEL), o_l.reshape(N_LAT, D_MODEL))
        last = l == DEPTH - 1
        xs = _ffn((x,), mod_l, norm_g[l, 2], w_in, w_out, 1, os=os, w_o=w_o, split_out=last,
                  jobs=[] if last else [(ffn_w_in, (l + 1, 0)), (ffn_w_out, (l + 1, 0))])
        if not last:
            xs, (w_in, w_out) = xs[:1], xs[1:]

    kt_stack, v_stack = diff_stacks
    ckv_stack, kpet_stack = mla_stacks
    new_diff_k = jnp.transpose(kt_stack.reshape(BATCH, N_DIFF, DIFF_HEADS, 2, DIFF_HD, SEQ),
                               (0, 1, 5, 2, 3, 4))
    new_diff_v = v_stack.reshape(BATCH, N_DIFF, SEQ, DIFF_HEADS, DIFF_VD)
    new_mla_kpe = jnp.transpose(kpet_stack, (0, 1, 3, 2))
    return (xs[0].reshape(BATCH, SEQ, D_MODEL), xs[1].reshape(DEC_BATCH, DEC_SEQ, D_MODEL),
            new_diff_k, new_diff_v, ckv_stack, new_mla_kpe)
```
